```python
import math
import jax, jax.numpy as jnp
from jax import lax
import numpy as np

D_MODEL = 1024
BATCH = 1
SEQ = 16384
DEPTH = 1

PLE_DIM = 256
D_FF = 2816
EPS = 1e-6
MLA_HEADS = 8
MLA_Q_RANK = 256
MLA_KV_RANK = 256
MLA_NOPE = 64
MLA_ROPE = 32
MLA_V = 64
ROPE_THETA = 10000.0
Q_BLOCK = 128
ML_HEADS = 4
ML_QK = 64
ML_V = 128
ML_CHUNK = 128
CONV_W = 5
N_DIR = 2
MLA_OUT = MLA_HEADS * MLA_V
ML_OUT = ML_HEADS * ML_V
IN_SPLITS = (MLA_Q_RANK, MLA_KV_RANK, MLA_ROPE, ML_HEADS * ML_QK, ML_HEADS * ML_QK, ML_OUT,
             N_DIR * ML_HEADS, N_DIR * ML_HEADS, ML_OUT, 2 * D_MODEL)
IN_WIDTH = sum(IN_SPLITS)

kernel_name = 'hybrid_mla_mlstm_macaron_encoder'


def rmsnorm(x, w):
    xf = x.astype(jnp.float32)
    y = xf * lax.rsqrt(jnp.mean(xf * xf, axis=-1, keepdims=True) + EPS)
    return (y * w.astype(jnp.float32)).astype(x.dtype)


def swiglu(x, w_gate, w_up, w_down):
    return (jax.nn.silu(x @ w_gate) * (x @ w_up)) @ w_down


def apply_rope(x, cos, sin):
    half = x.shape[-1] // 2
    x1, x2 = x[..., :half], x[..., half:]
    return jnp.concatenate([x1 * cos - x2 * sin, x2 * cos + x1 * sin], axis=-1)


def depthwise_conv(x, w, b):
    C = x.shape[-1]
    y = lax.conv_general_dilated(x, w[:, None, :].astype(x.dtype), window_strides=(1,),
                                 padding=[(CONV_W // 2, CONV_W // 2)],
                                 dimension_numbers=('NWC', 'WIO', 'NWC'),
                                 feature_group_count=C)
    return y + b


def mla_branch(c_q, c_kv, k_r, cos, sin, q_norm, kv_norm, w_uq, w_uk, w_uv):
    B, S, _ = c_q.shape
    q = (rmsnorm(c_q, q_norm) @ w_uq).reshape(B, S, MLA_HEADS, MLA_NOPE + MLA_ROPE)
    q_nope = q[..., :MLA_NOPE]
    q_rope = apply_rope(q[..., MLA_NOPE:], cos[:, :, None, :], sin[:, :, None, :])
    ckv = rmsnorm(c_kv, kv_norm)
    k_nope = (ckv @ w_uk).reshape(B, S, MLA_HEADS, MLA_NOPE)
    v = (ckv @ w_uv).reshape(B, S, MLA_HEADS, MLA_V)
    k_rope = apply_rope(k_r, cos, sin)
    scale = (MLA_NOPE + MLA_ROPE) ** -0.5
    nb = S // Q_BLOCK
    qn_b = q_nope.reshape(B, nb, Q_BLOCK, MLA_HEADS, MLA_NOPE).transpose(1, 0, 2, 3, 4)
    qr_b = q_rope.reshape(B, nb, Q_BLOCK, MLA_HEADS, MLA_ROPE).transpose(1, 0, 2, 3, 4)

    def block(args):
        qn, qr = args
        s = (jnp.einsum('bqhd,bkhd->bhqk', qn, k_nope)
             + jnp.einsum('bqhd,bkd->bhqk', qr, k_rope))
        pr = jax.nn.softmax(s.astype(jnp.float32) * scale, axis=-1).astype(v.dtype)
        return jnp.einsum('bhqk,bkhd->bqhd', pr, v)

    o = lax.map(block, (qn_b, qr_b))
    return o.transpose(1, 0, 2, 3, 4).reshape(B, S, MLA_OUT)


def mlstm_chunkwise(q, k, v, li, lf):
    dtype = v.dtype
    q, k, v, li, lf = (t.astype(jnp.float32) for t in (q, k, v, li, lf))
    N, H, S, dk = q.shape
    dv = v.shape[-1]
    L = ML_CHUNK
    nc = S // L
    qc = q.reshape(N, H, nc, L, dk).transpose(2, 0, 1, 3, 4)
    kc = k.reshape(N, H, nc, L, dk).transpose(2, 0, 1, 3, 4)
    vc = v.reshape(N, H, nc, L, dv).transpose(2, 0, 1, 3, 4)
    ic = li.reshape(N, H, nc, L).transpose(2, 0, 1, 3)
    fc = lf.reshape(N, H, nc, L).transpose(2, 0, 1, 3)
    mask = jnp.tril(jnp.ones((L, L), dtype=bool))

    def step(carry, inp):
        C, n, m = carry
        qb, kb, vb, ib, fb = inp
        b = jnp.cumsum(fb, axis=-1)
        dmat = jnp.where(mask, b[..., :, None] - b[..., None, :] + ib[..., None, :], -jnp.inf)
        m_t = jnp.maximum(b + m[..., None], jnp.max(dmat, axis=-1))
        inter = jnp.exp(b + m[..., None] - m_t)
        s = jnp.einsum('nhtd,nhsd->nhts', qb, kb) * jnp.exp(dmat - m_t[..., None])
        num = (jnp.einsum('nhts,nhsv->nhtv', s, vb)
               + inter[..., None] * jnp.einsum('nhtd,nhdv->nhtv', qb, C))
        den = jnp.sum(s, axis=-1) + inter * jnp.einsum('nhtd,nhd->nht', qb, n)
        h = num / jnp.maximum(jnp.abs(den), jnp.exp(-m_t))[..., None]
        b_end = b[..., -1]
        g = b_end[..., None] - b + ib
        m_new = jnp.maximum(b_end + m, jnp.max(g, axis=-1))
        decay = jnp.exp(b_end + m - m_new)
        wgt = jnp.exp(g - m_new[..., None])
        C_new = decay[..., None, None] * C + jnp.einsum('nhs,nhsd,nhsv->nhdv', wgt, kb, vb)
        n_new = decay[..., None] * n + jnp.einsum('nhs,nhsd->nhd', wgt, kb)
        return (C_new, n_new, m_new), h

    init = (jnp.zeros((N, H, dk, dv), jnp.float32), jnp.zeros((N, H, dk), jnp.float32),
            jnp.zeros((N, H), jnp.float32))
    _, hs = lax.scan(step, init, (qc, kc, vc, ic, fc))
    return hs.transpose(1, 2, 0, 3, 4).reshape(N, H, S, dv).astype(dtype)


def mlstm_branch(ml_q, ml_k, ml_v, ml_i, ml_f, ml_o, conv_w, conv_b, i_bias, f_bias, head_norm):
    B, S, _ = ml_q.shape
    qk = jax.nn.silu(depthwise_conv(jnp.concatenate([ml_q, ml_k], axis=-1), conv_w, conv_b))
    q, k = jnp.split(qk, 2, axis=-1)
    q = q.reshape(B, S, ML_HEADS, ML_QK).transpose(0, 2, 1, 3)
    k = k.reshape(B, S, ML_HEADS, ML_QK).transpose(0, 2, 1, 3) * (ML_QK ** -0.5)
    v = ml_v.reshape(B, S, ML_HEADS, ML_V).transpose(0, 2, 1, 3)
    li = (ml_i.reshape(B, S, N_DIR, ML_HEADS) + i_bias).transpose(2, 0, 3, 1)
    lf = jax.nn.log_sigmoid(ml_f.reshape(B, S, N_DIR, ML_HEADS) + f_bias).transpose(2, 0, 3, 1)
    qd = jnp.stack([q, q[:, :, ::-1]]).reshape(N_DIR * B, ML_HEADS, S, ML_QK)
    kd = jnp.stack([k, k[:, :, ::-1]]).reshape(N_DIR * B, ML_HEADS, S, ML_QK)
    vd = jnp.stack([v, v[:, :, ::-1]]).reshape(N_DIR * B, ML_HEADS, S, ML_V)
    lid = jnp.stack([li[0], li[1][..., ::-1]]).reshape(N_DIR * B, ML_HEADS, S)
    lfd = jnp.stack([lf[0], lf[1][..., ::-1]]).reshape(N_DIR * B, ML_HEADS, S)
    hs = mlstm_chunkwise(qd, kd, vd, lid, lfd).reshape(N_DIR, B, ML_HEADS, S, ML_V)
    h = (hs[0] + hs[1][:, :, ::-1]).transpose(0, 2, 1, 3)
    h = rmsnorm(h, head_norm.reshape(ML_HEADS, ML_V)).reshape(B, S, ML_OUT)
    return h * jax.nn.sigmoid(ml_o)


def setup_inputs(seed: int = 0) -> dict:
    key = jax.random.key(seed)
    k = jax.random.split(key, 40)

    def w(kk, shape, fan_in):
        return jax.random.normal(kk, (DEPTH,) + shape, jnp.float32) * (fan_in ** -0.5)

    def gain(kk, n):
        return 1.0 + 0.05 * jax.random.normal(kk, (DEPTH, n), jnp.float32)

    x = jax.random.normal(k[0], (BATCH, SEQ, D_MODEL), jnp.float32)
    p = jax.random.normal(k[1], (DEPTH, BATCH, SEQ, PLE_DIM), jnp.float32)
    positions = (jnp.arange(SEQ, dtype=jnp.int32)[None, :]
                 + jax.random.randint(k[2], (BATCH, 1), 0, 1024, dtype=jnp.int32))
    return {
        'x': x, 'p': p, 'positions': positions,
        'ffn1_pre_norm': gain(k[3], D_MODEL), 'ffn1_post_norm': gain(k[4], D_MODEL),
        'ffn1_w_gate': w(k[5], (D_MODEL, D_FF), D_MODEL), 'ffn1_w_up': w(k[6], (D_MODEL, D_FF), D_MODEL),
        'ffn1_w_down': w(k[7], (D_FF, D_MODEL), D_FF),
        'mix_pre_norm': gain(k[8], D_MODEL), 'mix_post_norm': gain(k[9], D_MODEL),
        'w_in': w(k[10], (D_MODEL, IN_WIDTH), D_MODEL),
        'mla_q_norm': gain(k[11], MLA_Q_RANK), 'mla_kv_norm': gain(k[12], MLA_KV_RANK),
        'mla_w_uq': w(k[13], (MLA_Q_RANK, MLA_HEADS * (MLA_NOPE + MLA_ROPE)), MLA_Q_RANK),
        'mla_w_uk': w(k[14], (MLA_KV_RANK, MLA_HEADS * MLA_NOPE), MLA_KV_RANK),
        'mla_w_uv': w(k[15], (MLA_KV_RANK, MLA_HEADS * MLA_V), MLA_KV_RANK),
        'ml_conv_w': w(k[16], (CONV_W, 2 * ML_HEADS * ML_QK), CONV_W),
        'ml_conv_b': 0.02 * jax.random.normal(k[17], (DEPTH, 2 * ML_HEADS * ML_QK), jnp.float32),
        'ml_i_bias': -1.0 + 0.1 * jax.random.normal(k[18], (DEPTH, N_DIR, ML_HEADS), jnp.float32),
        'ml_f_bias': 3.0 + 0.5 * jax.random.normal(k[19], (DEPTH, N_DIR, ML_HEADS), jnp.float32),
        'ml_head_norm': gain(k[20], ML_OUT),
        'w_branch_mla': w(k[21], (MLA_OUT, D_MODEL), MLA_OUT),
        'w_branch_ml': w(k[22], (ML_OUT, D_MODEL), ML_OUT),
        'w_out': w(k[23], (D_MODEL, D_MODEL), D_MODEL),
        'ffn2_pre_norm': gain(k[24], D_MODEL), 'ffn2_post_norm': gain(k[25], D_MODEL),
        'ffn2_w_gate': w(k[26], (D_MODEL, D_FF), D_MODEL), 'ffn2_w_up': w(k[27], (D_MODEL, D_FF), D_MODEL),
        'ffn2_w_down': w(k[28], (D_FF, D_MODEL), D_FF),
        'ple_pre_norm': gain(k[29], D_MODEL), 'ple_post_norm': gain(k[30], D_MODEL),
        'ple_w_proj': w(k[31], (PLE_DIM, D_MODEL), PLE_DIM),
        'ple_w_gate': w(k[32], (D_MODEL, D_MODEL), D_MODEL),
    }


def reference(x, p, positions,
              ffn1_pre_norm, ffn1_post_norm, ffn1_w_gate, ffn1_w_up, ffn1_w_down,
              mix_pre_norm, mix_post_norm, w_in,
              mla_q_norm, mla_kv_norm, mla_w_uq, mla_w_uk, mla_w_uv,
              ml_conv_w, ml_conv_b, ml_i_bias, ml_f_bias, ml_head_norm,
              w_branch_mla, w_branch_ml, w_out,
              ffn2_pre_norm, ffn2_post_norm, ffn2_w_gate, ffn2_w_up, ffn2_w_down,
              ple_pre_norm, ple_post_norm, ple_w_proj, ple_w_gate):
    inv_freq = ROPE_THETA ** (-jnp.arange(0, MLA_ROPE, 2, dtype=jnp.float32) / MLA_ROPE)
    ang = positions.astype(jnp.float32)[..., None] * inv_freq
    cos = jnp.cos(ang).astype(x.dtype)
    sin = jnp.sin(ang).astype(x.dtype)
    split_idx = [int(s) for s in np.cumsum(IN_SPLITS)[:-1]]

    h = x
    for i in range(DEPTH):
        u = rmsnorm(h, ffn1_pre_norm[i])
        h = h + 0.5 * rmsnorm(swiglu(u, ffn1_w_gate[i], ffn1_w_up[i], ffn1_w_down[i]), ffn1_post_norm[i])

        u = rmsnorm(h, mix_pre_norm[i])
        z = u @ w_in[i]
        c_q, c_kv, k_r, ml_q, ml_k, ml_v, ml_i, ml_f, ml_o, gates = jnp.split(z, split_idx, axis=-1)
        a = mla_branch(c_q, c_kv, k_r, cos, sin, mla_q_norm[i], mla_kv_norm[i],
                       mla_w_uq[i], mla_w_uk[i], mla_w_uv[i]) @ w_branch_mla[i]
        bm = mlstm_branch(ml_q, ml_k, ml_v, ml_i, ml_f, ml_o, ml_conv_w[i], ml_conv_b[i],
                          ml_i_bias[i], ml_f_bias[i], ml_head_norm[i]) @ w_branch_ml[i]
        g_a, g_b = jnp.split(jax.nn.sigmoid(gates), 2, axis=-1)
        mixed = (g_a * a + g_b * bm) @ w_out[i]
        h = h + rmsnorm(mixed, mix_post_norm[i])

        u = rmsnorm(h, ffn2_pre_norm[i])
        h = h + 0.5 * rmsnorm(swiglu(u, ffn2_w_gate[i], ffn2_w_up[i], ffn2_w_down[i]), ffn2_post_norm[i])

        e = p[i] @ ple_w_proj[i]
        g = jax.nn.sigmoid(rmsnorm(h, ple_pre_norm[i]) @ ple_w_gate[i])
        h = h + rmsnorm(g * e, ple_post_norm[i])
    return h
```

```python
import functools
import math

import jax
import jax.numpy as jnp
from jax import lax
from jax.experimental import pallas as pl
from jax.experimental.pallas import tpu as pltpu

D_MODEL = 1024
SEQ = 16384
PLE_DIM = 256
D_FF = 2816
EPS = 1e-6
MLA_HEADS = 8
MLA_Q_RANK = 256
MLA_KV_RANK = 256
MLA_NOPE = 64
MLA_ROPE = 32
MLA_V = 64
ROPE_THETA = 10000.0
ML_HEADS = 4
ML_QK = 64
ML_V = 128
ML_CHUNK = 128
CONV_W = 5
N_DIR = 2
MLA_OUT = MLA_HEADS * MLA_V
ML_OUT = ML_HEADS * ML_V

LANES = 128
HEAD_PAD = LANES
ROPE_LO = MLA_NOPE
ROPE_HALF = MLA_ROPE // 2
VMEM_LIMIT = 56 * 1024 * 1024

F32 = jnp.float32
BF16 = jnp.bfloat16
LOG2E = math.log2(math.e)
Q_SCALE = (MLA_NOPE + MLA_ROPE) ** -0.5 * LOG2E

_O_CQ = 0
_O_CKV = _O_CQ + MLA_Q_RANK
_O_KR = _O_CKV + MLA_KV_RANK
_O_MLQ = _O_KR + MLA_ROPE
_O_MLK = _O_MLQ + ML_HEADS * ML_QK
_O_MLV = _O_MLK + ML_HEADS * ML_QK
_O_MLI = _O_MLV + ML_OUT
_O_MLF = _O_MLI + N_DIR * ML_HEADS
_O_MLO = _O_MLF + N_DIR * ML_HEADS
_O_GATES = _O_MLO + ML_OUT
_O_END = _O_GATES + 2 * D_MODEL

N_GATE = N_DIR * ML_HEADS
QK_PAD = ML_HEADS * HEAD_PAD

_P_CQ = 0
_P_CKV = _P_CQ + MLA_Q_RANK
_P_MLQK = _P_CKV + MLA_KV_RANK
_P_MLV = _P_MLQK + 2 * QK_PAD
_P_MLO = _P_MLV + ML_OUT
_P_GATES = _P_MLO + ML_OUT
_P_SMALL = _P_GATES + 2 * D_MODEL
_P_END = _P_SMALL + LANES


def _rms(x, w):
    ms = jnp.mean(x * x, axis=-1, keepdims=True)
    return x * lax.rsqrt(ms + EPS) * w


def _sigmoid(x):
    return 1.0 / (1.0 + jnp.exp(-x))


def _dot(a, b):
    return jnp.dot(a, b, preferred_element_type=F32)


def _const_spec(shape):
    nd = len(shape)
    return pl.BlockSpec(shape, lambda *_: (0,) * nd, pipeline_mode=pl.Buffered(1))


def _params(sem):
    return pltpu.CompilerParams(dimension_semantics=sem, vmem_limit_bytes=VMEM_LIMIT)


def _ffn_body(x, pre, post, wg, wu, wd):
    u = _rms(x, pre).astype(BF16)
    g = _dot(u, wg)
    up = _dot(u, wu)
    hid = (g * _sigmoid(g) * up).astype(BF16)
    y = _dot(hid, wd)
    return x + 0.5 * _rms(y, post)


def _ffn_kernel(x_ref, pre_ref, post_ref, wg_ref, wu_ref, wd_ref, o_ref):
    o_ref[...] = _ffn_body(x_ref[...], pre_ref[...], post_ref[...],
                           wg_ref[...], wu_ref[...], wd_ref[...])


def _ffn_ple_kernel(x_ref, pre_ref, post_ref, wg_ref, wu_ref, wd_ref,
                    p_ref, ppre_ref, ppost_ref, wproj_ref, wgate_ref, o_ref):
    h = _ffn_body(x_ref[...], pre_ref[...], post_ref[...],
                  wg_ref[...], wu_ref[...], wd_ref[...])
    e = _dot(p_ref[...].astype(BF16), wproj_ref[...])
    g = _sigmoid(_dot(_rms(h, ppre_ref[...]).astype(BF16), wgate_ref[...]))
    o_ref[...] = h + _rms(g * e, ppost_ref[...])


def _ffn(x, pre, post, wg, wu, wd, ple=None, tm=512):
    s, d = x.shape
    row = pl.BlockSpec((tm, d), lambda i: (i, 0))
    in_specs = [row, _const_spec((1, d)), _const_spec((1, d)),
                _const_spec(wg.shape), _const_spec(wu.shape), _const_spec(wd.shape)]
    args = [x, pre, post, wg, wu, wd]
    kern = _ffn_kernel
    if ple is not None:
        p, ppre, ppost, wproj, wgate = ple
        in_specs += [pl.BlockSpec((tm, p.shape[1]), lambda i: (i, 0)),
                     _const_spec((1, d)), _const_spec((1, d)),
                     _const_spec(wproj.shape), _const_spec(wgate.shape)]
        args += [p, ppre, ppost, wproj, wgate]
        kern = _ffn_ple_kernel
    return pl.pallas_call(
        kern, grid=(s // tm,), in_specs=in_specs, out_specs=row,
        out_shape=jax.ShapeDtypeStruct((s, d), F32),
        compiler_params=_params(("parallel",)),
        name="ffn" if ple is None else "ffn_ple",
    )(*args)


def _rope_tables(pos_ref, invf_ref):
    ang = pos_ref[...].astype(F32) * invf_ref[...]
    c = jnp.cos(ang)
    s = jnp.sin(ang)
    lane = lax.broadcasted_iota(jnp.int32, (1, LANES), 1)
    lo = (lane >= ROPE_LO) & (lane < ROPE_LO + ROPE_HALF)
    hi = (lane >= ROPE_LO + ROPE_HALF) & (lane < ROPE_LO + MLA_ROPE)
    s_from_lo = jnp.where(hi, s, 0.0)
    s_from_hi = jnp.where(lo, -s, 0.0)
    return c, s_from_lo, s_from_hi, lo | hi


def _rope(x, c, s_from_lo, s_from_hi):
    return (x * c + pltpu.roll(x, ROPE_HALF, 1) * s_from_lo
            + pltpu.roll(x, LANES - ROPE_HALF, 1) * s_from_hi)


def _proj_kernel(h_ref, pos_ref, invf_ref, pre_ref, win_ref, qn_ref, kvn_ref,
                 wuq_ref, wuk_ref, wuv_ref, vone_ref,
                 q_ref, k_ref, v_ref, mlqk_ref, mlv_ref, small_ref, mlo_ref, gates_ref):
    u = _rms(h_ref[...], pre_ref[...]).astype(BF16)
    z = _dot(u, win_ref[...])
    mlqk_ref[...] = z[:, _P_MLQK:_P_MLV]
    mlv_ref[...] = z[:, _P_MLV:_P_MLO]
    mlo_ref[...] = z[:, _P_MLO:_P_GATES]
    gates_ref[...] = z[:, _P_GATES:_P_SMALL]
    small = z[:, _P_SMALL:_P_END]
    small_ref[...] = small

    c, s_lo, s_hi, rope_mask = _rope_tables(pos_ref, invf_ref)
    k_rope = jnp.where(rope_mask, _rope(small, c, s_lo, s_hi), 0.0)

    cq = _rms(z[:, _P_CQ:_P_CKV], qn_ref[...]).astype(BF16)
    ckv = _rms(z[:, _P_CKV:_P_MLQK], kvn_ref[...]).astype(BF16)
    qp = _dot(cq, wuq_ref[...])
    kp = _dot(ckv, wuk_ref[...])
    vp = _dot(ckv, wuv_ref[...]) + vone_ref[...]
    for hd in range(MLA_HEADS):
        blk = slice(hd * HEAD_PAD, (hd + 1) * HEAD_PAD)
        q_ref[hd] = (_rope(qp[:, blk], c, s_lo, s_hi) * Q_SCALE).astype(BF16)
        k_ref[hd] = (kp[:, blk] + k_rope).astype(BF16)
        v_ref[hd] = vp[:, blk].astype(BF16)


def _proj(h, pos, invf, pre, win, qn, kvn, wuq, wuk, wuv, vone, tm=256):
    s, d = h.shape
    row = lambda w: pl.BlockSpec((tm, w), lambda i: (i, 0))
    head = pl.BlockSpec((MLA_HEADS, tm, HEAD_PAD), lambda i: (0, i, 0))
    hshape = jax.ShapeDtypeStruct((MLA_HEADS, s, HEAD_PAD), BF16)
    return pl.pallas_call(
        _proj_kernel, grid=(s // tm,),
        in_specs=[row(d), row(1), _const_spec((1, LANES)), _const_spec((1, d)),
                  _const_spec(win.shape), _const_spec(qn.shape), _const_spec(kvn.shape),
                  _const_spec(wuq.shape), _const_spec(wuk.shape), _const_spec(wuv.shape),
                  _const_spec(vone.shape)],
        out_specs=[head, head, head, row(2 * QK_PAD), row(ML_OUT), row(LANES),
                   row(ML_OUT), row(2 * D_MODEL)],
        out_shape=[hshape, hshape, hshape,
                   jax.ShapeDtypeStruct((s, 2 * QK_PAD), F32),
                   jax.ShapeDtypeStruct((s, ML_OUT), F32),
                   jax.ShapeDtypeStruct((s, LANES), F32),
                   jax.ShapeDtypeStruct((s, ML_OUT), F32),
                   jax.ShapeDtypeStruct((s, 2 * D_MODEL), F32)],
        compiler_params=_params(("parallel",)),
        name="proj",
    )(h, pos, invf, pre, win, qn, kvn, wuq, wuk, wuv, vone)


def _attn_kernel(q_ref, k_ref, v_ref, o_ref, m_sc, acc_sc, *, tk):
    q = q_ref[0]
    m_sc[...] = jnp.full(m_sc.shape, -jnp.inf, F32)
    acc_sc[...] = jnp.zeros(acc_sc.shape, F32)

    def body(t, carry):
        off = pl.multiple_of(t * tk, tk)
        k = k_ref[0, pl.ds(off, tk), :]
        v = v_ref[0, pl.ds(off, tk), :]
        s = lax.dot_general(q, k, (((1,), (1,)), ((), ())), preferred_element_type=F32)
        m_old = m_sc[...]
        m_new = jnp.maximum(m_old, jnp.max(s, axis=1, keepdims=True))
        p = jnp.exp2(s - m_new)
        alpha = jnp.exp2(m_old - m_new)
        acc_sc[...] = alpha * acc_sc[...] + _dot(p.astype(BF16), v)
        m_sc[...] = m_new
        return carry

    lax.fori_loop(0, k_ref.shape[1] // tk, body, 0)
    acc = acc_sc[...]
    o_ref[...] = (acc / acc[:, MLA_V:MLA_V + 1]).astype(o_ref.dtype)


def _attn(q, k, v, tq=512, tk=512):
    nh, s, w = q.shape
    kv = pl.BlockSpec((1, s, w), lambda h, i: (h, 0, 0))
    return pl.pallas_call(
        functools.partial(_attn_kernel, tk=tk), grid=(nh, s // tq),
        in_specs=[pl.BlockSpec((1, tq, w), lambda h, i: (h, i, 0)), kv, kv],
        out_specs=pl.BlockSpec((tq, w), lambda h, i: (i, h)),
        out_shape=jax.ShapeDtypeStruct((s, nh * w), BF16),
        scratch_shapes=[pltpu.VMEM((tq, 1), F32), pltpu.VMEM((tq, w), F32)],
        compiler_params=_params(("parallel", "parallel")),
        name="attn",
    )(q, k, v)


CONV_HALO = 8


def _conv_kernel(x_ref, prev_ref, next_ref, w_ref, b_ref, scale_ref, o_ref, buf, *, tm):
    i = pl.program_id(0)
    n = pl.num_programs(0)
    buf[CONV_HALO:CONV_HALO + tm, :] = x_ref[...]
    buf[0:CONV_HALO, :] = jnp.where(i > 0, prev_ref[...], 0.0)
    buf[CONV_HALO + tm:, :] = jnp.where(i < n - 1, next_ref[...], 0.0)
    acc = jnp.broadcast_to(b_ref[...], (tm, x_ref.shape[1]))
    for j in range(CONV_W):
        start = CONV_HALO - CONV_W // 2 + j
        acc = acc + w_ref[j:j + 1, :] * buf[start:start + tm, :]
    o_ref[...] = acc * _sigmoid(acc) * scale_ref[...]


def _conv(x, w, b, scale, tm=512):
    s, c = x.shape
    r = tm // CONV_HALO
    nblk = s // CONV_HALO
    return pl.pallas_call(
        functools.partial(_conv_kernel, tm=tm), grid=(s // tm,),
        in_specs=[pl.BlockSpec((tm, c), lambda i: (i, 0)),
                  pl.BlockSpec((CONV_HALO, c), lambda i: (jnp.maximum(i * r - 1, 0), 0)),
                  pl.BlockSpec((CONV_HALO, c), lambda i: (jnp.minimum((i + 1) * r, nblk - 1), 0)),
                  _const_spec(w.shape), _const_spec(b.shape), _const_spec(scale.shape)],
        out_specs=pl.BlockSpec((tm, c), lambda i: (i, 0)),
        out_shape=jax.ShapeDtypeStruct((s, c), F32),
        scratch_shapes=[pltpu.VMEM((tm + 2 * CONV_HALO, c), F32)],
        compiler_params=_params(("parallel",)),
        name="conv",
    )(x, x, x, w, b, scale)


def _mlstm_kernel(qkf_ref, vf_ref, gf_ref, qkb_ref, vb_ref, gb_ref, bias_ref,
                  hf_ref, hb_ref, c_sc, m_sc):
    L = ML_CHUNK

    @pl.when(pl.program_id(0) == 0)
    def _():
        c_sc[...] = jnp.zeros(c_sc.shape, F32)
        m_sc[...] = jnp.zeros(m_sc.shape, F32)

    row = lax.broadcasted_iota(jnp.int32, (L, L), 0)
    col = lax.broadcasted_iota(jnp.int32, (L, L), 1)
    lane = lax.broadcasted_iota(jnp.int32, (1, LANES), 1)
    is_f = (lane >= N_GATE) & (lane < 2 * N_GATE)
    ones_blk = jnp.where(col == 0, 1.0, 0.0).astype(BF16)

    for d in range(N_DIR):
        qk_ref, v_ref, g_ref, h_ref = ((qkf_ref, vf_ref, gf_ref, hf_ref) if d == 0
                                       else (qkb_ref, vb_ref, gb_ref, hb_ref))
        mask = (row >= col) if d == 0 else (row <= col)
        pre = g_ref[...] + bias_ref[...]
        gate = jnp.where(is_f, jax.nn.log_sigmoid(pre), pre)
        bcol = jnp.dot(mask.astype(F32), gate, preferred_element_type=F32,
                       precision=lax.Precision.HIGHEST)
        brow = bcol.T
        grow = gate.T
        for hd in range(ML_HEADS):
            j = d * ML_HEADS + hd
            blk = slice(hd * HEAD_PAD, (hd + 1) * HEAD_PAD)
            q = qk_ref[:, blk].astype(BF16)
            k = qk_ref[:, QK_PAD + hd * HEAD_PAD:QK_PAD + (hd + 1) * HEAD_PAD]
            v_aug = jnp.concatenate([v_ref[:, blk].astype(BF16), ones_blk], axis=1)
            bc = bcol[:, N_GATE + j:N_GATE + j + 1]
            br = brow[N_GATE + j:N_GATE + j + 1, :]
            ir = grow[j:j + 1, :]
            ic = gate[:, j:j + 1]
            m_prev = m_sc[j][0:1, 0:1]
            c_aug = c_sc[j]

            dmat = jnp.where(mask, bc - br + ir, -jnp.inf)
            m_t = jnp.maximum(bc + m_prev, jnp.max(dmat, axis=1, keepdims=True))
            inter = jnp.exp(bc + m_prev - m_t)
            qk = lax.dot_general(q, k.astype(BF16), (((1,), (1,)), ((), ())),
                                 preferred_element_type=F32)
            smat = qk * jnp.exp(dmat - m_t)
            nv = _dot(smat.astype(BF16), v_aug) + inter * _dot(q, c_aug.astype(BF16))
            num = nv[:, :ML_V]
            den = nv[:, ML_V:ML_V + 1]
            h_ref[:, blk] = num / jnp.maximum(jnp.abs(den), jnp.exp(-m_t))

            b_end = br[:, L - 1:L] if d == 0 else br[:, 0:1]
            gcol = b_end - bc + ic
            m_new = jnp.maximum(b_end + m_prev, jnp.max(gcol, axis=0, keepdims=True))
            decay = jnp.exp(b_end + m_prev - m_new)
            kw = (k * jnp.exp(gcol - m_new)).astype(BF16)
            upd = lax.dot_general(kw, v_aug, (((0,), (0,)), ((), ())),
                                  preferred_element_type=F32)
            c_sc[j] = decay * c_aug + upd
            m_sc[j] = jnp.broadcast_to(m_new, m_sc.shape[1:])


def _mlstm(qk, v, small, bias):
    s = qk.shape[0]
    L = ML_CHUNK
    nc = s // L
    fwd = lambda w: pl.BlockSpec((L, w), lambda c: (c, 0))
    bwd = lambda w: pl.BlockSpec((L, w), lambda c: (nc - 1 - c, 0))
    hshape = jax.ShapeDtypeStruct((s, ML_OUT), F32)
    return pl.pallas_call(
        _mlstm_kernel, grid=(nc,),
        in_specs=[fwd(2 * QK_PAD), fwd(ML_OUT), fwd(LANES),
                  bwd(2 * QK_PAD), bwd(ML_OUT), bwd(LANES), _const_spec((1, LANES))],
        out_specs=[fwd(ML_OUT), bwd(ML_OUT)],
        out_shape=[hshape, hshape],
        scratch_shapes=[pltpu.VMEM((N_GATE, HEAD_PAD, ML_V + LANES), F32),
                        pltpu.VMEM((N_GATE, 8, LANES), F32)],
        compiler_params=_params(("arbitrary",)),
        name="mlstm",
    )(qk, v, small, qk, v, small, bias)


def _merge_kernel(h_ref, att_ref, hf_ref, hb_ref, mlo_ref, gates_ref, hn_ref,
                  wba_ref, wbm_ref, wout_ref, post_ref, o_ref):
    a = _dot(att_ref[...], wba_ref[...])
    hm = hf_ref[...] + hb_ref[...]
    heads = []
    for hd in range(ML_HEADS):
        blk = slice(hd * ML_V, (hd + 1) * ML_V)
        heads.append(_rms(hm[:, blk], hn_ref[:, blk]))
    hm = jnp.concatenate(heads, axis=1) * _sigmoid(mlo_ref[...])
    bm = _dot(hm.astype(BF16), wbm_ref[...])
    g = _sigmoid(gates_ref[...])
    mixed = _dot((g[:, :D_MODEL] * a + g[:, D_MODEL:] * bm).astype(BF16), wout_ref[...])
    o_ref[...] = h_ref[...] + _rms(mixed, post_ref[...])


def _merge(h, att, hf, hb, mlo, gates, hn, wba, wbm, wout, post, tm=512):
    s, d = h.shape
    row = lambda w: pl.BlockSpec((tm, w), lambda i: (i, 0))
    return pl.pallas_call(
        _merge_kernel, grid=(s // tm,),
        in_specs=[row(d), row(att.shape[1]), row(ML_OUT), row(ML_OUT), row(ML_OUT),
                  row(2 * d), _const_spec(hn.shape), _const_spec(wba.shape),
                  _const_spec(wbm.shape), _const_spec(wout.shape), _const_spec(post.shape)],
        out_specs=row(d),
        out_shape=jax.ShapeDtypeStruct((s, d), F32),
        compiler_params=_params(("parallel",)),
        name="merge",
    )(h, att, hf, hb, mlo, gates, hn, wba, wbm, wout, post)


def _pad_heads(w, nh, width):
    r = w.shape[0]
    w = w.reshape(r, nh, width)
    return jnp.pad(w, ((0, 0), (0, 0), (0, HEAD_PAD - width))).reshape(r, nh * HEAD_PAD)


def _permute_w_in(w_in):
    d = w_in.shape[0]
    small = jnp.zeros((d, LANES), w_in.dtype)
    small = small.at[:, 0:N_GATE].set(w_in[:, _O_MLI:_O_MLF])
    small = small.at[:, N_GATE:2 * N_GATE].set(w_in[:, _O_MLF:_O_MLO])
    small = small.at[:, ROPE_LO:ROPE_LO + MLA_ROPE].set(w_in[:, _O_KR:_O_MLQ])
    return jnp.concatenate([
        w_in[:, _O_CQ:_O_KR],
        _pad_heads(w_in[:, _O_MLQ:_O_MLK], ML_HEADS, ML_QK),
        _pad_heads(w_in[:, _O_MLK:_O_MLV], ML_HEADS, ML_QK),
        w_in[:, _O_MLV:_O_MLI],
        w_in[:, _O_MLO:_O_END],
        small], axis=1)


def kernel(x, p, positions, ffn1_pre_norm, ffn1_post_norm, ffn1_w_gate, ffn1_w_up, ffn1_w_down, mix_pre_norm, mix_post_norm, w_in, mla_q_norm, mla_kv_norm, mla_w_uq, mla_w_uk, mla_w_uv, ml_conv_w, ml_conv_b, ml_i_bias, ml_f_bias, ml_head_norm, w_branch_mla, w_branch_ml, w_out, ffn2_pre_norm, ffn2_post_norm, ffn2_w_gate, ffn2_w_up, ffn2_w_down, ple_pre_norm, ple_post_norm, ple_w_proj, ple_w_gate):
    depth, batch = p.shape[0], x.shape[0]
    assert depth == 1 and batch == 1, "kernel is specialised to DEPTH == 1, BATCH == 1"
    bf = lambda w: w.astype(BF16)
    h = x[0]
    pos = positions[0][:, None]

    inv_freq = ROPE_THETA ** (-jnp.arange(0, MLA_ROPE, 2, dtype=F32) / MLA_ROPE)
    invf = jnp.zeros((1, LANES), F32)
    invf = invf.at[0, ROPE_LO:ROPE_LO + ROPE_HALF].set(inv_freq)
    invf = invf.at[0, ROPE_LO + ROPE_HALF:ROPE_LO + MLA_ROPE].set(inv_freq)

    i = 0
    h = _ffn(h, ffn1_pre_norm[i][None], ffn1_post_norm[i][None],
             bf(ffn1_w_gate[i]), bf(ffn1_w_up[i]), bf(ffn1_w_down[i]))

    vone = jnp.zeros((MLA_HEADS, HEAD_PAD), F32).at[:, MLA_V].set(1.0).reshape(1, -1)
    q, k, v, mlqk, mlv, small, mlo, gates = _proj(
        h, pos, invf, mix_pre_norm[i][None], bf(_permute_w_in(w_in[i])),
        mla_q_norm[i][None], mla_kv_norm[i][None],
        bf(_pad_heads(mla_w_uq[i], MLA_HEADS, MLA_NOPE + MLA_ROPE)),
        bf(_pad_heads(mla_w_uk[i], MLA_HEADS, MLA_NOPE)),
        bf(_pad_heads(mla_w_uv[i], MLA_HEADS, MLA_V)), vone)

    att = _attn(q, k, v)

    conv_w = jnp.concatenate([_pad_heads(ml_conv_w[i][:, :ML_HEADS * ML_QK], ML_HEADS, ML_QK),
                              _pad_heads(ml_conv_w[i][:, ML_HEADS * ML_QK:], ML_HEADS, ML_QK)], axis=1)
    conv_w = jnp.pad(conv_w, ((0, 8 - CONV_W), (0, 0)))
    conv_b = jnp.concatenate([_pad_heads(ml_conv_b[i][None, :ML_HEADS * ML_QK], ML_HEADS, ML_QK),
                              _pad_heads(ml_conv_b[i][None, ML_HEADS * ML_QK:], ML_HEADS, ML_QK)], axis=1)
    k_scale = jnp.concatenate([jnp.ones((1, QK_PAD), F32),
                               jnp.full((1, QK_PAD), ML_QK ** -0.5, F32)], axis=1)
    qk_conv = _conv(mlqk, conv_w, conv_b, k_scale)

    gate_bias = jnp.zeros((1, LANES), F32)
    gate_bias = gate_bias.at[0, 0:N_GATE].set(ml_i_bias[i].reshape(-1))
    gate_bias = gate_bias.at[0, N_GATE:2 * N_GATE].set(ml_f_bias[i].reshape(-1))
    hf, hb = _mlstm(qk_conv, mlv, small, gate_bias)

    wba = jnp.pad(w_branch_mla[i].reshape(MLA_HEADS, MLA_V, D_MODEL),
                  ((0, 0), (0, HEAD_PAD - MLA_V), (0, 0))).reshape(MLA_HEADS * HEAD_PAD, D_MODEL)
    h = _merge(h, att, hf, hb, mlo, gates, ml_head_norm[i][None], bf(wba),
               bf(w_branch_ml[i]), bf(w_out[i]), mix_post_norm[i][None])

    h = _ffn(h, ffn2_pre_norm[i][None], ffn2_post_norm[i][None],
             bf(ffn2_w_gate[i]), bf(ffn2_w_up[i]), bf(ffn2_w_down[i]),
             ple=(p[i, 0], ple_pre_norm[i][None], ple_post_norm[i][None],
                  bf(ple_w_proj[i]), bf(ple_w_gate[i])))
    return h[None]
```

```python
import functools
import math

import jax
import jax.numpy as jnp
from jax import lax
from jax.experimental import pallas as pl
from jax.experimental.pallas import tpu as pltpu

D_MODEL = 1024
SEQ = 16384
PLE_DIM = 256
D_FF = 2816
EPS = 1e-6
MLA_HEADS = 8
MLA_Q_RANK = 256
MLA_KV_RANK = 256
MLA_NOPE = 64
MLA_ROPE = 32
MLA_V = 64
ROPE_THETA = 10000.0
ML_HEADS = 4
ML_QK = 64
ML_V = 128
ML_CHUNK = 128
CONV_W = 5
N_DIR = 2
MLA_OUT = MLA_HEADS * MLA_V
ML_OUT = ML_HEADS * ML_V

LANES = 128
HEAD_PAD = LANES
ROPE_LO = MLA_NOPE
ROPE_HALF = MLA_ROPE // 2
VMEM_LIMIT = 56 * 1024 * 1024

F32 = jnp.float32
BF16 = jnp.bfloat16
LOG2E = math.log2(math.e)
Q_SCALE = (MLA_NOPE + MLA_ROPE) ** -0.5 * LOG2E

_O_CQ = 0
_O_CKV = _O_CQ + MLA_Q_RANK
_O_KR = _O_CKV + MLA_KV_RANK
_O_MLQ = _O_KR + MLA_ROPE
_O_MLK = _O_MLQ + ML_HEADS * ML_QK
_O_MLV = _O_MLK + ML_HEADS * ML_QK
_O_MLI = _O_MLV + ML_OUT
_O_MLF = _O_MLI + N_DIR * ML_HEADS
_O_MLO = _O_MLF + N_DIR * ML_HEADS
_O_GATES = _O_MLO + ML_OUT
_O_END = _O_GATES + 2 * D_MODEL

N_GATE = N_DIR * ML_HEADS
QK_PAD = ML_HEADS * HEAD_PAD

_P_CQ = 0
_P_CKV = _P_CQ + MLA_Q_RANK
_P_MLQK = _P_CKV + MLA_KV_RANK
_P_MLV = _P_MLQK + 2 * QK_PAD
_P_MLO = _P_MLV + ML_OUT
_P_GATES = _P_MLO + ML_OUT
_P_SMALL = _P_GATES + 2 * D_MODEL
_P_END = _P_SMALL + LANES


def _rms(x, w):
    ms = jnp.mean(x * x, axis=-1, keepdims=True)
    return x * lax.rsqrt(ms + EPS) * w


def _sigmoid(x):
    return 1.0 / (1.0 + jnp.exp(-x))


def _dot(a, b):
    return jnp.dot(a, b, preferred_element_type=F32)


def _const_spec(shape):
    nd = len(shape)
    return pl.BlockSpec(shape, lambda *_: (0,) * nd, pipeline_mode=pl.Buffered(1))


def _params(sem):
    return pltpu.CompilerParams(dimension_semantics=sem, vmem_limit_bytes=VMEM_LIMIT)


def _ffn_body(x, pre, post, wg, wu, wd):
    u = _rms(x, pre).astype(BF16)
    g = _dot(u, wg)
    up = _dot(u, wu)
    hid = (g * _sigmoid(g) * up).astype(BF16)
    y = _dot(hid, wd)
    return x + 0.5 * _rms(y, post)


def _ffn_kernel(x_ref, pre_ref, post_ref, wg_ref, wu_ref, wd_ref, o_ref):
    o_ref[...] = _ffn_body(x_ref[...], pre_ref[...], post_ref[...],
                           wg_ref[...], wu_ref[...], wd_ref[...])


def _ffn_ple_kernel(x_ref, pre_ref, post_ref, wg_ref, wu_ref, wd_ref,
                    p_ref, ppre_ref, ppost_ref, wproj_ref, wgate_ref, o_ref):
    h = _ffn_body(x_ref[...], pre_ref[...], post_ref[...],
                  wg_ref[...], wu_ref[...], wd_ref[...])
    e = _dot(p_ref[...].astype(BF16), wproj_ref[...])
    g = _sigmoid(_dot(_rms(h, ppre_ref[...]).astype(BF16), wgate_ref[...]))
    o_ref[...] = h + _rms(g * e, ppost_ref[...])


def _ffn(x, pre, post, wg, wu, wd, ple=None, tm=512):
    s, d = x.shape
    row = pl.BlockSpec((tm, d), lambda i: (i, 0))
    in_specs = [row, _const_spec((1, d)), _const_spec((1, d)),
                _const_spec(wg.shape), _const_spec(wu.shape), _const_spec(wd.shape)]
    args = [x, pre, post, wg, wu, wd]
    kern = _ffn_kernel
    if ple is not None:
        p, ppre, ppost, wproj, wgate = ple
        in_specs += [pl.BlockSpec((tm, p.shape[1]), lambda i: (i, 0)),
                     _const_spec((1, d)), _const_spec((1, d)),
                     _const_spec(wproj.shape), _const_spec(wgate.shape)]
        args += [p, ppre, ppost, wproj, wgate]
        kern = _ffn_ple_kernel
    return pl.pallas_call(
        kern, grid=(s // tm,), in_specs=in_specs, out_specs=row,
        out_shape=jax.ShapeDtypeStruct((s, d), F32),
        compiler_params=_params(("parallel",)),
        name="ffn" if ple is None else "ffn_ple",
    )(*args)


def _dot_nt(a, b):
    return lax.dot_general(a, b, (((1,), (1,)), ((), ())), preferred_element_type=F32)


def _rope_tables_t(pos_ref, invf_ref):
    ang = pos_ref[...].astype(F32) * invf_ref[...]
    c = jnp.cos(ang)
    s = jnp.sin(ang)
    r = lax.broadcasted_iota(jnp.int32, (HEAD_PAD, 1), 0)
    lo = (r >= ROPE_LO) & (r < ROPE_LO + ROPE_HALF)
    hi = (r >= ROPE_LO + ROPE_HALF) & (r < ROPE_LO + MLA_ROPE)
    s_from_lo = jnp.where(hi, s, 0.0)
    s_from_hi = jnp.where(lo, -s, 0.0)
    return c, s_from_lo, s_from_hi, lo | hi


def _rope_t(x, c, s_from_lo, s_from_hi):
    return (x * c + pltpu.roll(x, ROPE_HALF, 0) * s_from_lo
            + pltpu.roll(x, HEAD_PAD - ROPE_HALF, 0) * s_from_hi)


def _proj_kernel(h_ref, pos_ref, invf_ref, pre_ref, win_ref, wsmt_ref, qn_ref, kvn_ref,
                 wuqt_ref, wuk_ref, wuvt_ref, vone_ref,
                 qt_ref, k_ref, vt_ref, mlqk_ref, mlv_ref, small_ref, mlo_ref, gates_ref):
    u = _rms(h_ref[...], pre_ref[...]).astype(BF16)
    z = _dot(u, win_ref[...])
    mlqk_ref[...] = z[:, _P_MLQK:_P_MLV]
    mlv_ref[...] = z[:, _P_MLV:_P_MLO]
    mlo_ref[...] = z[:, _P_MLO:_P_GATES]
    gates_ref[...] = z[:, _P_GATES:_P_SMALL]
    small_ref[...] = z[:, _P_SMALL:_P_END]

    c, s_lo, s_hi, rope_rows = _rope_tables_t(pos_ref, invf_ref)
    small_t = _dot_nt(wsmt_ref[...], u)
    k_rope = jnp.where(rope_rows, _rope_t(small_t, c, s_lo, s_hi), 0.0).T

    cq = _rms(z[:, _P_CQ:_P_CKV], qn_ref[...]).astype(BF16)
    ckv = _rms(z[:, _P_CKV:_P_MLQK], kvn_ref[...]).astype(BF16)
    qpt = _dot_nt(wuqt_ref[...], cq)
    kp = _dot(ckv, wuk_ref[...])
    vt_ref[...] = (_dot_nt(wuvt_ref[...], ckv) + vone_ref[...]).astype(BF16)
    for hd in range(MLA_HEADS):
        blk = slice(hd * HEAD_PAD, (hd + 1) * HEAD_PAD)
        qt_ref[blk, :] = (_rope_t(qpt[blk, :], c, s_lo, s_hi) * Q_SCALE).astype(BF16)
        k_ref[hd] = (kp[:, blk] + k_rope).astype(BF16)


def _proj(h, pos, invf, pre, win, wsmt, qn, kvn, wuqt, wuk, wuvt, vone, tm=256):
    s, d = h.shape
    row = lambda w: pl.BlockSpec((tm, w), lambda i: (i, 0))
    col = pl.BlockSpec((MLA_HEADS * HEAD_PAD, tm), lambda i: (0, i))
    cshape = jax.ShapeDtypeStruct((MLA_HEADS * HEAD_PAD, s), BF16)
    return pl.pallas_call(
        _proj_kernel, grid=(s // tm,),
        in_specs=[row(d), pl.BlockSpec((1, tm), lambda i: (0, i)),
                  _const_spec(invf.shape), _const_spec((1, d)),
                  _const_spec(win.shape), _const_spec(wsmt.shape),
                  _const_spec(qn.shape), _const_spec(kvn.shape),
                  _const_spec(wuqt.shape), _const_spec(wuk.shape), _const_spec(wuvt.shape),
                  _const_spec(vone.shape)],
        out_specs=[col, pl.BlockSpec((MLA_HEADS, tm, HEAD_PAD), lambda i: (0, i, 0)), col,
                   row(2 * QK_PAD), row(ML_OUT), row(LANES), row(ML_OUT), row(2 * D_MODEL)],
        out_shape=[cshape, jax.ShapeDtypeStruct((MLA_HEADS, s, HEAD_PAD), BF16), cshape,
                   jax.ShapeDtypeStruct((s, 2 * QK_PAD), F32),
                   jax.ShapeDtypeStruct((s, ML_OUT), F32),
                   jax.ShapeDtypeStruct((s, LANES), F32),
                   jax.ShapeDtypeStruct((s, ML_OUT), F32),
                   jax.ShapeDtypeStruct((s, 2 * D_MODEL), F32)],
        compiler_params=_params(("parallel",)),
        name="proj",
    )(h, pos, invf, pre, win, wsmt, qn, kvn, wuqt, wuk, wuvt, vone)


BF16_ROWS = 16
V_ROWS = -(-(MLA_V + 1) // BF16_ROWS) * BF16_ROWS


def _attn_kernel(qt_ref, k_ref, vt_ref, o_ref, acc_sc, st0, st1, p0, p1, *, tk, steps):
    qt = qt_ref[...]
    tq = qt.shape[1]
    n = k_ref.shape[1] // tk
    st, pb = (st0, st1), (p0, p1)

    acc_sc[...] = jnp.zeros(acc_sc.shape, F32)
    p1[...] = jnp.zeros(p1.shape, BF16)
    st0[...] = _dot(k_ref[0, 0:tk, :], qt)

    def accumulate(t_prev, slot, alpha):
        off = pl.multiple_of(t_prev * tk, tk)
        vt = vt_ref[0:V_ROWS, pl.ds(off, tk)]
        acc_sc[0:V_ROWS, :] = alpha * acc_sc[0:V_ROWS, :] + _dot(vt, pb[slot][...])

    def step(t, slot, m, alpha_prev):
        off = pl.multiple_of(jnp.minimum(t + 1, n - 1) * tk, tk)
        st[1 - slot][...] = _dot(k_ref[0, pl.ds(off, tk), :], qt)
        accumulate(jnp.maximum(t - 1, 0), 1 - slot, alpha_prev)
        x = st[slot][...]
        m_new = jnp.maximum(m, jnp.max(x, axis=0, keepdims=True))
        pb[slot][...] = jnp.exp2(x - m_new).astype(BF16)
        return m_new, jnp.exp2(m - m_new)

    def body(i, carry):
        m, alpha = carry
        for j in range(steps):
            m, alpha = step(i * steps + j, j % 2, m, alpha)
        return m, alpha

    init = (jnp.full((1, tq), -jnp.inf, F32), jnp.ones((1, tq), F32))
    _, alpha = lax.fori_loop(0, n // steps, body, init)
    accumulate(n - 1, (n - 1) % 2, alpha)
    acc = acc_sc[...]
    o_ref[...] = (acc / acc[MLA_V:MLA_V + 1, :]).T.astype(o_ref.dtype)


def _attn(qt, k, vt, tq=512, tk=256, steps=4):
    nh, s, w = k.shape
    assert steps % 2 == 0 and (s // tk) % steps == 0
    return pl.pallas_call(
        functools.partial(_attn_kernel, tk=tk, steps=steps), grid=(nh, s // tq),
        in_specs=[pl.BlockSpec((w, tq), lambda h, i: (h, i)),
                  pl.BlockSpec((1, s, w), lambda h, i: (h, 0, 0)),
                  pl.BlockSpec((w, s), lambda h, i: (h, 0))],
        out_specs=pl.BlockSpec((tq, w), lambda h, i: (i, h)),
        out_shape=jax.ShapeDtypeStruct((s, nh * w), BF16),
        scratch_shapes=[pltpu.VMEM((w, tq), F32),
                        pltpu.VMEM((tk, tq), F32), pltpu.VMEM((tk, tq), F32),
                        pltpu.VMEM((tk, tq), BF16), pltpu.VMEM((tk, tq), BF16)],
        compiler_params=_params(("parallel", "parallel")),
        name="attn",
    )(qt, k, vt)


CONV_HALO = 8


def _conv_kernel(x_ref, prev_ref, next_ref, w_ref, b_ref, scale_ref, o_ref, buf, *, tm):
    i = pl.program_id(0)
    n = pl.num_programs(0)
    buf[CONV_HALO:CONV_HALO + tm, :] = x_ref[...]
    buf[0:CONV_HALO, :] = jnp.where(i > 0, prev_ref[...], 0.0)
    buf[CONV_HALO + tm:, :] = jnp.where(i < n - 1, next_ref[...], 0.0)
    acc = jnp.broadcast_to(b_ref[...], (tm, x_ref.shape[1]))
    for j in range(CONV_W):
        start = CONV_HALO - CONV_W // 2 + j
        acc = acc + w_ref[j:j + 1, :] * buf[start:start + tm, :]
    o_ref[...] = acc * _sigmoid(acc) * scale_ref[...]


def _conv(x, w, b, scale, tm=512):
    s, c = x.shape
    r = tm // CONV_HALO
    nblk = s // CONV_HALO
    return pl.pallas_call(
        functools.partial(_conv_kernel, tm=tm), grid=(s // tm,),
        in_specs=[pl.BlockSpec((tm, c), lambda i: (i, 0)),
                  pl.BlockSpec((CONV_HALO, c), lambda i: (jnp.maximum(i * r - 1, 0), 0)),
                  pl.BlockSpec((CONV_HALO, c), lambda i: (jnp.minimum((i + 1) * r, nblk - 1), 0)),
                  _const_spec(w.shape), _const_spec(b.shape), _const_spec(scale.shape)],
        out_specs=pl.BlockSpec((tm, c), lambda i: (i, 0)),
        out_shape=jax.ShapeDtypeStruct((s, c), F32),
        scratch_shapes=[pltpu.VMEM((tm + 2 * CONV_HALO, c), F32)],
        compiler_params=_params(("parallel",)),
        name="conv",
    )(x, x, x, w, b, scale)


def _mlstm_kernel(qkf_ref, vf_ref, gf_ref, qkb_ref, vb_ref, gb_ref, bias_ref,
                  hf_ref, hb_ref, c_sc, m_sc):
    L = ML_CHUNK

    @pl.when(pl.program_id(0) == 0)
    def _():
        c_sc[...] = jnp.zeros(c_sc.shape, F32)
        m_sc[...] = jnp.zeros(m_sc.shape, F32)

    row = lax.broadcasted_iota(jnp.int32, (L, L), 0)
    col = lax.broadcasted_iota(jnp.int32, (L, L), 1)
    lane = lax.broadcasted_iota(jnp.int32, (1, LANES), 1)
    is_f = (lane >= N_GATE) & (lane < 2 * N_GATE)
    ones_blk = jnp.where(col == 0, 1.0, 0.0).astype(BF16)

    for d in range(N_DIR):
        qk_ref, v_ref, g_ref, h_ref = ((qkf_ref, vf_ref, gf_ref, hf_ref) if d == 0
                                       else (qkb_ref, vb_ref, gb_ref, hb_ref))
        mask = (row >= col) if d == 0 else (row <= col)
        pre = g_ref[...] + bias_ref[...]
        gate = jnp.where(is_f, jax.nn.log_sigmoid(pre), pre)
        bcol = jnp.dot(mask.astype(F32), gate, preferred_element_type=F32,
                       precision=lax.Precision.HIGHEST)
        brow = bcol.T
        grow = gate.T
        for hd in range(ML_HEADS):
            j = d * ML_HEADS + hd
            blk = slice(hd * HEAD_PAD, (hd + 1) * HEAD_PAD)
            q = qk_ref[:, blk].astype(BF16)
            k = qk_ref[:, QK_PAD + hd * HEAD_PAD:QK_PAD + (hd + 1) * HEAD_PAD]
            v_aug = jnp.concatenate([v_ref[:, blk].astype(BF16), ones_blk], axis=1)
            bc = bcol[:, N_GATE + j:N_GATE + j + 1]
            br = brow[N_GATE + j:N_GATE + j + 1, :]
            ir = grow[j:j + 1, :]
            ic = gate[:, j:j + 1]
            m_prev = m_sc[j][0:1, 0:1]
            c_aug = c_sc[j]

            dmat = jnp.where(mask, bc - br + ir, -jnp.inf)
            m_t = jnp.maximum(bc + m_prev, jnp.max(dmat, axis=1, keepdims=True))
            inter = jnp.exp(bc + m_prev - m_t)
            qk = lax.dot_general(q, k.astype(BF16), (((1,), (1,)), ((), ())),
                                 preferred_element_type=F32)
            smat = qk * jnp.exp(dmat - m_t)
            nv = _dot(smat.astype(BF16), v_aug) + inter * _dot(q, c_aug.astype(BF16))
            num = nv[:, :ML_V]
            den = nv[:, ML_V:ML_V + 1]
            h_ref[:, blk] = num / jnp.maximum(jnp.abs(den), jnp.exp(-m_t))

            b_end = br[:, L - 1:L] if d == 0 else br[:, 0:1]
            gcol = b_end - bc + ic
            m_new = jnp.maximum(b_end + m_prev, jnp.max(gcol, axis=0, keepdims=True))
            decay = jnp.exp(b_end + m_prev - m_new)
            kw = (k * jnp.exp(gcol - m_new)).astype(BF16)
            upd = lax.dot_general(kw, v_aug, (((0,), (0,)), ((), ())),
                                  preferred_element_type=F32)
            c_sc[j] = decay * c_aug + upd
            m_sc[j] = jnp.broadcast_to(m_new, m_sc.shape[1:])


def _mlstm(qk, v, small, bias):
    s = qk.shape[0]
    L = ML_CHUNK
    nc = s // L
    fwd = lambda w: pl.BlockSpec((L, w), lambda c: (c, 0))
    bwd = lambda w: pl.BlockSpec((L, w), lambda c: (nc - 1 - c, 0))
    hshape = jax.ShapeDtypeStruct((s, ML_OUT), F32)
    return pl.pallas_call(
        _mlstm_kernel, grid=(nc,),
        in_specs=[fwd(2 * QK_PAD), fwd(ML_OUT), fwd(LANES),
                  bwd(2 * QK_PAD), bwd(ML_OUT), bwd(LANES), _const_spec((1, LANES))],
        out_specs=[fwd(ML_OUT), bwd(ML_OUT)],
        out_shape=[hshape, hshape],
        scratch_shapes=[pltpu.VMEM((N_GATE, HEAD_PAD, ML_V + LANES), F32),
                        pltpu.VMEM((N_GATE, 8, LANES), F32)],
        compiler_params=_params(("arbitrary",)),
        name="mlstm",
    )(qk, v, small, qk, v, small, bias)


def _merge_kernel(h_ref, att_ref, hf_ref, hb_ref, mlo_ref, gates_ref, hn_ref,
                  wba_ref, wbm_ref, wout_ref, post_ref, o_ref):
    a = _dot(att_ref[...], wba_ref[...])
    hm = hf_ref[...] + hb_ref[...]
    heads = []
    for hd in range(ML_HEADS):
        blk = slice(hd * ML_V, (hd + 1) * ML_V)
        heads.append(_rms(hm[:, blk], hn_ref[:, blk]))
    hm = jnp.concatenate(heads, axis=1) * _sigmoid(mlo_ref[...])
    bm = _dot(hm.astype(BF16), wbm_ref[...])
    g = _sigmoid(gates_ref[...])
    mixed = _dot((g[:, :D_MODEL] * a + g[:, D_MODEL:] * bm).astype(BF16), wout_ref[...])
    o_ref[...] = h_ref[...] + _rms(mixed, post_ref[...])


def _merge(h, att, hf, hb, mlo, gates, hn, wba, wbm, wout, post, tm=512):
    s, d = h.shape
    row = lambda w: pl.BlockSpec((tm, w), lambda i: (i, 0))
    return pl.pallas_call(
        _merge_kernel, grid=(s // tm,),
        in_specs=[row(d), row(att.shape[1]), row(ML_OUT), row(ML_OUT), row(ML_OUT),
                  row(2 * d), _const_spec(hn.shape), _const_spec(wba.shape),
                  _const_spec(wbm.shape), _const_spec(wout.shape), _const_spec(post.shape)],
        out_specs=row(d),
        out_shape=jax.ShapeDtypeStruct((s, d), F32),
        compiler_params=_params(("parallel",)),
        name="merge",
    )(h, att, hf, hb, mlo, gates, hn, wba, wbm, wout, post)


def _pad_heads(w, nh, width):
    r = w.shape[0]
    w = w.reshape(r, nh, width)
    return jnp.pad(w, ((0, 0), (0, 0), (0, HEAD_PAD - width))).reshape(r, nh * HEAD_PAD)


def _permute_w_in(w_in):
    d = w_in.shape[0]
    small = jnp.zeros((d, LANES), w_in.dtype)
    small = small.at[:, 0:N_GATE].set(w_in[:, _O_MLI:_O_MLF])
    small = small.at[:, N_GATE:2 * N_GATE].set(w_in[:, _O_MLF:_O_MLO])
    small = small.at[:, ROPE_LO:ROPE_LO + MLA_ROPE].set(w_in[:, _O_KR:_O_MLQ])
    return jnp.concatenate([
        w_in[:, _O_CQ:_O_KR],
        _pad_heads(w_in[:, _O_MLQ:_O_MLK], ML_HEADS, ML_QK),
        _pad_heads(w_in[:, _O_MLK:_O_MLV], ML_HEADS, ML_QK),
        w_in[:, _O_MLV:_O_MLI],
        w_in[:, _O_MLO:_O_END],
        small], axis=1)


def kernel(x, p, positions, ffn1_pre_norm, ffn1_post_norm, ffn1_w_gate, ffn1_w_up, ffn1_w_down, mix_pre_norm, mix_post_norm, w_in, mla_q_norm, mla_kv_norm, mla_w_uq, mla_w_uk, mla_w_uv, ml_conv_w, ml_conv_b, ml_i_bias, ml_f_bias, ml_head_norm, w_branch_mla, w_branch_ml, w_out, ffn2_pre_norm, ffn2_post_norm, ffn2_w_gate, ffn2_w_up, ffn2_w_down, ple_pre_norm, ple_post_norm, ple_w_proj, ple_w_gate):
    depth, batch = p.shape[0], x.shape[0]
    assert depth == 1 and batch == 1, "kernel is specialised to DEPTH == 1, BATCH == 1"
    bf = lambda w: w.astype(BF16)
    h = x[0]
    pos = positions

    inv_freq = ROPE_THETA ** (-jnp.arange(0, MLA_ROPE, 2, dtype=F32) / MLA_ROPE)
    invf = jnp.zeros((HEAD_PAD, 1), F32)
    invf = invf.at[ROPE_LO:ROPE_LO + ROPE_HALF, 0].set(inv_freq)
    invf = invf.at[ROPE_LO + ROPE_HALF:ROPE_LO + MLA_ROPE, 0].set(inv_freq)

    i = 0
    h = _ffn(h, ffn1_pre_norm[i][None], ffn1_post_norm[i][None],
             bf(ffn1_w_gate[i]), bf(ffn1_w_up[i]), bf(ffn1_w_down[i]))

    vone = jnp.zeros((MLA_HEADS, HEAD_PAD), F32).at[:, MLA_V].set(1.0).reshape(-1, 1)
    win_p = bf(_permute_w_in(w_in[i]))
    qt, k, vt, mlqk, mlv, small, mlo, gates = _proj(
        h, pos, invf, mix_pre_norm[i][None], win_p, win_p[:, _P_SMALL:_P_END].T,
        mla_q_norm[i][None], mla_kv_norm[i][None],
        bf(_pad_heads(mla_w_uq[i], MLA_HEADS, MLA_NOPE + MLA_ROPE)).T,
        bf(_pad_heads(mla_w_uk[i], MLA_HEADS, MLA_NOPE)),
        bf(_pad_heads(mla_w_uv[i], MLA_HEADS, MLA_V)).T, vone)

    att = _attn(qt, k, vt)

    conv_w = jnp.concatenate([_pad_heads(ml_conv_w[i][:, :ML_HEADS * ML_QK], ML_HEADS, ML_QK),
                              _pad_heads(ml_conv_w[i][:, ML_HEADS * ML_QK:], ML_HEADS, ML_QK)], axis=1)
    conv_w = jnp.pad(conv_w, ((0, 8 - CONV_W), (0, 0)))
    conv_b = jnp.concatenate([_pad_heads(ml_conv_b[i][None, :ML_HEADS * ML_QK], ML_HEADS, ML_QK),
                              _pad_heads(ml_conv_b[i][None, ML_HEADS * ML_QK:], ML_HEADS, ML_QK)], axis=1)
    k_scale = jnp.concatenate([jnp.ones((1, QK_PAD), F32),
                               jnp.full((1, QK_PAD), ML_QK ** -0.5, F32)], axis=1)
    qk_conv = _conv(mlqk, conv_w, conv_b, k_scale)

    gate_bias = jnp.zeros((1, LANES), F32)
    gate_bias = gate_bias.at[0, 0:N_GATE].set(ml_i_bias[i].reshape(-1))
    gate_bias = gate_bias.at[0, N_GATE:2 * N_GATE].set(ml_f_bias[i].reshape(-1))
    hf, hb = _mlstm(qk_conv, mlv, small, gate_bias)

    wba = jnp.pad(w_branch_mla[i].reshape(MLA_HEADS, MLA_V, D_MODEL),
                  ((0, 0), (0, HEAD_PAD - MLA_V), (0, 0))).reshape(MLA_HEADS * HEAD_PAD, D_MODEL)
    h = _merge(h, att, hf, hb, mlo, gates, ml_head_norm[i][None], bf(wba),
               bf(w_branch_ml[i]), bf(w_out[i]), mix_post_norm[i][None])

    h = _ffn(h, ffn2_pre_norm[i][None], ffn2_post_norm[i][None],
             bf(ffn2_w_gate[i]), bf(ffn2_w_up[i]), bf(ffn2_w_down[i]),
             ple=(p[i, 0], ple_pre_norm[i][None], ple_post_norm[i][None],
                  bf(ple_w_proj[i]), bf(ple_w_gate[i])))
    return h[None]
```

```python
import functools
import math

import jax
import jax.numpy as jnp
from jax import lax
from jax.experimental import pallas as pl
from jax.experimental.pallas import tpu as pltpu

D_MODEL = 1024
SEQ = 16384
PLE_DIM = 256
D_FF = 2816
EPS = 1e-6
MLA_HEADS = 8
MLA_Q_RANK = 256
MLA_KV_RANK = 256
MLA_NOPE = 64
MLA_ROPE = 32
MLA_V = 64
ROPE_THETA = 10000.0
ML_HEADS = 4
ML_QK = 64
ML_V = 128
ML_CHUNK = 128
CONV_W = 5
N_DIR = 2
MLA_OUT = MLA_HEADS * MLA_V
ML_OUT = ML_HEADS * ML_V

LANES = 128
HEAD_PAD = LANES
ROPE_LO = MLA_NOPE
ROPE_HALF = MLA_ROPE // 2
VMEM_LIMIT = 56 * 1024 * 1024

F32 = jnp.float32
BF16 = jnp.bfloat16
LOG2E = math.log2(math.e)
Q_SCALE = (MLA_NOPE + MLA_ROPE) ** -0.5 * LOG2E

_O_CQ = 0
_O_CKV = _O_CQ + MLA_Q_RANK
_O_KR = _O_CKV + MLA_KV_RANK
_O_MLQ = _O_KR + MLA_ROPE
_O_MLK = _O_MLQ + ML_HEADS * ML_QK
_O_MLV = _O_MLK + ML_HEADS * ML_QK
_O_MLI = _O_MLV + ML_OUT
_O_MLF = _O_MLI + N_DIR * ML_HEADS
_O_MLO = _O_MLF + N_DIR * ML_HEADS
_O_GATES = _O_MLO + ML_OUT
_O_END = _O_GATES + 2 * D_MODEL

N_GATE = N_DIR * ML_HEADS
QK_PAD = ML_HEADS * HEAD_PAD

_P_CQ = 0
_P_CKV = _P_CQ + MLA_Q_RANK
_P_MLQK = _P_CKV + MLA_KV_RANK
_P_MLV = _P_MLQK + 2 * QK_PAD
_P_MLO = _P_MLV + ML_OUT
_P_GATES = _P_MLO + ML_OUT
_P_SMALL = _P_GATES + 2 * D_MODEL
_P_END = _P_SMALL + LANES


def _rms(x, w):
    ms = jnp.mean(x * x, axis=-1, keepdims=True)
    return x * lax.rsqrt(ms + EPS) * w


def _sigmoid(x):
    return 1.0 / (1.0 + jnp.exp(-x))


def _dot(a, b):
    return jnp.dot(a, b, preferred_element_type=F32)


def _const_spec(shape):
    nd = len(shape)
    return pl.BlockSpec(shape, lambda *_: (0,) * nd, pipeline_mode=pl.Buffered(1))


def _params(sem):
    return pltpu.CompilerParams(dimension_semantics=sem, vmem_limit_bytes=VMEM_LIMIT)


def _ffn_body(x, pre, post, wg, wu, wd):
    u = _rms(x, pre).astype(BF16)
    g = _dot(u, wg)
    up = _dot(u, wu)
    hid = (g * _sigmoid(g) * up).astype(BF16)
    y = _dot(hid, wd)
    return x + 0.5 * _rms(y, post)


def _ffn_kernel(x_ref, pre_ref, post_ref, wg_ref, wu_ref, wd_ref, o_ref):
    o_ref[...] = _ffn_body(x_ref[...], pre_ref[...], post_ref[...],
                           wg_ref[...], wu_ref[...], wd_ref[...])


def _ffn_ple_kernel(x_ref, pre_ref, post_ref, wg_ref, wu_ref, wd_ref,
                    p_ref, ppre_ref, ppost_ref, wproj_ref, wgate_ref, o_ref):
    h = _ffn_body(x_ref[...], pre_ref[...], post_ref[...],
                  wg_ref[...], wu_ref[...], wd_ref[...])
    e = _dot(p_ref[...].astype(BF16), wproj_ref[...])
    g = _sigmoid(_dot(_rms(h, ppre_ref[...]).astype(BF16), wgate_ref[...]))
    o_ref[...] = h + _rms(g * e, ppost_ref[...])


def _ffn(x, pre, post, wg, wu, wd, ple=None, tm=512):
    s, d = x.shape
    row = pl.BlockSpec((tm, d), lambda i: (i, 0))
    in_specs = [row, _const_spec((1, d)), _const_spec((1, d)),
                _const_spec(wg.shape), _const_spec(wu.shape), _const_spec(wd.shape)]
    args = [x, pre, post, wg, wu, wd]
    kern = _ffn_kernel
    if ple is not None:
        p, ppre, ppost, wproj, wgate = ple
        in_specs += [pl.BlockSpec((tm, p.shape[1]), lambda i: (i, 0)),
                     _const_spec((1, d)), _const_spec((1, d)),
                     _const_spec(wproj.shape), _const_spec(wgate.shape)]
        args += [p, ppre, ppost, wproj, wgate]
        kern = _ffn_ple_kernel
    return pl.pallas_call(
        kern, grid=(s // tm,), in_specs=in_specs, out_specs=row,
        out_shape=jax.ShapeDtypeStruct((s, d), F32),
        compiler_params=_params(("parallel",)),
        name="ffn" if ple is None else "ffn_ple",
    )(*args)


def _dot_nt(a, b):
    return lax.dot_general(a, b, (((1,), (1,)), ((), ())), preferred_element_type=F32)


def _rope_tables_t(pos_ref, invf_ref):
    ang = pos_ref[...].astype(F32) * invf_ref[...]
    c = jnp.cos(ang)
    s = jnp.sin(ang)
    r = lax.broadcasted_iota(jnp.int32, (HEAD_PAD, 1), 0)
    lo = (r >= ROPE_LO) & (r < ROPE_LO + ROPE_HALF)
    hi = (r >= ROPE_LO + ROPE_HALF) & (r < ROPE_LO + MLA_ROPE)
    s_from_lo = jnp.where(hi, s, 0.0)
    s_from_hi = jnp.where(lo, -s, 0.0)
    return c, s_from_lo, s_from_hi, lo | hi


def _rope_t(x, c, s_from_lo, s_from_hi):
    return (x * c + pltpu.roll(x, ROPE_HALF, 0) * s_from_lo
            + pltpu.roll(x, HEAD_PAD - ROPE_HALF, 0) * s_from_hi)


def _proj_kernel(h_ref, pos_ref, invf_ref, pre_ref, win_ref, wsmt_ref, gbias_ref, qn_ref, kvn_ref,
                 wuqt_ref, wuk_ref, wuvt_ref, vone_ref,
                 qt_ref, k_ref, vt_ref, mlqk_ref, mlv_ref, gt_ref, cum_ref, mlo_ref, gates_ref):
    u = _rms(h_ref[...], pre_ref[...]).astype(BF16)
    z = _dot(u, win_ref[...])
    mlqk_ref[...] = z[:, _P_MLQK:_P_MLV]
    mlv_ref[...] = z[:, _P_MLV:_P_MLO].astype(BF16)
    mlo_ref[...] = z[:, _P_MLO:_P_GATES]
    gates_ref[...] = z[:, _P_GATES:_P_SMALL]

    c, s_lo, s_hi, rope_rows = _rope_tables_t(pos_ref, invf_ref)
    small_t = _dot_nt(wsmt_ref[...], u)

    pre = small_t[0:2 * N_GATE, :] + gbias_ref[...]
    is_f = lax.broadcasted_iota(jnp.int32, (2 * N_GATE, 1), 0) >= N_GATE
    gate = jnp.where(is_f, jax.nn.log_sigmoid(pre), pre)
    gt_ref[...] = gate
    r = lax.broadcasted_iota(jnp.int32, (ML_CHUNK, ML_CHUNK), 0)
    cidx = lax.broadcasted_iota(jnp.int32, (ML_CHUNK, ML_CHUNK), 1)
    for d, tri in enumerate((r <= cidx, r >= cidx)):
        for ch in range(gate.shape[1] // ML_CHUNK):
            cs = slice(ch * ML_CHUNK, (ch + 1) * ML_CHUNK)
            cum_ref[d * 2 * N_GATE:(d + 1) * 2 * N_GATE, cs] = jnp.dot(
                gate[:, cs], tri.astype(F32), preferred_element_type=F32,
                precision=lax.Precision.HIGHEST)

    k_rope = jnp.where(rope_rows, _rope_t(small_t, c, s_lo, s_hi), 0.0).T

    cq = _rms(z[:, _P_CQ:_P_CKV], qn_ref[...]).astype(BF16)
    ckv = _rms(z[:, _P_CKV:_P_MLQK], kvn_ref[...]).astype(BF16)
    qpt = _dot_nt(wuqt_ref[...], cq)
    kp = _dot(ckv, wuk_ref[...])
    vt_ref[...] = (_dot_nt(wuvt_ref[...], ckv) + vone_ref[...]).astype(BF16)
    for hd in range(MLA_HEADS):
        blk = slice(hd * HEAD_PAD, (hd + 1) * HEAD_PAD)
        qt_ref[blk, :] = (_rope_t(qpt[blk, :], c, s_lo, s_hi) * Q_SCALE).astype(BF16)
        k_ref[hd] = (kp[:, blk] + k_rope).astype(BF16)


def _proj(h, pos, invf, pre, win, wsmt, gbias, qn, kvn, wuqt, wuk, wuvt, vone, tm=256):
    s, d = h.shape
    ng = 2 * N_GATE
    row = lambda w: pl.BlockSpec((tm, w), lambda i: (i, 0))
    col = lambda r: pl.BlockSpec((r, tm), lambda i: (0, i))
    cshape = jax.ShapeDtypeStruct((MLA_HEADS * HEAD_PAD, s), BF16)
    return pl.pallas_call(
        _proj_kernel, grid=(s // tm,),
        in_specs=[row(d), col(1), _const_spec(invf.shape), _const_spec((1, d)),
                  _const_spec(win.shape), _const_spec(wsmt.shape), _const_spec(gbias.shape),
                  _const_spec(qn.shape), _const_spec(kvn.shape),
                  _const_spec(wuqt.shape), _const_spec(wuk.shape), _const_spec(wuvt.shape),
                  _const_spec(vone.shape)],
        out_specs=[col(MLA_HEADS * HEAD_PAD),
                   pl.BlockSpec((MLA_HEADS, tm, HEAD_PAD), lambda i: (0, i, 0)),
                   col(MLA_HEADS * HEAD_PAD),
                   row(2 * QK_PAD), row(ML_OUT), col(ng), col(N_DIR * ng),
                   row(ML_OUT), row(2 * D_MODEL)],
        out_shape=[cshape, jax.ShapeDtypeStruct((MLA_HEADS, s, HEAD_PAD), BF16), cshape,
                   jax.ShapeDtypeStruct((s, 2 * QK_PAD), F32),
                   jax.ShapeDtypeStruct((s, ML_OUT), BF16),
                   jax.ShapeDtypeStruct((ng, s), F32),
                   jax.ShapeDtypeStruct((N_DIR * ng, s), F32),
                   jax.ShapeDtypeStruct((s, ML_OUT), F32),
                   jax.ShapeDtypeStruct((s, 2 * D_MODEL), F32)],
        compiler_params=_params(("parallel",)),
        name="proj",
    )(h, pos, invf, pre, win, wsmt, gbias, qn, kvn, wuqt, wuk, wuvt, vone)


BF16_ROWS = 16
V_ROWS = -(-(MLA_V + 1) // BF16_ROWS) * BF16_ROWS


def _attn_kernel(qt_ref, k_ref, vt_ref, o_ref, acc_sc, st0, st1, p0, p1, *, tk, steps):
    qt = qt_ref[...]
    tq = qt.shape[1]
    n = k_ref.shape[1] // tk
    st, pb = (st0, st1), (p0, p1)

    acc_sc[...] = jnp.zeros(acc_sc.shape, F32)
    p1[...] = jnp.zeros(p1.shape, BF16)
    st0[...] = _dot(k_ref[0, 0:tk, :], qt)

    def accumulate(t_prev, slot, alpha):
        off = pl.multiple_of(t_prev * tk, tk)
        vt = vt_ref[0:V_ROWS, pl.ds(off, tk)]
        acc_sc[0:V_ROWS, :] = alpha * acc_sc[0:V_ROWS, :] + _dot(vt, pb[slot][...])

    def step(t, slot, m, alpha_prev):
        off = pl.multiple_of(jnp.minimum(t + 1, n - 1) * tk, tk)
        st[1 - slot][...] = _dot(k_ref[0, pl.ds(off, tk), :], qt)
        accumulate(jnp.maximum(t - 1, 0), 1 - slot, alpha_prev)
        x = st[slot][...]
        m_new = jnp.maximum(m, jnp.max(x, axis=0, keepdims=True))
        pb[slot][...] = jnp.exp2(x - m_new).astype(BF16)
        return m_new, jnp.exp2(m - m_new)

    def body(i, carry):
        m, alpha = carry
        for j in range(steps):
            m, alpha = step(i * steps + j, j % 2, m, alpha)
        return m, alpha

    init = (jnp.full((1, tq), -jnp.inf, F32), jnp.ones((1, tq), F32))
    _, alpha = lax.fori_loop(0, n // steps, body, init)
    accumulate(n - 1, (n - 1) % 2, alpha)
    acc = acc_sc[...]
    o_ref[...] = (acc / acc[MLA_V:MLA_V + 1, :]).T.astype(o_ref.dtype)


def _attn(qt, k, vt, tq=512, tk=256, steps=32):
    nh, s, w = k.shape
    assert steps % 2 == 0 and (s // tk) % steps == 0
    return pl.pallas_call(
        functools.partial(_attn_kernel, tk=tk, steps=steps), grid=(nh, s // tq),
        in_specs=[pl.BlockSpec((w, tq), lambda h, i: (h, i)),
                  pl.BlockSpec((1, s, w), lambda h, i: (h, 0, 0)),
                  pl.BlockSpec((w, s), lambda h, i: (h, 0))],
        out_specs=pl.BlockSpec((tq, w), lambda h, i: (i, h)),
        out_shape=jax.ShapeDtypeStruct((s, nh * w), BF16),
        scratch_shapes=[pltpu.VMEM((w, tq), F32),
                        pltpu.VMEM((tk, tq), F32), pltpu.VMEM((tk, tq), F32),
                        pltpu.VMEM((tk, tq), BF16), pltpu.VMEM((tk, tq), BF16)],
        compiler_params=_params(("parallel", "parallel")),
        name="attn",
    )(qt, k, vt)


CONV_HALO = 8


def _conv_kernel(x_ref, prev_ref, next_ref, w_ref, b_ref, q_ref, kt_ref, buf, *, tm):
    i = pl.program_id(0)
    n = pl.num_programs(0)
    buf[CONV_HALO:CONV_HALO + tm, :] = x_ref[...]
    buf[0:CONV_HALO, :] = jnp.where(i > 0, prev_ref[...], 0.0)
    buf[CONV_HALO + tm:, :] = jnp.where(i < n - 1, next_ref[...], 0.0)
    acc = jnp.broadcast_to(b_ref[...], (tm, x_ref.shape[1]))
    for j in range(CONV_W):
        start = CONV_HALO - CONV_W // 2 + j
        acc = acc + w_ref[j:j + 1, :] * buf[start:start + tm, :]
    y = acc * _sigmoid(acc)
    q_ref[...] = y[:, :QK_PAD].astype(BF16)
    kt_ref[...] = (y[:, QK_PAD:] * ML_QK ** -0.5).T


def _conv(x, w, b, tm=512):
    s, c = x.shape
    r = tm // CONV_HALO
    nblk = s // CONV_HALO
    return pl.pallas_call(
        functools.partial(_conv_kernel, tm=tm), grid=(s // tm,),
        in_specs=[pl.BlockSpec((tm, c), lambda i: (i, 0)),
                  pl.BlockSpec((CONV_HALO, c), lambda i: (jnp.maximum(i * r - 1, 0), 0)),
                  pl.BlockSpec((CONV_HALO, c), lambda i: (jnp.minimum((i + 1) * r, nblk - 1), 0)),
                  _const_spec(w.shape), _const_spec(b.shape)],
        out_specs=[pl.BlockSpec((tm, QK_PAD), lambda i: (i, 0)),
                   pl.BlockSpec((QK_PAD, tm), lambda i: (0, i))],
        out_shape=[jax.ShapeDtypeStruct((s, QK_PAD), BF16),
                   jax.ShapeDtypeStruct((QK_PAD, s), F32)],
        scratch_shapes=[pltpu.VMEM((tm + 2 * CONV_HALO, c), F32)],
        compiler_params=_params(("parallel",)),
        name="conv",
    )(x, x, x, w, b)


def _mlstm_kernel(qf_ref, ktf_ref, vf_ref, gf_ref, cumf_ref, qb_ref, ktb_ref, vb_ref, gb_ref,
                  cumb_ref, hf_ref, hb_ref, c_sc, m_sc):
    L = ML_CHUNK

    @pl.when(pl.program_id(0) == 0)
    def _():
        c_sc[...] = jnp.zeros(c_sc.shape, F32)
        m_sc[...] = jnp.zeros(m_sc.shape, F32)

    row = lax.broadcasted_iota(jnp.int32, (L, L), 0)
    col = lax.broadcasted_iota(jnp.int32, (L, L), 1)
    ones_blk = jnp.ones((L, LANES), BF16)
    refs = ((qf_ref, ktf_ref, vf_ref, gf_ref, hf_ref), (qb_ref, ktb_ref, vb_ref, gb_ref, hb_ref))
    heads = [(d, hd) for d in range(N_DIR) for hd in range(ML_HEADS)]
    blk = lambda hd: slice(hd * HEAD_PAD, (hd + 1) * HEAD_PAD)

    gate = [refs[d][3][...] for d in range(N_DIR)]
    cum = [cumf_ref[...], cumb_ref[...]]
    mask = [col <= row, col >= row]

    q, v_aug, qk, qc = {}, {}, {}, {}
    for d, hd in heads:
        j = d * ML_HEADS + hd
        q[j] = refs[d][0][:, blk(hd)]
        v_aug[j] = jnp.concatenate([refs[d][2][:, blk(hd)], ones_blk], axis=1)
        qk[j] = _dot(q[j], refs[d][1][blk(hd), :].astype(BF16))
        qc[j] = _dot(q[j], c_sc[j].astype(BF16))

    m_prev = {}
    for d, hd in heads:
        j = d * ML_HEADS + hd
        br = cum[d][N_GATE + j:N_GATE + j + 1, :]
        ir = gate[d][j:j + 1, :]
        m_prev[j] = m_sc[j][0:1, 0:1]
        b_end = br[:, L - 1:L] if d == 0 else br[:, 0:1]
        g_row = b_end - br + ir
        m_new = jnp.maximum(b_end + m_prev[j], jnp.max(g_row, axis=1, keepdims=True))
        decay = jnp.exp(b_end + m_prev[j] - m_new)
        kw_t = (refs[d][1][blk(hd), :] * jnp.exp(g_row - m_new)).astype(BF16)
        c_sc[j] = decay * c_sc[j] + _dot(kw_t, v_aug[j])
        m_sc[j] = jnp.broadcast_to(m_new, m_sc.shape[1:])

    a, mm, mm_b, bc, sv = {}, {}, {}, {}, {}
    for d, hd in heads:
        j = d * ML_HEADS + hd
        br = cum[d][N_GATE + j:N_GATE + j + 1, :]
        lf = gate[d][N_GATE + j:N_GATE + j + 1, :]
        a[j] = jnp.where(mask[d], gate[d][j:j + 1, :] - br, -jnp.inf)
        mm[j] = jnp.maximum(m_prev[j], jnp.max(a[j], axis=1, keepdims=True))
        bc[j] = jnp.sum(jnp.where(mask[d], lf, 0.0), axis=1, keepdims=True)
    for d, hd in heads:
        j = d * ML_HEADS + hd
        mm_b[j] = jnp.broadcast_to(mm[j], (L, L))
        smat = qk[j] * jnp.exp(a[j] - mm_b[j])
        sv[j] = _dot(smat.astype(BF16), v_aug[j])
    for d, hd in heads:
        j = d * ML_HEADS + hd
        inter = jnp.exp(m_prev[j] - mm_b[j])
        nv = sv[j] + jnp.concatenate([inter, inter], axis=1) * qc[j]
        den = nv[:, ML_V:ML_V + 1]
        inv = 1.0 / jnp.maximum(jnp.abs(den), jnp.exp(-(bc[j] + mm[j])))
        refs[d][4][:, blk(hd)] = nv[:, :ML_V] * inv


def _mlstm(q, kt, v, gate_t, cum_t):
    s = q.shape[0]
    L = ML_CHUNK
    nc = s // L
    ng = 2 * N_GATE
    rows = lambda w, f: pl.BlockSpec((L, w), lambda c: (f(c), 0))
    cols = lambda h, f, r=0: pl.BlockSpec((h, L), lambda c: (r, f(c)))
    fwd = lambda c: c
    bwd = lambda c: nc - 1 - c
    hshape = jax.ShapeDtypeStruct((s, ML_OUT), F32)
    return pl.pallas_call(
        _mlstm_kernel, grid=(nc,),
        in_specs=[rows(QK_PAD, fwd), cols(QK_PAD, fwd), rows(ML_OUT, fwd),
                  cols(ng, fwd), cols(ng, fwd, 0),
                  rows(QK_PAD, bwd), cols(QK_PAD, bwd), rows(ML_OUT, bwd),
                  cols(ng, bwd), cols(ng, bwd, 1)],
        out_specs=[rows(ML_OUT, fwd), rows(ML_OUT, bwd)],
        out_shape=[hshape, hshape],
        scratch_shapes=[pltpu.VMEM((N_GATE, HEAD_PAD, ML_V + LANES), F32),
                        pltpu.VMEM((N_GATE, 8, LANES), F32)],
        compiler_params=_params(("arbitrary",)),
        name="mlstm",
    )(q, kt, v, gate_t, cum_t, q, kt, v, gate_t, cum_t)


def _merge_kernel(h_ref, att_ref, hf_ref, hb_ref, mlo_ref, gates_ref, hn_ref,
                  wba_ref, wbm_ref, wout_ref, post_ref, o_ref):
    a = _dot(att_ref[...], wba_ref[...])
    hm = hf_ref[...] + hb_ref[...]
    heads = []
    for hd in range(ML_HEADS):
        blk = slice(hd * ML_V, (hd + 1) * ML_V)
        heads.append(_rms(hm[:, blk], hn_ref[:, blk]))
    hm = jnp.concatenate(heads, axis=1) * _sigmoid(mlo_ref[...])
    bm = _dot(hm.astype(BF16), wbm_ref[...])
    g = _sigmoid(gates_ref[...])
    mixed = _dot((g[:, :D_MODEL] * a + g[:, D_MODEL:] * bm).astype(BF16), wout_ref[...])
    o_ref[...] = h_ref[...] + _rms(mixed, post_ref[...])


def _merge(h, att, hf, hb, mlo, gates, hn, wba, wbm, wout, post, tm=512):
    s, d = h.shape
    row = lambda w: pl.BlockSpec((tm, w), lambda i: (i, 0))
    return pl.pallas_call(
        _merge_kernel, grid=(s // tm,),
        in_specs=[row(d), row(att.shape[1]), row(ML_OUT), row(ML_OUT), row(ML_OUT),
                  row(2 * d), _const_spec(hn.shape), _const_spec(wba.shape),
                  _const_spec(wbm.shape), _const_spec(wout.shape), _const_spec(post.shape)],
        out_specs=row(d),
        out_shape=jax.ShapeDtypeStruct((s, d), F32),
        compiler_params=_params(("parallel",)),
        name="merge",
    )(h, att, hf, hb, mlo, gates, hn, wba, wbm, wout, post)


def _pad_heads(w, nh, width):
    r = w.shape[0]
    w = w.reshape(r, nh, width)
    return jnp.pad(w, ((0, 0), (0, 0), (0, HEAD_PAD - width))).reshape(r, nh * HEAD_PAD)


def _permute_w_in(w_in):
    d = w_in.shape[0]
    small = jnp.zeros((d, LANES), w_in.dtype)
    small = small.at[:, 0:N_GATE].set(w_in[:, _O_MLI:_O_MLF])
    small = small.at[:, N_GATE:2 * N_GATE].set(w_in[:, _O_MLF:_O_MLO])
    small = small.at[:, ROPE_LO:ROPE_LO + MLA_ROPE].set(w_in[:, _O_KR:_O_MLQ])
    return jnp.concatenate([
        w_in[:, _O_CQ:_O_KR],
        _pad_heads(w_in[:, _O_MLQ:_O_MLK], ML_HEADS, ML_QK),
        _pad_heads(w_in[:, _O_MLK:_O_MLV], ML_HEADS, ML_QK),
        w_in[:, _O_MLV:_O_MLI],
        w_in[:, _O_MLO:_O_END],
        small], axis=1)


def kernel(x, p, positions, ffn1_pre_norm, ffn1_post_norm, ffn1_w_gate, ffn1_w_up, ffn1_w_down, mix_pre_norm, mix_post_norm, w_in, mla_q_norm, mla_kv_norm, mla_w_uq, mla_w_uk, mla_w_uv, ml_conv_w, ml_conv_b, ml_i_bias, ml_f_bias, ml_head_norm, w_branch_mla, w_branch_ml, w_out, ffn2_pre_norm, ffn2_post_norm, ffn2_w_gate, ffn2_w_up, ffn2_w_down, ple_pre_norm, ple_post_norm, ple_w_proj, ple_w_gate):
    depth, batch = p.shape[0], x.shape[0]
    assert depth == 1 and batch == 1, "kernel is specialised to DEPTH == 1, BATCH == 1"
    bf = lambda w: w.astype(BF16)
    h = x[0]
    pos = positions

    inv_freq = ROPE_THETA ** (-jnp.arange(0, MLA_ROPE, 2, dtype=F32) / MLA_ROPE)
    invf = jnp.zeros((HEAD_PAD, 1), F32)
    invf = invf.at[ROPE_LO:ROPE_LO + ROPE_HALF, 0].set(inv_freq)
    invf = invf.at[ROPE_LO + ROPE_HALF:ROPE_LO + MLA_ROPE, 0].set(inv_freq)

    i = 0
    h = _ffn(h, ffn1_pre_norm[i][None], ffn1_post_norm[i][None],
             bf(ffn1_w_gate[i]), bf(ffn1_w_up[i]), bf(ffn1_w_down[i]))

    vone = jnp.zeros((MLA_HEADS, HEAD_PAD), F32).at[:, MLA_V].set(1.0).reshape(-1, 1)
    win_p = bf(_permute_w_in(w_in[i]))
    gate_bias = jnp.concatenate([ml_i_bias[i].reshape(-1), ml_f_bias[i].reshape(-1)])[:, None]
    qt, k, vt, mlqk, mlv, gate_t, cum_t, mlo, gates = _proj(
        h, pos, invf, mix_pre_norm[i][None], win_p[:, :_P_SMALL], win_p[:, _P_SMALL:_P_END].T,
        gate_bias, mla_q_norm[i][None], mla_kv_norm[i][None],
        bf(_pad_heads(mla_w_uq[i], MLA_HEADS, MLA_NOPE + MLA_ROPE)).T,
        bf(_pad_heads(mla_w_uk[i], MLA_HEADS, MLA_NOPE)),
        bf(_pad_heads(mla_w_uv[i], MLA_HEADS, MLA_V)).T, vone)

    att = _attn(qt, k, vt)

    conv_w = jnp.concatenate([_pad_heads(ml_conv_w[i][:, :ML_HEADS * ML_QK], ML_HEADS, ML_QK),
                              _pad_heads(ml_conv_w[i][:, ML_HEADS * ML_QK:], ML_HEADS, ML_QK)], axis=1)
    conv_w = jnp.pad(conv_w, ((0, 8 - CONV_W), (0, 0)))
    conv_b = jnp.concatenate([_pad_heads(ml_conv_b[i][None, :ML_HEADS * ML_QK], ML_HEADS, ML_QK),
                              _pad_heads(ml_conv_b[i][None, ML_HEADS * ML_QK:], ML_HEADS, ML_QK)], axis=1)
    q_conv, kt_conv = _conv(mlqk, conv_w, conv_b)

    hf, hb = _mlstm(q_conv, kt_conv, mlv, gate_t, cum_t)

    wba = jnp.pad(w_branch_mla[i].reshape(MLA_HEADS, MLA_V, D_MODEL),
                  ((0, 0), (0, HEAD_PAD - MLA_V), (0, 0))).reshape(MLA_HEADS * HEAD_PAD, D_MODEL)
    h = _merge(h, att, hf, hb, mlo, gates, ml_head_norm[i][None], bf(wba),
               bf(w_branch_ml[i]), bf(w_out[i]), mix_post_norm[i][None])

    h = _ffn(h, ffn2_pre_norm[i][None], ffn2_post_norm[i][None],
             bf(ffn2_w_gate[i]), bf(ffn2_w_up[i]), bf(ffn2_w_down[i]),
             ple=(p[i, 0], ple_pre_norm[i][None], ple_post_norm[i][None],
                  bf(ple_w_proj[i]), bf(ple_w_gate[i])))
    return h[None]
```

```python
import functools
import math

import jax
import jax.numpy as jnp
from jax import lax
from jax.experimental import pallas as pl
from jax.experimental.pallas import tpu as pltpu

D_MODEL = 1024
SEQ = 16384
PLE_DIM = 256
D_FF = 2816
EPS = 1e-6
MLA_HEADS = 8
MLA_Q_RANK = 256
MLA_KV_RANK = 256
MLA_NOPE = 64
MLA_ROPE = 32
MLA_V = 64
ROPE_THETA = 10000.0
ML_HEADS = 4
ML_QK = 64
ML_V = 128
ML_CHUNK = 128
CONV_W = 5
N_DIR = 2
MLA_OUT = MLA_HEADS * MLA_V
ML_OUT = ML_HEADS * ML_V

LANES = 128
HEAD_PAD = LANES
ROPE_LO = MLA_NOPE
ROPE_HALF = MLA_ROPE // 2
VMEM_LIMIT = 56 * 1024 * 1024

F32 = jnp.float32
BF16 = jnp.bfloat16
LOG2E = math.log2(math.e)
Q_SCALE = (MLA_NOPE + MLA_ROPE) ** -0.5 * LOG2E

_O_CQ = 0
_O_CKV = _O_CQ + MLA_Q_RANK
_O_KR = _O_CKV + MLA_KV_RANK
_O_MLQ = _O_KR + MLA_ROPE
_O_MLK = _O_MLQ + ML_HEADS * ML_QK
_O_MLV = _O_MLK + ML_HEADS * ML_QK
_O_MLI = _O_MLV + ML_OUT
_O_MLF = _O_MLI + N_DIR * ML_HEADS
_O_MLO = _O_MLF + N_DIR * ML_HEADS
_O_GATES = _O_MLO + ML_OUT
_O_END = _O_GATES + 2 * D_MODEL

N_GATE = N_DIR * ML_HEADS
QK_W = ML_HEADS * ML_QK
QK_PAD = ML_HEADS * HEAD_PAD

_P_CQ = 0
_P_CKV = _P_CQ + MLA_Q_RANK
_P_MLQK = _P_CKV + MLA_KV_RANK
_P_MLV = _P_MLQK + 2 * QK_W
_P_MLO = _P_MLV + ML_OUT
_P_GATES = _P_MLO + ML_OUT
_P_END = _P_GATES + 2 * D_MODEL


def _rms(x, w):
    ms = jnp.mean(x * x, axis=-1, keepdims=True)
    return x * lax.rsqrt(ms + EPS) * w


def _sigmoid(x):
    return 1.0 / (1.0 + jnp.exp(-x))


def _dot(a, b):
    return jnp.dot(a, b, preferred_element_type=F32)


def _const_spec(shape):
    nd = len(shape)
    return pl.BlockSpec(shape, lambda *_: (0,) * nd, pipeline_mode=pl.Buffered(1))


def _params(sem):
    return pltpu.CompilerParams(dimension_semantics=sem, vmem_limit_bytes=VMEM_LIMIT)


def _ffn_body(x, pre, post, wg, wu, wd):
    u = _rms(x, pre).astype(BF16)
    g = _dot(u, wg)
    up = _dot(u, wu)
    hid = (g * _sigmoid(g) * up).astype(BF16)
    y = _dot(hid, wd)
    return x + 0.5 * _rms(y, post)


def _ffn_kernel(x_ref, pre_ref, post_ref, wg_ref, wu_ref, wd_ref, o_ref):
    o_ref[...] = _ffn_body(x_ref[...], pre_ref[...], post_ref[...],
                           wg_ref[...], wu_ref[...], wd_ref[...])


def _ffn_ple_kernel(x_ref, pre_ref, post_ref, wg_ref, wu_ref, wd_ref,
                    p_ref, ppre_ref, ppost_ref, wproj_ref, wgate_ref, o_ref):
    h = _ffn_body(x_ref[...], pre_ref[...], post_ref[...],
                  wg_ref[...], wu_ref[...], wd_ref[...])
    e = _dot(p_ref[...].astype(BF16), wproj_ref[...])
    g = _sigmoid(_dot(_rms(h, ppre_ref[...]).astype(BF16), wgate_ref[...]))
    o_ref[...] = h + _rms(g * e, ppost_ref[...])


def _ffn(x, pre, post, wg, wu, wd, ple=None, tm=512):
    s, d = x.shape
    row = pl.BlockSpec((tm, d), lambda i: (i, 0))
    in_specs = [row, _const_spec((1, d)), _const_spec((1, d)),
                _const_spec(wg.shape), _const_spec(wu.shape), _const_spec(wd.shape)]
    args = [x, pre, post, wg, wu, wd]
    kern = _ffn_kernel
    if ple is not None:
        p, ppre, ppost, wproj, wgate = ple
        in_specs += [pl.BlockSpec((tm, p.shape[1]), lambda i: (i, 0)),
                     _const_spec((1, d)), _const_spec((1, d)),
                     _const_spec(wproj.shape), _const_spec(wgate.shape)]
        args += [p, ppre, ppost, wproj, wgate]
        kern = _ffn_ple_kernel
    return pl.pallas_call(
        kern, grid=(s // tm,), in_specs=in_specs, out_specs=row,
        out_shape=jax.ShapeDtypeStruct((s, d), F32),
        compiler_params=_params(("parallel",)),
        name="ffn" if ple is None else "ffn_ple",
    )(*args)


def _dot_nt(a, b):
    return lax.dot_general(a, b, (((1,), (1,)), ((), ())), preferred_element_type=F32)


def _rope_tables_t(pos_ref, invf_ref):
    ang = pos_ref[...].astype(F32) * invf_ref[...]
    return jnp.cos(ang), jnp.sin(ang)


def _rope_t(x, c, s):
    x1, x2 = x[:ROPE_HALF], x[ROPE_HALF:]
    return x1 * c - x2 * s, x2 * c + x1 * s


def _proj_kernel(h_ref, pos_ref, invf_ref, pre_ref, win_ref, wsmt_ref, gbias_ref, qn_ref, kvn_ref,
                 wuqt_ref, wuk_ref, wuvt_ref, vone_ref,
                 qt_ref, k_ref, vt_ref, mlqk_ref, mlv_ref, gt_ref, cum_ref, mlo_ref, gates_ref):
    u = _rms(h_ref[...], pre_ref[...]).astype(BF16)
    zdot = lambda a, b: _dot(u, win_ref[:, a:b])
    c, s = _rope_tables_t(pos_ref, invf_ref)
    rope = slice(ROPE_LO, ROPE_LO + MLA_ROPE)
    small_t = _dot_nt(wsmt_ref[...], u)
    z_c = zdot(_P_CQ, _P_MLQK)
    gates_ref[...] = zdot(_P_GATES, _P_END)

    pre = small_t[0:2 * N_GATE, :] + gbias_ref[...]
    is_f = lax.broadcasted_iota(jnp.int32, (2 * N_GATE, 1), 0) >= N_GATE
    gate = jnp.where(is_f, jax.nn.log_sigmoid(pre), pre)
    gt_ref[...] = gate
    r = lax.broadcasted_iota(jnp.int32, (ML_CHUNK, ML_CHUNK), 0)
    cidx = lax.broadcasted_iota(jnp.int32, (ML_CHUNK, ML_CHUNK), 1)
    for d, tri in enumerate((r <= cidx, r >= cidx)):
        for ch in range(gate.shape[1] // ML_CHUNK):
            cs = slice(ch * ML_CHUNK, (ch + 1) * ML_CHUNK)
            cum_ref[d * 2 * N_GATE:(d + 1) * 2 * N_GATE, cs] = jnp.dot(
                gate[:, cs], tri.astype(F32), preferred_element_type=F32,
                precision=lax.Precision.HIGHEST)

    tm = small_t.shape[1]
    k_rope = jnp.concatenate(
        [jnp.zeros((ROPE_LO, tm), F32), *_rope_t(small_t[rope, :], c, s),
         jnp.zeros((HEAD_PAD - ROPE_LO - MLA_ROPE, tm), F32)], axis=0).T

    cq = _rms(z_c[:, :MLA_Q_RANK], qn_ref[...]).astype(BF16)
    ckv = _rms(z_c[:, MLA_Q_RANK:], kvn_ref[...]).astype(BF16)
    qpt = _dot_nt(wuqt_ref[...], cq)
    kp = _dot(ckv, wuk_ref[...])
    vt_ref[...] = (_dot_nt(wuvt_ref[...], ckv) + vone_ref[...]).astype(BF16)
    mlqk_ref[...] = zdot(_P_MLQK, _P_MLV)
    mlv_ref[...] = zdot(_P_MLV, _P_MLO).astype(BF16)
    mlo_ref[...] = zdot(_P_MLO, _P_GATES)
    for hd in range(MLA_HEADS):
        blk = slice(hd * HEAD_PAD, (hd + 1) * HEAD_PAD)
        x = qpt[blk, :]
        r1, r2 = _rope_t(x[rope, :], c, s)
        qt_ref[blk, :] = jnp.concatenate(
            [x[:ROPE_LO] * Q_SCALE, r1 * Q_SCALE, r2 * Q_SCALE, x[ROPE_LO + MLA_ROPE:]],
            axis=0).astype(BF16)
        k_ref[hd] = (kp[:, blk] + k_rope).astype(BF16)


def _proj(h, pos, invf, pre, win, wsmt, gbias, qn, kvn, wuqt, wuk, wuvt, vone, tm=512):
    s, d = h.shape
    ng = 2 * N_GATE
    row = lambda w: pl.BlockSpec((tm, w), lambda i: (i, 0))
    col = lambda r: pl.BlockSpec((r, tm), lambda i: (0, i))
    cshape = jax.ShapeDtypeStruct((MLA_HEADS * HEAD_PAD, s), BF16)
    return pl.pallas_call(
        _proj_kernel, grid=(s // tm,),
        in_specs=[row(d), col(1), _const_spec(invf.shape), _const_spec((1, d)),
                  _const_spec(win.shape), _const_spec(wsmt.shape), _const_spec(gbias.shape),
                  _const_spec(qn.shape), _const_spec(kvn.shape),
                  _const_spec(wuqt.shape), _const_spec(wuk.shape), _const_spec(wuvt.shape),
                  _const_spec(vone.shape)],
        out_specs=[col(MLA_HEADS * HEAD_PAD),
                   pl.BlockSpec((MLA_HEADS, tm, HEAD_PAD), lambda i: (0, i, 0)),
                   col(MLA_HEADS * HEAD_PAD),
                   row(2 * QK_W), row(ML_OUT), col(ng), col(N_DIR * ng),
                   row(ML_OUT), row(2 * D_MODEL)],
        out_shape=[cshape, jax.ShapeDtypeStruct((MLA_HEADS, s, HEAD_PAD), BF16), cshape,
                   jax.ShapeDtypeStruct((s, 2 * QK_W), F32),
                   jax.ShapeDtypeStruct((s, ML_OUT), BF16),
                   jax.ShapeDtypeStruct((ng, s), F32),
                   jax.ShapeDtypeStruct((N_DIR * ng, s), F32),
                   jax.ShapeDtypeStruct((s, ML_OUT), F32),
                   jax.ShapeDtypeStruct((s, 2 * D_MODEL), F32)],
        compiler_params=_params(("parallel",)),
        name="proj",
    )(h, pos, invf, pre, win, wsmt, gbias, qn, kvn, wuqt, wuk, wuvt, vone)


BF16_ROWS = 16
V_ROWS = -(-(MLA_V + 1) // BF16_ROWS) * BF16_ROWS


def _attn_kernel(qt_ref, k_ref, vt_ref, o_ref, acc_sc, st0, st1, p0, p1, *, tk, steps):
    qt = qt_ref[...]
    tq = qt.shape[1]
    n = k_ref.shape[1] // tk
    st, pb = (st0, st1), (p0, p1)

    acc_sc[...] = jnp.zeros(acc_sc.shape, F32)
    p1[...] = jnp.zeros(p1.shape, BF16)
    st0[...] = _dot(k_ref[0, 0:tk, :], qt)

    def accumulate(t_prev, slot, alpha):
        off = pl.multiple_of(t_prev * tk, tk)
        vt = vt_ref[0:V_ROWS, pl.ds(off, tk)]
        acc_sc[...] = alpha * acc_sc[...] + _dot(vt, pb[slot][...])

    def step(t, slot, m, alpha_prev):
        off = pl.multiple_of(jnp.minimum(t + 1, n - 1) * tk, tk)
        st[1 - slot][...] = _dot(k_ref[0, pl.ds(off, tk), :], qt)
        accumulate(jnp.maximum(t - 1, 0), 1 - slot, alpha_prev)
        x = st[slot][...]
        m_new = jnp.maximum(m, jnp.max(x, axis=0, keepdims=True))
        pb[slot][...] = jnp.exp2(x - m_new).astype(BF16)
        return m_new, jnp.exp2(m - m_new)

    def body(i, carry):
        m, alpha = carry
        for j in range(steps):
            m, alpha = step(i * steps + j, j % 2, m, alpha)
        return m, alpha

    init = (jnp.full((1, tq), -jnp.inf, F32), jnp.ones((1, tq), F32))
    _, alpha = lax.fori_loop(0, n // steps, body, init)
    accumulate(n - 1, (n - 1) % 2, alpha)
    acc = acc_sc[...]
    o_ref[...] = (acc[0:MLA_V] / acc[MLA_V:MLA_V + 1]).astype(o_ref.dtype)


def _attn(qt, k, vt, tq=512, tk=256, steps=32):
    nh, s, w = k.shape
    assert steps % 2 == 0 and (s // tk) % steps == 0
    return pl.pallas_call(
        functools.partial(_attn_kernel, tk=tk, steps=steps), grid=(nh, s // tq),
        in_specs=[pl.BlockSpec((w, tq), lambda h, i: (h, i)),
                  pl.BlockSpec((1, s, w), lambda h, i: (h, 0, 0)),
                  pl.BlockSpec((w, s), lambda h, i: (h, 0))],
        out_specs=pl.BlockSpec((MLA_V, tq), lambda h, i: (h, i)),
        out_shape=jax.ShapeDtypeStruct((nh * MLA_V, s), BF16),
        scratch_shapes=[pltpu.VMEM((V_ROWS, tq), F32),
                        pltpu.VMEM((tk, tq), F32), pltpu.VMEM((tk, tq), F32),
                        pltpu.VMEM((tk, tq), BF16), pltpu.VMEM((tk, tq), BF16)],
        compiler_params=_params(("parallel", "parallel")),
        name="attn",
    )(qt, k, vt)


CONV_HALO = 8


def _conv_kernel(x_ref, prev_ref, next_ref, w_ref, b_ref, q_ref, kt_ref, buf, *, tm):
    i = pl.program_id(0)
    n = pl.num_programs(0)
    buf[CONV_HALO:CONV_HALO + tm, :] = x_ref[...]
    buf[0:CONV_HALO, :] = jnp.where(i > 0, prev_ref[...], 0.0)
    buf[CONV_HALO + tm:, :] = jnp.where(i < n - 1, next_ref[...], 0.0)
    acc = jnp.broadcast_to(b_ref[...], (tm, x_ref.shape[1]))
    for j in range(CONV_W):
        start = CONV_HALO - CONV_W // 2 + j
        acc = acc + w_ref[j:j + 1, :] * buf[start:start + tm, :]
    y = acc * _sigmoid(acc)
    pad = jnp.zeros((tm, HEAD_PAD - ML_QK), F32)
    q_ref[...] = jnp.concatenate(
        [piece for hd in range(ML_HEADS) for piece in (y[:, hd * ML_QK:(hd + 1) * ML_QK], pad)],
        axis=1).astype(BF16)
    kt_ref[...] = (y[:, QK_W:] * ML_QK ** -0.5).T


def _conv(x, w, b, tm=512):
    s, c = x.shape
    r = tm // CONV_HALO
    nblk = s // CONV_HALO
    return pl.pallas_call(
        functools.partial(_conv_kernel, tm=tm), grid=(s // tm,),
        in_specs=[pl.BlockSpec((tm, c), lambda i: (i, 0)),
                  pl.BlockSpec((CONV_HALO, c), lambda i: (jnp.maximum(i * r - 1, 0), 0)),
                  pl.BlockSpec((CONV_HALO, c), lambda i: (jnp.minimum((i + 1) * r, nblk - 1), 0)),
                  _const_spec(w.shape), _const_spec(b.shape)],
        out_specs=[pl.BlockSpec((tm, QK_PAD), lambda i: (i, 0)),
                   pl.BlockSpec((QK_W, tm), lambda i: (0, i))],
        out_shape=[jax.ShapeDtypeStruct((s, QK_PAD), BF16),
                   jax.ShapeDtypeStruct((QK_W, s), F32)],
        scratch_shapes=[pltpu.VMEM((tm + 2 * CONV_HALO, c), F32)],
        compiler_params=_params(("parallel",)),
        name="conv",
    )(x, x, x, w, b)


def _mlstm_kernel(qf_ref, ktf_ref, vf_ref, gf_ref, cumf_ref, qb_ref, ktb_ref, vb_ref, gb_ref,
                  cumb_ref, hf_ref, hb_ref, c_sc, m_sc, *, chunks):
    L = ML_CHUNK

    @pl.when(pl.program_id(0) == 0)
    def _():
        c_sc[...] = jnp.zeros(c_sc.shape, F32)
        m_sc[...] = jnp.zeros(m_sc.shape, F32)

    row = lax.broadcasted_iota(jnp.int32, (L, L), 0)
    col = lax.broadcasted_iota(jnp.int32, (L, L), 1)
    mask = [col <= row, col >= row]
    ones_blk = jnp.ones((L, LANES), BF16)
    k_pad = jnp.zeros((HEAD_PAD - ML_QK, L), BF16)
    c_pad = jnp.zeros((HEAD_PAD - ML_QK, ML_V + LANES), BF16)
    refs = ((qf_ref, ktf_ref, vf_ref, gf_ref, hf_ref), (qb_ref, ktb_ref, vb_ref, gb_ref, hb_ref))
    cum_refs = (cumf_ref, cumb_ref)
    heads = [(d, hd) for d in range(N_DIR) for hd in range(ML_HEADS)]
    blk = lambda hd: slice(hd * HEAD_PAD, (hd + 1) * HEAD_PAD)
    kblk = lambda hd: slice(hd * ML_QK, (hd + 1) * ML_QK)

    for step in range(chunks):
        win = [slice(lc * L, (lc + 1) * L) for lc in (step, chunks - 1 - step)]
        gate = [refs[d][3][:, win[d]] for d in range(N_DIR)]
        cum = [cum_refs[d][:, win[d]] for d in range(N_DIR)]

        q, kt, v_aug, qk, qc = {}, {}, {}, {}, {}
        for d, hd in heads:
            j = d * ML_HEADS + hd
            q[j] = refs[d][0][win[d], blk(hd)]
            kt[j] = refs[d][1][kblk(hd), win[d]]
            v_aug[j] = jnp.concatenate([refs[d][2][win[d], blk(hd)], ones_blk], axis=1)
            qk[j] = _dot(q[j], jnp.concatenate([kt[j].astype(BF16), k_pad], axis=0))
            qc[j] = _dot(q[j], jnp.concatenate([c_sc[j].astype(BF16), c_pad], axis=0))

        m_prev = {}
        for d, hd in heads:
            j = d * ML_HEADS + hd
            br = cum[d][N_GATE + j:N_GATE + j + 1, :]
            ir = gate[d][j:j + 1, :]
            m_prev[j] = m_sc[j][0:1, 0:1]
            b_end = br[:, L - 1:L] if d == 0 else br[:, 0:1]
            g_row = b_end - br + ir
            m_new = jnp.maximum(b_end + m_prev[j], jnp.max(g_row, axis=1, keepdims=True))
            decay = jnp.exp(b_end + m_prev[j] - m_new)
            kw_t = (kt[j] * jnp.exp(g_row - m_new)).astype(BF16)
            c_sc[j] = decay * c_sc[j] + _dot(kw_t, v_aug[j])
            m_sc[j] = jnp.broadcast_to(m_new, m_sc.shape[1:])

        a, mm, mm_b, bc, sv = {}, {}, {}, {}, {}
        for d, hd in heads:
            j = d * ML_HEADS + hd
            br = cum[d][N_GATE + j:N_GATE + j + 1, :]
            lf = gate[d][N_GATE + j:N_GATE + j + 1, :]
            a[j] = jnp.where(mask[d], gate[d][j:j + 1, :] - br, -jnp.inf)
            mm[j] = jnp.maximum(m_prev[j], jnp.max(a[j], axis=1, keepdims=True))
            bc[j] = jnp.sum(jnp.where(mask[d], lf, 0.0), axis=1, keepdims=True)
        for d, hd in heads:
            j = d * ML_HEADS + hd
            mm_b[j] = jnp.broadcast_to(mm[j], (L, L))
            smat = qk[j] * jnp.exp(a[j] - mm_b[j])
            sv[j] = _dot(smat.astype(BF16), v_aug[j])
        for d, hd in heads:
            j = d * ML_HEADS + hd
            inter = jnp.exp(m_prev[j] - mm_b[j])
            nv = sv[j] + jnp.concatenate([inter, inter], axis=1) * qc[j]
            den = nv[:, ML_V:ML_V + 1]
            inv = 1.0 / jnp.maximum(jnp.abs(den), jnp.exp(-(bc[j] + mm[j])))
            refs[d][4][win[d], blk(hd)] = nv[:, :ML_V] * inv


def _mlstm(q, kt, v, gate_t, cum_t, chunks=1):
    s = q.shape[0]
    tb = chunks * ML_CHUNK
    nb = s // tb
    ng = 2 * N_GATE
    rows = lambda w, f: pl.BlockSpec((tb, w), lambda c: (f(c), 0))
    cols = lambda h, f, r=0: pl.BlockSpec((h, tb), lambda c: (r, f(c)))
    fwd = lambda c: c
    bwd = lambda c: nb - 1 - c
    hshape = jax.ShapeDtypeStruct((s, ML_OUT), F32)
    return pl.pallas_call(
        functools.partial(_mlstm_kernel, chunks=chunks), grid=(nb,),
        in_specs=[rows(QK_PAD, fwd), cols(QK_W, fwd), rows(ML_OUT, fwd),
                  cols(ng, fwd), cols(ng, fwd, 0),
                  rows(QK_PAD, bwd), cols(QK_W, bwd), rows(ML_OUT, bwd),
                  cols(ng, bwd), cols(ng, bwd, 1)],
        out_specs=[rows(ML_OUT, fwd), rows(ML_OUT, bwd)],
        out_shape=[hshape, hshape],
        scratch_shapes=[pltpu.VMEM((N_GATE, ML_QK, ML_V + LANES), F32),
                        pltpu.VMEM((N_GATE, 8, LANES), F32)],
        compiler_params=_params(("arbitrary",)),
        name="mlstm",
    )(q, kt, v, gate_t, cum_t, q, kt, v, gate_t, cum_t)


def _merge_kernel(h_ref, att_ref, hf_ref, hb_ref, mlo_ref, gates_ref, hn_ref,
                  wba_ref, wbm_ref, wout_ref, post_ref, o_ref):
    a = lax.dot_general(att_ref[...], wba_ref[...], (((0,), (0,)), ((), ())),
                        preferred_element_type=F32)
    hm = hf_ref[...] + hb_ref[...]
    heads = []
    for hd in range(ML_HEADS):
        blk = slice(hd * ML_V, (hd + 1) * ML_V)
        heads.append(_rms(hm[:, blk], hn_ref[:, blk]))
    hm = jnp.concatenate(heads, axis=1) * _sigmoid(mlo_ref[...])
    bm = _dot(hm.astype(BF16), wbm_ref[...])
    g = _sigmoid(gates_ref[...])
    mixed = _dot((g[:, :D_MODEL] * a + g[:, D_MODEL:] * bm).astype(BF16), wout_ref[...])
    o_ref[...] = h_ref[...] + _rms(mixed, post_ref[...])


def _merge(h, att, hf, hb, mlo, gates, hn, wba, wbm, wout, post, tm=512):
    s, d = h.shape
    row = lambda w: pl.BlockSpec((tm, w), lambda i: (i, 0))
    return pl.pallas_call(
        _merge_kernel, grid=(s // tm,),
        in_specs=[row(d), pl.BlockSpec((att.shape[0], tm), lambda i: (0, i)),
                  row(ML_OUT), row(ML_OUT), row(ML_OUT),
                  row(2 * d), _const_spec(hn.shape), _const_spec(wba.shape),
                  _const_spec(wbm.shape), _const_spec(wout.shape), _const_spec(post.shape)],
        out_specs=row(d),
        out_shape=jax.ShapeDtypeStruct((s, d), F32),
        compiler_params=_params(("parallel",)),
        name="merge",
    )(h, att, hf, hb, mlo, gates, hn, wba, wbm, wout, post)


def _pad_heads(w, nh, width):
    r = w.shape[0]
    w = w.reshape(r, nh, width)
    return jnp.pad(w, ((0, 0), (0, 0), (0, HEAD_PAD - width))).reshape(r, nh * HEAD_PAD)


def _permute_w_in(w_in):
    d = w_in.shape[0]
    small = jnp.zeros((LANES, d), w_in.dtype)
    small = small.at[0:N_GATE].set(w_in[:, _O_MLI:_O_MLF].T)
    small = small.at[N_GATE:2 * N_GATE].set(w_in[:, _O_MLF:_O_MLO].T)
    small = small.at[ROPE_LO:ROPE_LO + MLA_ROPE].set(w_in[:, _O_KR:_O_MLQ].T)
    main = jnp.concatenate([w_in[:, _O_CQ:_O_KR], w_in[:, _O_MLQ:_O_MLI], w_in[:, _O_MLO:_O_END]],
                           axis=1)
    return main, small


def kernel(x, p, positions, ffn1_pre_norm, ffn1_post_norm, ffn1_w_gate, ffn1_w_up, ffn1_w_down, mix_pre_norm, mix_post_norm, w_in, mla_q_norm, mla_kv_norm, mla_w_uq, mla_w_uk, mla_w_uv, ml_conv_w, ml_conv_b, ml_i_bias, ml_f_bias, ml_head_norm, w_branch_mla, w_branch_ml, w_out, ffn2_pre_norm, ffn2_post_norm, ffn2_w_gate, ffn2_w_up, ffn2_w_down, ple_pre_norm, ple_post_norm, ple_w_proj, ple_w_gate):
    depth, batch = p.shape[0], x.shape[0]
    assert depth == 1 and batch == 1, "kernel is specialised to DEPTH == 1, BATCH == 1"
    bf = lambda w: w.astype(BF16)
    h = x[0]
    pos = positions

    inv_freq = ROPE_THETA ** (-jnp.arange(0, MLA_ROPE, 2, dtype=F32) / MLA_ROPE)
    invf = inv_freq[:, None]

    i = 0
    h = _ffn(h, ffn1_pre_norm[i][None], ffn1_post_norm[i][None],
             bf(ffn1_w_gate[i]), bf(ffn1_w_up[i]), bf(ffn1_w_down[i]))

    vone = jnp.zeros((MLA_HEADS, HEAD_PAD), F32).at[:, MLA_V].set(1.0).reshape(-1, 1)
    win_main, win_small_t = _permute_w_in(w_in[i])
    gate_bias = jnp.concatenate([ml_i_bias[i].reshape(-1), ml_f_bias[i].reshape(-1)])[:, None]
    qt, k, vt, mlqk, mlv, gate_t, cum_t, mlo, gates = _proj(
        h, pos, invf, mix_pre_norm[i][None], bf(win_main), bf(win_small_t),
        gate_bias, mla_q_norm[i][None], mla_kv_norm[i][None],
        bf(_pad_heads(mla_w_uq[i], MLA_HEADS, MLA_NOPE + MLA_ROPE)).T,
        bf(_pad_heads(mla_w_uk[i], MLA_HEADS, MLA_NOPE)),
        bf(_pad_heads(mla_w_uv[i], MLA_HEADS, MLA_V)).T, vone)

    att = _attn(qt, k, vt)

    conv_w = jnp.pad(ml_conv_w[i], ((0, 8 - CONV_W), (0, 0)))
    conv_b = ml_conv_b[i][None]
    q_conv, kt_conv = _conv(mlqk, conv_w, conv_b)

    hf, hb = _mlstm(q_conv, kt_conv, mlv, gate_t, cum_t)

    h = _merge(h, att, hf, hb, mlo, gates, ml_head_norm[i][None], bf(w_branch_mla[i]),
               bf(w_branch_ml[i]), bf(w_out[i]), mix_post_norm[i][None])

    h = _ffn(h, ffn2_pre_norm[i][None], ffn2_post_norm[i][None],
             bf(ffn2_w_gate[i]), bf(ffn2_w_up[i]), bf(ffn2_w_down[i]),
             ple=(p[i, 0], ple_pre_norm[i][None], ple_post_norm[i][None],
                  bf(ple_w_proj[i]), bf(ple_w_gate[i])))
    return h[None]
```

```python
import functools
import math

import jax
import jax.numpy as jnp
from jax import lax
from jax.experimental import pallas as pl
from jax.experimental.pallas import tpu as pltpu

D_MODEL = 1024
SEQ = 16384
PLE_DIM = 256
D_FF = 2816
EPS = 1e-6
MLA_HEADS = 8
MLA_Q_RANK = 256
MLA_KV_RANK = 256
MLA_NOPE = 64
MLA_ROPE = 32
MLA_V = 64
ROPE_THETA = 10000.0
ML_HEADS = 4
ML_QK = 64
ML_V = 128
ML_CHUNK = 128
CONV_W = 5
N_DIR = 2
MLA_OUT = MLA_HEADS * MLA_V
ML_OUT = ML_HEADS * ML_V

LANES = 128
HEAD_PAD = LANES
ROPE_LO = MLA_NOPE
ROPE_HALF = MLA_ROPE // 2
VMEM_LIMIT = 56 * 1024 * 1024

F32 = jnp.float32
BF16 = jnp.bfloat16
LOG2E = math.log2(math.e)
Q_SCALE = (MLA_NOPE + MLA_ROPE) ** -0.5 * LOG2E

_O_CQ = 0
_O_CKV = _O_CQ + MLA_Q_RANK
_O_KR = _O_CKV + MLA_KV_RANK
_O_MLQ = _O_KR + MLA_ROPE
_O_MLK = _O_MLQ + ML_HEADS * ML_QK
_O_MLV = _O_MLK + ML_HEADS * ML_QK
_O_MLI = _O_MLV + ML_OUT
_O_MLF = _O_MLI + N_DIR * ML_HEADS
_O_MLO = _O_MLF + N_DIR * ML_HEADS
_O_GATES = _O_MLO + ML_OUT
_O_END = _O_GATES + 2 * D_MODEL

N_GATE = N_DIR * ML_HEADS
QK_W = ML_HEADS * ML_QK
QK_PAD = ML_HEADS * HEAD_PAD

_P_CQ = 0
_P_CKV = _P_CQ + MLA_Q_RANK
_P_MLQK = _P_CKV + MLA_KV_RANK
_P_MLV = _P_MLQK + 2 * QK_W
_P_MLO = _P_MLV + ML_OUT
_P_GATES = _P_MLO + ML_OUT
_P_END = _P_GATES + 2 * D_MODEL


def _rms(x, w):
    ms = jnp.mean(x * x, axis=-1, keepdims=True)
    return x * lax.rsqrt(ms + EPS) * w


def _sigmoid(x):
    return 1.0 / (1.0 + jnp.exp(-x))


def _dot(a, b):
    return jnp.dot(a, b, preferred_element_type=F32)


def _const_spec(shape):
    nd = len(shape)
    return pl.BlockSpec(shape, lambda *_: (0,) * nd, pipeline_mode=pl.Buffered(1))


def _params(sem):
    return pltpu.CompilerParams(dimension_semantics=sem, vmem_limit_bytes=VMEM_LIMIT)


def _ffn_body(x, pre, post, wg, wu, wd):
    u = _rms(x, pre).astype(BF16)
    g = _dot(u, wg)
    up = _dot(u, wu)
    hid = (g * _sigmoid(g) * up).astype(BF16)
    y = _dot(hid, wd)
    return x + 0.5 * _rms(y, post)


def _ffn_kernel(x_ref, pre_ref, post_ref, wg_ref, wu_ref, wd_ref, o_ref):
    o_ref[...] = _ffn_body(x_ref[...], pre_ref[...], post_ref[...],
                           wg_ref[...], wu_ref[...], wd_ref[...])


def _ffn_ple_kernel(x_ref, pre_ref, post_ref, wg_ref, wu_ref, wd_ref,
                    p_ref, ppre_ref, ppost_ref, wproj_ref, wgate_ref, o_ref):
    h = _ffn_body(x_ref[...], pre_ref[...], post_ref[...],
                  wg_ref[...], wu_ref[...], wd_ref[...])
    e = _dot(p_ref[...].astype(BF16), wproj_ref[...])
    g = _sigmoid(_dot(_rms(h, ppre_ref[...]).astype(BF16), wgate_ref[...]))
    o_ref[...] = h + _rms(g * e, ppost_ref[...])


def _ffn(x, pre, post, wg, wu, wd, ple=None, tm=512):
    s, d = x.shape
    row = pl.BlockSpec((tm, d), lambda i: (i, 0))
    in_specs = [row, _const_spec((1, d)), _const_spec((1, d)),
                _const_spec(wg.shape), _const_spec(wu.shape), _const_spec(wd.shape)]
    args = [x, pre, post, wg, wu, wd]
    kern = _ffn_kernel
    if ple is not None:
        p, ppre, ppost, wproj, wgate = ple
        in_specs += [pl.BlockSpec((tm, p.shape[1]), lambda i: (i, 0)),
                     _const_spec((1, d)), _const_spec((1, d)),
                     _const_spec(wproj.shape), _const_spec(wgate.shape)]
        args += [p, ppre, ppost, wproj, wgate]
        kern = _ffn_ple_kernel
    return pl.pallas_call(
        kern, grid=(s // tm,), in_specs=in_specs, out_specs=row,
        out_shape=jax.ShapeDtypeStruct((s, d), F32),
        compiler_params=_params(("parallel",)),
        name="ffn" if ple is None else "ffn_ple",
    )(*args)


def _dot_nt(a, b):
    return lax.dot_general(a, b, (((1,), (1,)), ((), ())), preferred_element_type=F32)


def _rope_tables_t(pos_ref, invf_ref):
    ang = pos_ref[...].astype(F32) * invf_ref[...]
    return jnp.cos(ang), jnp.sin(ang)


def _rope_t(x, c, s):
    x1, x2 = x[:ROPE_HALF], x[ROPE_HALF:]
    return x1 * c - x2 * s, x2 * c + x1 * s


def _proj_kernel(h_ref, pos_ref, invf_ref, pre_ref, win_ref, wsmt_ref, gbias_ref, qn_ref, kvn_ref,
                 wuqt_ref, wuk_ref, wuvt_ref, vone_ref,
                 qt_ref, k_ref, vt_ref, mlqk_ref, mlv_ref, gt_ref, cum_ref, mlo_ref, gates_ref):
    u = _rms(h_ref[...], pre_ref[...]).astype(BF16)
    zdot = lambda a, b: _dot(u, win_ref[:, a:b])
    c, s = _rope_tables_t(pos_ref, invf_ref)
    rope = slice(ROPE_LO, ROPE_LO + MLA_ROPE)
    small_t = _dot_nt(wsmt_ref[...], u)
    z_c = zdot(_P_CQ, _P_MLQK)
    gates_ref[...] = _sigmoid(zdot(_P_GATES, _P_END)).astype(BF16)

    pre = small_t[0:2 * N_GATE, :] + gbias_ref[...]
    is_f = lax.broadcasted_iota(jnp.int32, (2 * N_GATE, 1), 0) >= N_GATE
    gate = jnp.where(is_f, jax.nn.log_sigmoid(pre), pre)
    gt_ref[...] = gate
    r = lax.broadcasted_iota(jnp.int32, (ML_CHUNK, ML_CHUNK), 0)
    cidx = lax.broadcasted_iota(jnp.int32, (ML_CHUNK, ML_CHUNK), 1)
    for d, tri in enumerate((r <= cidx, r >= cidx)):
        for ch in range(gate.shape[1] // ML_CHUNK):
            cs = slice(ch * ML_CHUNK, (ch + 1) * ML_CHUNK)
            cum_ref[d * 2 * N_GATE:(d + 1) * 2 * N_GATE, cs] = jnp.dot(
                gate[:, cs], tri.astype(F32), preferred_element_type=F32,
                precision=lax.Precision.HIGHEST)

    tm = small_t.shape[1]
    k_rope = jnp.concatenate(
        [jnp.zeros((ROPE_LO, tm), F32), *_rope_t(small_t[rope, :], c, s),
         jnp.zeros((HEAD_PAD - ROPE_LO - MLA_ROPE, tm), F32)], axis=0).T

    cq = _rms(z_c[:, :MLA_Q_RANK], qn_ref[...]).astype(BF16)
    ckv = _rms(z_c[:, MLA_Q_RANK:], kvn_ref[...]).astype(BF16)
    qpt = _dot_nt(wuqt_ref[...], cq)
    kp = _dot(ckv, wuk_ref[...])
    vt_ref[...] = (_dot_nt(wuvt_ref[...], ckv) + vone_ref[...]).astype(BF16)
    mlqk_ref[...] = zdot(_P_MLQK, _P_MLV)
    mlv_ref[...] = zdot(_P_MLV, _P_MLO).astype(BF16)
    mlo_ref[...] = _sigmoid(zdot(_P_MLO, _P_GATES)).astype(BF16)
    for hd in range(MLA_HEADS):
        blk = slice(hd * HEAD_PAD, (hd + 1) * HEAD_PAD)
        x = qpt[blk, :]
        r1, r2 = _rope_t(x[rope, :], c, s)
        qt_ref[blk, :] = jnp.concatenate(
            [x[:ROPE_LO] * Q_SCALE, r1 * Q_SCALE, r2 * Q_SCALE, x[ROPE_LO + MLA_ROPE:]],
            axis=0).astype(BF16)
        k_ref[hd] = (kp[:, blk] + k_rope).astype(BF16)


def _proj(h, pos, invf, pre, win, wsmt, gbias, qn, kvn, wuqt, wuk, wuvt, vone, tm=512):
    s, d = h.shape
    ng = 2 * N_GATE
    row = lambda w: pl.BlockSpec((tm, w), lambda i: (i, 0))
    col = lambda r: pl.BlockSpec((r, tm), lambda i: (0, i))
    cshape = jax.ShapeDtypeStruct((MLA_HEADS * HEAD_PAD, s), BF16)
    return pl.pallas_call(
        _proj_kernel, grid=(s // tm,),
        in_specs=[row(d), col(1), _const_spec(invf.shape), _const_spec((1, d)),
                  _const_spec(win.shape), _const_spec(wsmt.shape), _const_spec(gbias.shape),
                  _const_spec(qn.shape), _const_spec(kvn.shape),
                  _const_spec(wuqt.shape), _const_spec(wuk.shape), _const_spec(wuvt.shape),
                  _const_spec(vone.shape)],
        out_specs=[col(MLA_HEADS * HEAD_PAD),
                   pl.BlockSpec((MLA_HEADS, tm, HEAD_PAD), lambda i: (0, i, 0)),
                   col(MLA_HEADS * HEAD_PAD),
                   row(2 * QK_W), row(ML_OUT), col(ng), col(N_DIR * ng),
                   row(ML_OUT), row(2 * D_MODEL)],
        out_shape=[cshape, jax.ShapeDtypeStruct((MLA_HEADS, s, HEAD_PAD), BF16), cshape,
                   jax.ShapeDtypeStruct((s, 2 * QK_W), F32),
                   jax.ShapeDtypeStruct((s, ML_OUT), BF16),
                   jax.ShapeDtypeStruct((ng, s), F32),
                   jax.ShapeDtypeStruct((N_DIR * ng, s), F32),
                   jax.ShapeDtypeStruct((s, ML_OUT), BF16),
                   jax.ShapeDtypeStruct((s, 2 * D_MODEL), BF16)],
        compiler_params=_params(("parallel",)),
        name="proj",
    )(h, pos, invf, pre, win, wsmt, gbias, qn, kvn, wuqt, wuk, wuvt, vone)


BF16_ROWS = 16
V_ROWS = -(-(MLA_V + 1) // BF16_ROWS) * BF16_ROWS


def _attn_kernel(qt_ref, k_ref, vt_ref, o_ref, acc_sc, st0, st1, p0, p1, *, tk, steps):
    qt = qt_ref[...]
    tq = qt.shape[1]
    n = k_ref.shape[1] // tk
    st, pb = (st0, st1), (p0, p1)

    acc_sc[...] = jnp.zeros(acc_sc.shape, F32)
    p1[...] = jnp.zeros(p1.shape, BF16)
    st0[...] = _dot(k_ref[0, 0:tk, :], qt)

    def accumulate(t_prev, slot, alpha):
        off = pl.multiple_of(t_prev * tk, tk)
        vt = vt_ref[0:V_ROWS, pl.ds(off, tk)]
        acc_sc[...] = alpha * acc_sc[...] + _dot(vt, pb[slot][...])

    def step(t, slot, m, alpha_prev):
        off = pl.multiple_of(jnp.minimum(t + 1, n - 1) * tk, tk)
        st[1 - slot][...] = _dot(k_ref[0, pl.ds(off, tk), :], qt)
        accumulate(jnp.maximum(t - 1, 0), 1 - slot, alpha_prev)
        x = st[slot][...]
        m_new = jnp.maximum(m, jnp.max(x, axis=0, keepdims=True))
        pb[slot][...] = jnp.exp2(x - m_new).astype(BF16)
        return m_new, jnp.exp2(m - m_new)

    def body(i, carry):
        m, alpha = carry
        for j in range(steps):
            m, alpha = step(i * steps + j, j % 2, m, alpha)
        return m, alpha

    init = (jnp.full((1, tq), -jnp.inf, F32), jnp.ones((1, tq), F32))
    _, alpha = lax.fori_loop(0, n // steps, body, init)
    accumulate(n - 1, (n - 1) % 2, alpha)
    acc = acc_sc[...]
    o_ref[...] = (acc[0:MLA_V] / acc[MLA_V:MLA_V + 1]).astype(o_ref.dtype)


def _attn(qt, k, vt, tq=512, tk=256, steps=32):
    nh, s, w = k.shape
    assert steps % 2 == 0 and (s // tk) % steps == 0
    return pl.pallas_call(
        functools.partial(_attn_kernel, tk=tk, steps=steps), grid=(nh, s // tq),
        in_specs=[pl.BlockSpec((w, tq), lambda h, i: (h, i)),
                  pl.BlockSpec((1, s, w), lambda h, i: (h, 0, 0)),
                  pl.BlockSpec((w, s), lambda h, i: (h, 0))],
        out_specs=pl.BlockSpec((MLA_V, tq), lambda h, i: (h, i)),
        out_shape=jax.ShapeDtypeStruct((nh * MLA_V, s), BF16),
        scratch_shapes=[pltpu.VMEM((V_ROWS, tq), F32),
                        pltpu.VMEM((tk, tq), F32), pltpu.VMEM((tk, tq), F32),
                        pltpu.VMEM((tk, tq), BF16), pltpu.VMEM((tk, tq), BF16)],
        compiler_params=_params(("parallel", "parallel")),
        name="attn",
    )(qt, k, vt)


CONV_HALO = 8


def _conv_kernel(x_ref, prev_ref, next_ref, w_ref, b_ref, q_ref, kt_ref, buf, *, tm):
    i = pl.program_id(0)
    n = pl.num_programs(0)
    buf[CONV_HALO:CONV_HALO + tm, :] = x_ref[...]
    buf[0:CONV_HALO, :] = jnp.where(i > 0, prev_ref[...], 0.0)
    buf[CONV_HALO + tm:, :] = jnp.where(i < n - 1, next_ref[...], 0.0)
    acc = jnp.broadcast_to(b_ref[...], (tm, x_ref.shape[1]))
    for j in range(CONV_W):
        start = CONV_HALO - CONV_W // 2 + j
        acc = acc + w_ref[j:j + 1, :] * buf[start:start + tm, :]
    y = acc * _sigmoid(acc)
    pad = jnp.zeros((tm, HEAD_PAD - ML_QK), F32)
    q_ref[...] = jnp.concatenate(
        [piece for hd in range(ML_HEADS) for piece in (y[:, hd * ML_QK:(hd + 1) * ML_QK], pad)],
        axis=1).astype(BF16)
    kt_ref[...] = (y[:, QK_W:] * ML_QK ** -0.5).T


def _conv(x, w, b, tm=512):
    s, c = x.shape
    r = tm // CONV_HALO
    nblk = s // CONV_HALO
    return pl.pallas_call(
        functools.partial(_conv_kernel, tm=tm), grid=(s // tm,),
        in_specs=[pl.BlockSpec((tm, c), lambda i: (i, 0)),
                  pl.BlockSpec((CONV_HALO, c), lambda i: (jnp.maximum(i * r - 1, 0), 0)),
                  pl.BlockSpec((CONV_HALO, c), lambda i: (jnp.minimum((i + 1) * r, nblk - 1), 0)),
                  _const_spec(w.shape), _const_spec(b.shape)],
        out_specs=[pl.BlockSpec((tm, QK_PAD), lambda i: (i, 0)),
                   pl.BlockSpec((QK_W, tm), lambda i: (0, i))],
        out_shape=[jax.ShapeDtypeStruct((s, QK_PAD), BF16),
                   jax.ShapeDtypeStruct((QK_W, s), F32)],
        scratch_shapes=[pltpu.VMEM((tm + 2 * CONV_HALO, c), F32)],
        compiler_params=_params(("parallel",)),
        name="conv",
    )(x, x, x, w, b)


def _mlstm_kernel(qf_ref, ktf_ref, vf_ref, gf_ref, cumf_ref, qb_ref, ktb_ref, vb_ref, gb_ref,
                  cumb_ref, hf_ref, hb_ref, c_sc, m_sc, *, chunks):
    L = ML_CHUNK

    @pl.when(pl.program_id(0) == 0)
    def _():
        c_sc[...] = jnp.zeros(c_sc.shape, F32)
        m_sc[...] = jnp.zeros(m_sc.shape, F32)

    row = lax.broadcasted_iota(jnp.int32, (L, L), 0)
    col = lax.broadcasted_iota(jnp.int32, (L, L), 1)
    mask = [col <= row, col >= row]
    ones_blk = jnp.ones((L, LANES), BF16)
    k_pad = jnp.zeros((HEAD_PAD - ML_QK, L), BF16)
    c_pad = jnp.zeros((HEAD_PAD - ML_QK, ML_V + LANES), BF16)
    refs = ((qf_ref, ktf_ref, vf_ref, gf_ref, hf_ref), (qb_ref, ktb_ref, vb_ref, gb_ref, hb_ref))
    cum_refs = (cumf_ref, cumb_ref)
    heads = [(d, hd) for d in range(N_DIR) for hd in range(ML_HEADS)]
    blk = lambda hd: slice(hd * HEAD_PAD, (hd + 1) * HEAD_PAD)
    kblk = lambda hd: slice(hd * ML_QK, (hd + 1) * ML_QK)

    for step in range(chunks):
        win = [slice(lc * L, (lc + 1) * L) for lc in (step, chunks - 1 - step)]
        gate = [refs[d][3][:, win[d]] for d in range(N_DIR)]
        cum = [cum_refs[d][:, win[d]] for d in range(N_DIR)]

        q, kt, v_aug, qk, qc = {}, {}, {}, {}, {}
        for d, hd in heads:
            j = d * ML_HEADS + hd
            q[j] = refs[d][0][win[d], blk(hd)]
            kt[j] = refs[d][1][kblk(hd), win[d]]
            v_aug[j] = jnp.concatenate([refs[d][2][win[d], blk(hd)], ones_blk], axis=1)
            qk[j] = _dot(q[j], jnp.concatenate([kt[j].astype(BF16), k_pad], axis=0))
            qc[j] = _dot(q[j], jnp.concatenate([c_sc[j].astype(BF16), c_pad], axis=0))

        m_prev = {}
        for d, hd in heads:
            j = d * ML_HEADS + hd
            br = cum[d][N_GATE + j:N_GATE + j + 1, :]
            ir = gate[d][j:j + 1, :]
            m_prev[j] = m_sc[j][0:1, 0:1]
            b_end = br[:, L - 1:L] if d == 0 else br[:, 0:1]
            g_row = b_end - br + ir
            m_new = jnp.maximum(b_end + m_prev[j], jnp.max(g_row, axis=1, keepdims=True))
            decay = jnp.exp(b_end + m_prev[j] - m_new)
            kw_t = (kt[j] * jnp.exp(g_row - m_new)).astype(BF16)
            c_sc[j] = decay * c_sc[j] + _dot(kw_t, v_aug[j])
            m_sc[j] = jnp.broadcast_to(m_new, m_sc.shape[1:])

        a, mm, mm_b, bc, sv = {}, {}, {}, {}, {}
        for d, hd in heads:
            j = d * ML_HEADS + hd
            br = cum[d][N_GATE + j:N_GATE + j + 1, :]
            lf = gate[d][N_GATE + j:N_GATE + j + 1, :]
            a[j] = jnp.where(mask[d], gate[d][j:j + 1, :] - br, -jnp.inf)
            mm[j] = jnp.maximum(m_prev[j], jnp.max(a[j], axis=1, keepdims=True))
            bc[j] = jnp.sum(jnp.where(mask[d], lf, 0.0), axis=1, keepdims=True)
        for d, hd in heads:
            j = d * ML_HEADS + hd
            mm_b[j] = jnp.broadcast_to(mm[j], (L, L))
            smat = qk[j] * jnp.exp(a[j] - mm_b[j])
            sv[j] = _dot(smat.astype(BF16), v_aug[j])
        for d, hd in heads:
            j = d * ML_HEADS + hd
            inter = jnp.exp(m_prev[j] - mm_b[j])
            nv = sv[j] + jnp.concatenate([inter, inter], axis=1) * qc[j]
            den = nv[:, ML_V:ML_V + 1]
            inv = 1.0 / jnp.maximum(jnp.abs(den), jnp.exp(-(bc[j] + mm[j])))
            refs[d][4][win[d], blk(hd)] = (nv[:, :ML_V] * inv).astype(BF16)


def _mlstm(q, kt, v, gate_t, cum_t, chunks=1):
    s = q.shape[0]
    tb = chunks * ML_CHUNK
    nb = s // tb
    ng = 2 * N_GATE
    rows = lambda w, f: pl.BlockSpec((tb, w), lambda c: (f(c), 0))
    cols = lambda h, f, r=0: pl.BlockSpec((h, tb), lambda c: (r, f(c)))
    fwd = lambda c: c
    bwd = lambda c: nb - 1 - c
    hshape = jax.ShapeDtypeStruct((s, ML_OUT), BF16)
    return pl.pallas_call(
        functools.partial(_mlstm_kernel, chunks=chunks), grid=(nb,),
        in_specs=[rows(QK_PAD, fwd), cols(QK_W, fwd), rows(ML_OUT, fwd),
                  cols(ng, fwd), cols(ng, fwd, 0),
                  rows(QK_PAD, bwd), cols(QK_W, bwd), rows(ML_OUT, bwd),
                  cols(ng, bwd), cols(ng, bwd, 1)],
        out_specs=[rows(ML_OUT, fwd), rows(ML_OUT, bwd)],
        out_shape=[hshape, hshape],
        scratch_shapes=[pltpu.VMEM((N_GATE, ML_QK, ML_V + LANES), F32),
                        pltpu.VMEM((N_GATE, 8, LANES), F32)],
        compiler_params=_params(("arbitrary",)),
        name="mlstm",
    )(q, kt, v, gate_t, cum_t, q, kt, v, gate_t, cum_t)


def _merge_kernel(h_ref, att_ref, hf_ref, hb_ref, mlo_ref, gates_ref, hn_ref,
                  wba_ref, wbm_ref, wout_ref, post_ref, o_ref):
    a = lax.dot_general(att_ref[...], wba_ref[...], (((0,), (0,)), ((), ())),
                        preferred_element_type=F32)
    hm = hf_ref[...].astype(F32) + hb_ref[...].astype(F32)
    heads = []
    for hd in range(ML_HEADS):
        blk = slice(hd * ML_V, (hd + 1) * ML_V)
        heads.append(_rms(hm[:, blk], hn_ref[:, blk]))
    hm = jnp.concatenate(heads, axis=1) * mlo_ref[...]
    bm = _dot(hm.astype(BF16), wbm_ref[...])
    g = gates_ref[...]
    mixed = _dot((g[:, :D_MODEL] * a + g[:, D_MODEL:] * bm).astype(BF16), wout_ref[...])
    o_ref[...] = h_ref[...] + _rms(mixed, post_ref[...])


def _merge(h, att, hf, hb, mlo, gates, hn, wba, wbm, wout, post, tm=512):
    s, d = h.shape
    row = lambda w: pl.BlockSpec((tm, w), lambda i: (i, 0))
    return pl.pallas_call(
        _merge_kernel, grid=(s // tm,),
        in_specs=[row(d), pl.BlockSpec((att.shape[0], tm), lambda i: (0, i)),
                  row(ML_OUT), row(ML_OUT), row(ML_OUT),
                  row(2 * d), _const_spec(hn.shape), _const_spec(wba.shape),
                  _const_spec(wbm.shape), _const_spec(wout.shape), _const_spec(post.shape)],
        out_specs=row(d),
        out_shape=jax.ShapeDtypeStruct((s, d), F32),
        compiler_params=_params(("parallel",)),
        name="merge",
    )(h, att, hf, hb, mlo, gates, hn, wba, wbm, wout, post)


def _pad_heads(w, nh, width):
    r = w.shape[0]
    w = w.reshape(r, nh, width)
    return jnp.pad(w, ((0, 0), (0, 0), (0, HEAD_PAD - width))).reshape(r, nh * HEAD_PAD)


def _permute_w_in(w_in):
    d = w_in.shape[0]
    small = jnp.zeros((LANES, d), w_in.dtype)
    small = small.at[0:N_GATE].set(w_in[:, _O_MLI:_O_MLF].T)
    small = small.at[N_GATE:2 * N_GATE].set(w_in[:, _O_MLF:_O_MLO].T)
    small = small.at[ROPE_LO:ROPE_LO + MLA_ROPE].set(w_in[:, _O_KR:_O_MLQ].T)
    main = jnp.concatenate([w_in[:, _O_CQ:_O_KR].astype(BF16), w_in[:, _O_MLQ:_O_MLI].astype(BF16),
                            w_in[:, _O_MLO:_O_END].astype(BF16)], axis=1)
    return main, small.astype(BF16)


def kernel(x, p, positions, ffn1_pre_norm, ffn1_post_norm, ffn1_w_gate, ffn1_w_up, ffn1_w_down, mix_pre_norm, mix_post_norm, w_in, mla_q_norm, mla_kv_norm, mla_w_uq, mla_w_uk, mla_w_uv, ml_conv_w, ml_conv_b, ml_i_bias, ml_f_bias, ml_head_norm, w_branch_mla, w_branch_ml, w_out, ffn2_pre_norm, ffn2_post_norm, ffn2_w_gate, ffn2_w_up, ffn2_w_down, ple_pre_norm, ple_post_norm, ple_w_proj, ple_w_gate):
    depth, batch = p.shape[0], x.shape[0]
    assert depth == 1 and batch == 1, "kernel is specialised to DEPTH == 1, BATCH == 1"
    bf = lambda w: w.astype(BF16)
    h = x[0]
    pos = positions

    inv_freq = ROPE_THETA ** (-jnp.arange(0, MLA_ROPE, 2, dtype=F32) / MLA_ROPE)
    invf = inv_freq[:, None]

    i = 0
    h = _ffn(h, ffn1_pre_norm[i][None], ffn1_post_norm[i][None],
             bf(ffn1_w_gate[i]), bf(ffn1_w_up[i]), bf(ffn1_w_down[i]))

    vone = jnp.zeros((MLA_HEADS, HEAD_PAD), F32).at[:, MLA_V].set(1.0).reshape(-1, 1)
    win_main, win_small_t = _permute_w_in(w_in[i])
    gate_bias = jnp.concatenate([ml_i_bias[i].reshape(-1), ml_f_bias[i].reshape(-1)])[:, None]
    qt, k, vt, mlqk, mlv, gate_t, cum_t, mlo, gates = _proj(
        h, pos, invf, mix_pre_norm[i][None], win_main, win_small_t,
        gate_bias, mla_q_norm[i][None], mla_kv_norm[i][None],
        bf(_pad_heads(mla_w_uq[i], MLA_HEADS, MLA_NOPE + MLA_ROPE)).T,
        bf(_pad_heads(mla_w_uk[i], MLA_HEADS, MLA_NOPE)),
        bf(_pad_heads(mla_w_uv[i], MLA_HEADS, MLA_V)).T, vone)

    att = _attn(qt, k, vt)

    conv_w = jnp.pad(ml_conv_w[i], ((0, 8 - CONV_W), (0, 0)))
    conv_b = ml_conv_b[i][None]
    q_conv, kt_conv = _conv(mlqk, conv_w, conv_b)

    hf, hb = _mlstm(q_conv, kt_conv, mlv, gate_t, cum_t)

    h = _merge(h, att, hf, hb, mlo, gates, ml_head_norm[i][None], bf(w_branch_mla[i]),
               bf(w_branch_ml[i]), bf(w_out[i]), mix_post_norm[i][None])

    h = _ffn(h, ffn2_pre_norm[i][None], ffn2_post_norm[i][None],
             bf(ffn2_w_gate[i]), bf(ffn2_w_up[i]), bf(ffn2_w_down[i]),
             ple=(p[i, 0], ple_pre_norm[i][None], ple_post_norm[i][None],
                  bf(ple_w_proj[i]), bf(ple_w_gate[i])))
    return h[None]
```

```python
import functools
import math

import jax
import jax.numpy as jnp
from jax import lax
from jax.experimental import pallas as pl
from jax.experimental.pallas import tpu as pltpu

D_MODEL = 1024
SEQ = 16384
PLE_DIM = 256
D_FF = 2816
EPS = 1e-6
MLA_HEADS = 8
MLA_Q_RANK = 256
MLA_KV_RANK = 256
MLA_NOPE = 64
MLA_ROPE = 32
MLA_V = 64
ROPE_THETA = 10000.0
ML_HEADS = 4
ML_QK = 64
ML_V = 128
ML_CHUNK = 128
CONV_W = 5
N_DIR = 2
MLA_OUT = MLA_HEADS * MLA_V
ML_OUT = ML_HEADS * ML_V

LANES = 128
HEAD_PAD = LANES
ROPE_LO = MLA_NOPE
ROPE_HALF = MLA_ROPE // 2
VMEM_LIMIT = 56 * 1024 * 1024

F32 = jnp.float32
BF16 = jnp.bfloat16
LOG2E = math.log2(math.e)
Q_SCALE = (MLA_NOPE + MLA_ROPE) ** -0.5 * LOG2E

_O_CQ = 0
_O_CKV = _O_CQ + MLA_Q_RANK
_O_KR = _O_CKV + MLA_KV_RANK
_O_MLQ = _O_KR + MLA_ROPE
_O_MLK = _O_MLQ + ML_HEADS * ML_QK
_O_MLV = _O_MLK + ML_HEADS * ML_QK
_O_MLI = _O_MLV + ML_OUT
_O_MLF = _O_MLI + N_DIR * ML_HEADS
_O_MLO = _O_MLF + N_DIR * ML_HEADS
_O_GATES = _O_MLO + ML_OUT
_O_END = _O_GATES + 2 * D_MODEL

N_GATE = N_DIR * ML_HEADS
QK_W = ML_HEADS * ML_QK
QK_PAD = ML_HEADS * HEAD_PAD

_P_CQ = 0
_P_CKV = _P_CQ + MLA_Q_RANK
_P_MLQK = _P_CKV + MLA_KV_RANK
_P_MLV = _P_MLQK + 2 * QK_W
_P_MLO = _P_MLV + ML_OUT
_P_GATES = _P_MLO + ML_OUT
_P_END = _P_GATES + 2 * D_MODEL


def _rms(x, w):
    ms = jnp.mean(x * x, axis=-1, keepdims=True)
    return x * lax.rsqrt(ms + EPS) * w


def _sigmoid(x):
    return 1.0 / (1.0 + jnp.exp(-x))


def _dot(a, b):
    return jnp.dot(a, b, preferred_element_type=F32)


def _const_spec(shape):
    nd = len(shape)
    return pl.BlockSpec(shape, lambda *_: (0,) * nd, pipeline_mode=pl.Buffered(1))


def _params(sem):
    return pltpu.CompilerParams(dimension_semantics=sem, vmem_limit_bytes=VMEM_LIMIT)


def _ffn_body(x, pre, post, wg, wu, wd):
    u = _rms(x, pre).astype(BF16)
    g = _dot(u, wg)
    up = _dot(u, wu)
    hid = (g * _sigmoid(g) * up).astype(BF16)
    y = _dot(hid, wd)
    return x + 0.5 * _rms(y, post)


def _ffn_kernel(x_ref, pre_ref, post_ref, wg_ref, wu_ref, wd_ref, o_ref):
    o_ref[...] = _ffn_body(x_ref[...], pre_ref[...], post_ref[...],
                           wg_ref[...], wu_ref[...], wd_ref[...])


def _ffn_ple_kernel(x_ref, pre_ref, post_ref, wg_ref, wu_ref, wd_ref,
                    p_ref, ppre_ref, ppost_ref, wproj_ref, wgate_ref, o_ref):
    h = _ffn_body(x_ref[...], pre_ref[...], post_ref[...],
                  wg_ref[...], wu_ref[...], wd_ref[...])
    e = _dot(p_ref[...].astype(BF16), wproj_ref[...])
    g = _sigmoid(_dot(_rms(h, ppre_ref[...]).astype(BF16), wgate_ref[...]))
    o_ref[...] = h + _rms(g * e, ppost_ref[...])


def _ffn(x, pre, post, wg, wu, wd, ple=None, tm=512):
    s, d = x.shape
    row = pl.BlockSpec((tm, d), lambda i: (i, 0))
    in_specs = [row, _const_spec((1, d)), _const_spec((1, d)),
                _const_spec(wg.shape), _const_spec(wu.shape), _const_spec(wd.shape)]
    args = [x, pre, post, wg, wu, wd]
    kern = _ffn_kernel
    if ple is not None:
        p, ppre, ppost, wproj, wgate = ple
        in_specs += [pl.BlockSpec((tm, p.shape[1]), lambda i: (i, 0)),
                     _const_spec((1, d)), _const_spec((1, d)),
                     _const_spec(wproj.shape), _const_spec(wgate.shape)]
        args += [p, ppre, ppost, wproj, wgate]
        kern = _ffn_ple_kernel
    return pl.pallas_call(
        kern, grid=(s // tm,), in_specs=in_specs, out_specs=row,
        out_shape=jax.ShapeDtypeStruct((s, d), F32),
        compiler_params=_params(("parallel",)),
        name="ffn" if ple is None else "ffn_ple",
    )(*args)


def _dot_nt(a, b):
    return lax.dot_general(a, b, (((1,), (1,)), ((), ())), preferred_element_type=F32)


def _rope_tables_t(pos_ref, invf_ref):
    ang = pos_ref[...].astype(F32) * invf_ref[...]
    return jnp.cos(ang), jnp.sin(ang)


def _rope_t(x, c, s):
    x1, x2 = x[:ROPE_HALF], x[ROPE_HALF:]
    return x1 * c - x2 * s, x2 * c + x1 * s


def _proj_kernel(h_ref, pos_ref, invf_ref, pre_ref, win_ref, wsm_ref, gbias_ref, qn_ref, kvn_ref,
                 wuqt_ref, wuk_ref, wuvt_ref, vone_ref,
                 qt_ref, k_ref, vt_ref, mlqk_ref, mlv_ref, gt_ref, cum_ref, mlo_ref, gates_ref):
    u = _rms(h_ref[...], pre_ref[...]).astype(BF16)
    zdot = lambda a, b: _dot(u, win_ref[:, a:b])
    c, s = _rope_tables_t(pos_ref, invf_ref)
    rope = slice(ROPE_LO, ROPE_LO + MLA_ROPE)
    small_t = _dot_nt(wsm_ref[...], u)
    z_c = zdot(_P_CQ, _P_MLQK)
    gates_ref[...] = _sigmoid(zdot(_P_GATES, _P_END)).astype(BF16)

    pre = small_t[0:2 * N_GATE, :] + gbias_ref[...]
    is_f = lax.broadcasted_iota(jnp.int32, (2 * N_GATE, 1), 0) >= N_GATE
    gate = jnp.where(is_f, jax.nn.log_sigmoid(pre), pre)
    gt_ref[...] = gate
    r = lax.broadcasted_iota(jnp.int32, (ML_CHUNK, ML_CHUNK), 0)
    cidx = lax.broadcasted_iota(jnp.int32, (ML_CHUNK, ML_CHUNK), 1)
    for d, tri in enumerate((r <= cidx, r >= cidx)):
        for ch in range(gate.shape[1] // ML_CHUNK):
            cs = slice(ch * ML_CHUNK, (ch + 1) * ML_CHUNK)
            cum_ref[d * 2 * N_GATE:(d + 1) * 2 * N_GATE, cs] = jnp.dot(
                gate[:, cs], tri.astype(F32), preferred_element_type=F32,
                precision=lax.Precision.HIGHEST)

    tm = small_t.shape[1]
    k_rope = jnp.concatenate(
        [jnp.zeros((ROPE_LO, tm), F32), *_rope_t(small_t[rope, :], c, s),
         jnp.zeros((HEAD_PAD - ROPE_LO - MLA_ROPE, tm), F32)], axis=0).T

    cq = _rms(z_c[:, :MLA_Q_RANK], qn_ref[...]).astype(BF16)
    ckv = _rms(z_c[:, MLA_Q_RANK:], kvn_ref[...]).astype(BF16)
    qpt = _dot_nt(wuqt_ref[...], cq)
    kp = _dot(ckv, wuk_ref[...])
    vt_ref[...] = (_dot_nt(wuvt_ref[...], ckv) + vone_ref[...]).astype(BF16)
    mlqk_ref[...] = zdot(_P_MLQK, _P_MLV)
    mlv_ref[...] = zdot(_P_MLV, _P_MLO).astype(BF16)
    mlo_ref[...] = _sigmoid(zdot(_P_MLO, _P_GATES)).astype(BF16)
    for hd in range(MLA_HEADS):
        blk = slice(hd * HEAD_PAD, (hd + 1) * HEAD_PAD)
        x = qpt[blk, :]
        r1, r2 = _rope_t(x[rope, :], c, s)
        qt_ref[blk, :] = jnp.concatenate(
            [x[:ROPE_LO] * Q_SCALE, r1 * Q_SCALE, r2 * Q_SCALE, x[ROPE_LO + MLA_ROPE:]],
            axis=0).astype(BF16)
        k_ref[hd] = (kp[:, blk] + k_rope).astype(BF16)


def _proj(h, pos, invf, pre, win, wsm, gbias, qn, kvn, wuqt, wuk, wuvt, vone, tm=512):
    s, d = h.shape
    ng = 2 * N_GATE
    row = lambda w: pl.BlockSpec((tm, w), lambda i: (i, 0))
    col = lambda r: pl.BlockSpec((r, tm), lambda i: (0, i))
    cshape = jax.ShapeDtypeStruct((MLA_HEADS * HEAD_PAD, s), BF16)
    return pl.pallas_call(
        _proj_kernel, grid=(s // tm,),
        in_specs=[row(d), col(1), _const_spec(invf.shape), _const_spec((1, d)),
                  _const_spec(win.shape), _const_spec(wsm.shape), _const_spec(gbias.shape),
                  _const_spec(qn.shape), _const_spec(kvn.shape),
                  _const_spec(wuqt.shape), _const_spec(wuk.shape), _const_spec(wuvt.shape),
                  _const_spec(vone.shape)],
        out_specs=[col(MLA_HEADS * HEAD_PAD),
                   pl.BlockSpec((MLA_HEADS, tm, HEAD_PAD), lambda i: (0, i, 0)),
                   col(MLA_HEADS * HEAD_PAD),
                   row(2 * QK_W), row(ML_OUT), col(ng), col(N_DIR * ng),
                   row(ML_OUT), row(2 * D_MODEL)],
        out_shape=[cshape, jax.ShapeDtypeStruct((MLA_HEADS, s, HEAD_PAD), BF16), cshape,
                   jax.ShapeDtypeStruct((s, 2 * QK_W), F32),
                   jax.ShapeDtypeStruct((s, ML_OUT), BF16),
                   jax.ShapeDtypeStruct((ng, s), F32),
                   jax.ShapeDtypeStruct((N_DIR * ng, s), F32),
                   jax.ShapeDtypeStruct((s, ML_OUT), BF16),
                   jax.ShapeDtypeStruct((s, 2 * D_MODEL), BF16)],
        compiler_params=_params(("parallel",)),
        name="proj",
    )(h, pos, invf, pre, win, wsm, gbias, qn, kvn, wuqt, wuk, wuvt, vone)


BF16_ROWS = 16
V_ROWS = -(-(MLA_V + 1) // BF16_ROWS) * BF16_ROWS


def _attn_kernel(qt_ref, k_ref, vt_ref, o_ref, *scratch, tk, tq, steps):
    n = k_ref.shape[1] // tk
    nq = qt_ref.shape[1] // tq

    def tile(a):
        acc_sc, st0, st1, p0, p1 = scratch[5 * a:5 * a + 5]
        st, pb = (st0, st1), (p0, p1)
        qt = qt_ref[:, a * tq:(a + 1) * tq]

        def accumulate(t_prev, slot, alpha):
            off = pl.multiple_of(t_prev * tk, tk)
            vt = vt_ref[0:V_ROWS, pl.ds(off, tk)]
            acc_sc[...] = alpha * acc_sc[...] + _dot(vt, pb[slot][...])

        def prologue():
            acc_sc[...] = jnp.zeros(acc_sc.shape, F32)
            p1[...] = jnp.zeros(p1.shape, BF16)
            st0[...] = _dot(k_ref[0, 0:tk, :], qt)

        def step(t, slot, m, alpha_prev):
            off = pl.multiple_of(jnp.minimum(t + 1, n - 1) * tk, tk)
            st[1 - slot][...] = _dot(k_ref[0, pl.ds(off, tk), :], qt)
            accumulate(jnp.maximum(t - 1, 0), 1 - slot, alpha_prev)
            x = st[slot][...]
            m_new = jnp.maximum(m, jnp.max(x, axis=0, keepdims=True))
            pb[slot][...] = jnp.exp2(x - m_new).astype(BF16)
            return m_new, jnp.exp2(m - m_new)

        def body(i, carry):
            m, alpha = carry
            for j in range(steps):
                m, alpha = step(i * steps + j, j % 2, m, alpha)
            return m, alpha

        def loop():
            init = (jnp.full((1, tq), -jnp.inf, F32), jnp.ones((1, tq), F32))
            return lax.fori_loop(0, n // steps, body, init)[1]

        def epilogue(alpha):
            accumulate(n - 1, (n - 1) % 2, alpha)
            acc = acc_sc[...]
            o_ref[:, a * tq:(a + 1) * tq] = (acc[0:MLA_V] / acc[MLA_V:MLA_V + 1]).astype(o_ref.dtype)

        return prologue, loop, epilogue

    tiles = [tile(a) for a in range(nq)]
    tiles[0][0]()
    for a in range(nq):
        alpha = tiles[a][1]()
        if a + 1 < nq:
            tiles[a + 1][0]()
        tiles[a][2](alpha)


def _attn(qt, k, vt, tq=512, nq=4, tk=256, steps=32):
    nh, s, w = k.shape
    assert steps % 2 == 0 and (s // tk) % steps == 0
    sub_scratch = [pltpu.VMEM((V_ROWS, tq), F32),
                   pltpu.VMEM((tk, tq), F32), pltpu.VMEM((tk, tq), F32),
                   pltpu.VMEM((tk, tq), BF16), pltpu.VMEM((tk, tq), BF16)]
    return pl.pallas_call(
        functools.partial(_attn_kernel, tk=tk, tq=tq, steps=steps), grid=(nh, s // (nq * tq)),
        in_specs=[pl.BlockSpec((w, nq * tq), lambda h, i: (h, i)),
                  pl.BlockSpec((1, s, w), lambda h, i: (h, 0, 0)),
                  pl.BlockSpec((w, s), lambda h, i: (h, 0))],
        out_specs=pl.BlockSpec((MLA_V, nq * tq), lambda h, i: (h, i)),
        out_shape=jax.ShapeDtypeStruct((nh * MLA_V, s), BF16),
        scratch_shapes=sub_scratch * nq,
        compiler_params=_params(("parallel", "parallel")),
        name="attn",
    )(qt, k, vt)


CONV_HALO = 8


def _conv_kernel(x_ref, prev_ref, next_ref, w_ref, b_ref, q_ref, kt_ref, buf, *, tm):
    i = pl.program_id(0)
    n = pl.num_programs(0)
    buf[CONV_HALO:CONV_HALO + tm, :] = x_ref[...]
    buf[0:CONV_HALO, :] = jnp.where(i > 0, prev_ref[...], 0.0)
    buf[CONV_HALO + tm:, :] = jnp.where(i < n - 1, next_ref[...], 0.0)
    acc = jnp.broadcast_to(b_ref[...], (tm, x_ref.shape[1]))
    for j in range(CONV_W):
        start = CONV_HALO - CONV_W // 2 + j
        acc = acc + w_ref[j:j + 1, :] * buf[start:start + tm, :]
    y = acc * _sigmoid(acc)
    pad = jnp.zeros((tm, HEAD_PAD - ML_QK), F32)
    q_ref[...] = jnp.concatenate(
        [piece for hd in range(ML_HEADS) for piece in (y[:, hd * ML_QK:(hd + 1) * ML_QK], pad)],
        axis=1).astype(BF16)
    kt_ref[...] = (y[:, QK_W:] * ML_QK ** -0.5).T


def _conv(x, w, b, tm=1024):
    s, c = x.shape
    r = tm // CONV_HALO
    nblk = s // CONV_HALO
    return pl.pallas_call(
        functools.partial(_conv_kernel, tm=tm), grid=(s // tm,),
        in_specs=[pl.BlockSpec((tm, c), lambda i: (i, 0)),
                  pl.BlockSpec((CONV_HALO, c), lambda i: (jnp.maximum(i * r - 1, 0), 0)),
                  pl.BlockSpec((CONV_HALO, c), lambda i: (jnp.minimum((i + 1) * r, nblk - 1), 0)),
                  _const_spec(w.shape), _const_spec(b.shape)],
        out_specs=[pl.BlockSpec((tm, QK_PAD), lambda i: (i, 0)),
                   pl.BlockSpec((QK_W, tm), lambda i: (0, i))],
        out_shape=[jax.ShapeDtypeStruct((s, QK_PAD), BF16),
                   jax.ShapeDtypeStruct((QK_W, s), F32)],
        scratch_shapes=[pltpu.VMEM((tm + 2 * CONV_HALO, c), F32)],
        compiler_params=_params(("parallel",)),
        name="conv",
    )(x, x, x, w, b)


def _mlstm_kernel(qf_ref, ktf_ref, vf_ref, gf_ref, cumf_ref, qb_ref, ktb_ref, vb_ref, gb_ref,
                  cumb_ref, hf_ref, hb_ref, c_sc, m_sc, *, chunks):
    L = ML_CHUNK

    @pl.when(pl.program_id(0) == 0)
    def _():
        c_sc[...] = jnp.zeros(c_sc.shape, F32)
        m_sc[...] = jnp.zeros(m_sc.shape, F32)

    row = lax.broadcasted_iota(jnp.int32, (L, L), 0)
    col = lax.broadcasted_iota(jnp.int32, (L, L), 1)
    mask = [col <= row, col >= row]
    ones_blk = jnp.ones((L, LANES), BF16)
    k_pad = jnp.zeros((HEAD_PAD - ML_QK, L), BF16)
    c_pad = jnp.zeros((HEAD_PAD - ML_QK, ML_V + LANES), BF16)
    refs = ((qf_ref, ktf_ref, vf_ref, gf_ref, hf_ref), (qb_ref, ktb_ref, vb_ref, gb_ref, hb_ref))
    cum_refs = (cumf_ref, cumb_ref)
    heads = [(d, hd) for d in range(N_DIR) for hd in range(ML_HEADS)]
    blk = lambda hd: slice(hd * HEAD_PAD, (hd + 1) * HEAD_PAD)
    kblk = lambda hd: slice(hd * ML_QK, (hd + 1) * ML_QK)

    wins = [[slice(lc * L, (lc + 1) * L) for lc in (step, chunks - 1 - step)]
            for step in range(chunks)]
    gates = [[refs[d][3][:, win[d]] for d in range(N_DIR)] for win in wins]
    cums = [[cum_refs[d][:, win[d]] for d in range(N_DIR)] for win in wins]

    q, kt, v_aug, qk, qc, m_prev = [{} for _ in range(6)]
    for step, win in enumerate(wins):
        gate, cum = gates[step], cums[step]
        for d, hd in heads:
            j = d * ML_HEADS + hd
            c = (step, j)
            q[c] = refs[d][0][win[d], blk(hd)]
            kt[c] = refs[d][1][kblk(hd), win[d]]
            v_aug[c] = jnp.concatenate([refs[d][2][win[d], blk(hd)], ones_blk], axis=1)
            qk[c] = _dot(q[c], jnp.concatenate([kt[c].astype(BF16), k_pad], axis=0))
            qc[c] = _dot(q[c], jnp.concatenate([c_sc[j].astype(BF16), c_pad], axis=0))
        for d, hd in heads:
            j = d * ML_HEADS + hd
            c = (step, j)
            br = cum[d][N_GATE + j:N_GATE + j + 1, :]
            ir = gate[d][j:j + 1, :]
            m_prev[c] = m_sc[j][0:1, 0:1]
            b_end = br[:, L - 1:L] if d == 0 else br[:, 0:1]
            g_row = b_end - br + ir
            m_new = jnp.maximum(b_end + m_prev[c], jnp.max(g_row, axis=1, keepdims=True))
            decay = jnp.exp(b_end + m_prev[c] - m_new)
            kw_t = (kt[c] * jnp.exp(g_row - m_new)).astype(BF16)
            c_sc[j] = decay * c_sc[j] + _dot(kw_t, v_aug[c])
            m_sc[j] = jnp.broadcast_to(m_new, m_sc.shape[1:])

    chains = [(step, d, hd) for step in range(chunks) for d, hd in heads]
    a, mm, mm_b, bc, sv = {}, {}, {}, {}, {}
    for step, d, hd in chains:
        j = d * ML_HEADS + hd
        c = (step, j)
        br = cums[step][d][N_GATE + j:N_GATE + j + 1, :]
        lf = gates[step][d][N_GATE + j:N_GATE + j + 1, :]
        a[c] = jnp.where(mask[d], gates[step][d][j:j + 1, :] - br, -jnp.inf)
        mm[c] = jnp.maximum(m_prev[c], jnp.max(a[c], axis=1, keepdims=True))
        bc[c] = jnp.sum(jnp.where(mask[d], lf, 0.0), axis=1, keepdims=True)
    for step, d, hd in chains:
        c = (step, d * ML_HEADS + hd)
        mm_b[c] = jnp.broadcast_to(mm[c], (L, L))
        smat = qk[c] * jnp.exp(a[c] - mm_b[c])
        sv[c] = _dot(smat.astype(BF16), v_aug[c])
    for step, d, hd in chains:
        c = (step, d * ML_HEADS + hd)
        inter = jnp.exp(m_prev[c] - mm_b[c])
        nv = sv[c] + jnp.concatenate([inter, inter], axis=1) * qc[c]
        den = nv[:, ML_V:ML_V + 1]
        inv = 1.0 / jnp.maximum(jnp.abs(den), jnp.exp(-(bc[c] + mm[c])))
        refs[d][4][wins[step][d], blk(hd)] = (nv[:, :ML_V] * inv).astype(BF16)


def _mlstm(q, kt, v, gate_t, cum_t, chunks=4):
    s = q.shape[0]
    tb = chunks * ML_CHUNK
    nb = s // tb
    ng = 2 * N_GATE
    rows = lambda w, f: pl.BlockSpec((tb, w), lambda c: (f(c), 0))
    cols = lambda h, f, r=0: pl.BlockSpec((h, tb), lambda c: (r, f(c)))
    fwd = lambda c: c
    bwd = lambda c: nb - 1 - c
    hshape = jax.ShapeDtypeStruct((s, ML_OUT), BF16)
    return pl.pallas_call(
        functools.partial(_mlstm_kernel, chunks=chunks), grid=(nb,),
        in_specs=[rows(QK_PAD, fwd), cols(QK_W, fwd), rows(ML_OUT, fwd),
                  cols(ng, fwd), cols(ng, fwd, 0),
                  rows(QK_PAD, bwd), cols(QK_W, bwd), rows(ML_OUT, bwd),
                  cols(ng, bwd), cols(ng, bwd, 1)],
        out_specs=[rows(ML_OUT, fwd), rows(ML_OUT, bwd)],
        out_shape=[hshape, hshape],
        scratch_shapes=[pltpu.VMEM((N_GATE, ML_QK, ML_V + LANES), F32),
                        pltpu.VMEM((N_GATE, 8, LANES), F32)],
        compiler_params=_params(("arbitrary",)),
        name="mlstm",
    )(q, kt, v, gate_t, cum_t, q, kt, v, gate_t, cum_t)


def _merge_kernel(h_ref, att_ref, hf_ref, hb_ref, mlo_ref, gates_ref, hn_ref,
                  wba_ref, wbm_ref, wout_ref, post_ref, o_ref):
    a = lax.dot_general(att_ref[...], wba_ref[...], (((0,), (0,)), ((), ())),
                        preferred_element_type=F32)
    hm = hf_ref[...].astype(F32) + hb_ref[...].astype(F32)
    heads = []
    for hd in range(ML_HEADS):
        blk = slice(hd * ML_V, (hd + 1) * ML_V)
        heads.append(_rms(hm[:, blk], hn_ref[:, blk]))
    hm = jnp.concatenate(heads, axis=1) * mlo_ref[...]
    bm = _dot(hm.astype(BF16), wbm_ref[...])
    g = gates_ref[...]
    mixed = _dot((g[:, :D_MODEL] * a + g[:, D_MODEL:] * bm).astype(BF16), wout_ref[...])
    o_ref[...] = h_ref[...] + _rms(mixed, post_ref[...])


def _merge(h, att, hf, hb, mlo, gates, hn, wba, wbm, wout, post, tm=512):
    s, d = h.shape
    row = lambda w: pl.BlockSpec((tm, w), lambda i: (i, 0))
    return pl.pallas_call(
        _merge_kernel, grid=(s // tm,),
        in_specs=[row(d), pl.BlockSpec((att.shape[0], tm), lambda i: (0, i)),
                  row(ML_OUT), row(ML_OUT), row(ML_OUT),
                  row(2 * d), _const_spec(hn.shape), _const_spec(wba.shape),
                  _const_spec(wbm.shape), _const_spec(wout.shape), _const_spec(post.shape)],
        out_specs=row(d),
        out_shape=jax.ShapeDtypeStruct((s, d), F32),
        compiler_params=_params(("parallel",)),
        name="merge",
    )(h, att, hf, hb, mlo, gates, hn, wba, wbm, wout, post)


def _pad_heads(w, nh, width):
    r = w.shape[0]
    w = w.reshape(r, nh, width)
    return jnp.pad(w, ((0, 0), (0, 0), (0, HEAD_PAD - width))).reshape(r, nh * HEAD_PAD)


def _permute_w_in(w_in):
    d = w_in.shape[0]
    small = jnp.zeros((LANES, d), w_in.dtype)
    small = small.at[0:N_GATE].set(w_in[:, _O_MLI:_O_MLF].T)
    small = small.at[N_GATE:2 * N_GATE].set(w_in[:, _O_MLF:_O_MLO].T)
    small = small.at[ROPE_LO:ROPE_LO + MLA_ROPE].set(w_in[:, _O_KR:_O_MLQ].T)
    main = jnp.concatenate([w_in[:, _O_CQ:_O_KR].astype(BF16), w_in[:, _O_MLQ:_O_MLI].astype(BF16),
                            w_in[:, _O_MLO:_O_END].astype(BF16)], axis=1)
    return main, small.astype(BF16)


def kernel(x, p, positions, ffn1_pre_norm, ffn1_post_norm, ffn1_w_gate, ffn1_w_up, ffn1_w_down, mix_pre_norm, mix_post_norm, w_in, mla_q_norm, mla_kv_norm, mla_w_uq, mla_w_uk, mla_w_uv, ml_conv_w, ml_conv_b, ml_i_bias, ml_f_bias, ml_head_norm, w_branch_mla, w_branch_ml, w_out, ffn2_pre_norm, ffn2_post_norm, ffn2_w_gate, ffn2_w_up, ffn2_w_down, ple_pre_norm, ple_post_norm, ple_w_proj, ple_w_gate):
    depth, batch = p.shape[0], x.shape[0]
    assert depth == 1 and batch == 1, "kernel is specialised to DEPTH == 1, BATCH == 1"
    bf = lambda w: w.astype(BF16)
    h = x[0]
    pos = positions

    inv_freq = ROPE_THETA ** (-jnp.arange(0, MLA_ROPE, 2, dtype=F32) / MLA_ROPE)
    invf = inv_freq[:, None]

    i = 0
    h = _ffn(h, ffn1_pre_norm[i][None], ffn1_post_norm[i][None],
             bf(ffn1_w_gate[i]), bf(ffn1_w_up[i]), bf(ffn1_w_down[i]))

    vone = jnp.zeros((MLA_HEADS, HEAD_PAD), F32).at[:, MLA_V].set(1.0).reshape(-1, 1)
    win_main, win_small = _permute_w_in(w_in[i])
    gate_bias = jnp.concatenate([ml_i_bias[i].reshape(-1), ml_f_bias[i].reshape(-1)])[:, None]
    qt, k, vt, mlqk, mlv, gate_t, cum_t, mlo, gates = _proj(
        h, pos, invf, mix_pre_norm[i][None], win_main, win_small,
        gate_bias, mla_q_norm[i][None], mla_kv_norm[i][None],
        bf(_pad_heads(mla_w_uq[i], MLA_HEADS, MLA_NOPE + MLA_ROPE)).T,
        bf(_pad_heads(mla_w_uk[i], MLA_HEADS, MLA_NOPE)),
        bf(_pad_heads(mla_w_uv[i], MLA_HEADS, MLA_V)).T, vone)

    att = _attn(qt, k, vt)

    conv_w = jnp.pad(ml_conv_w[i], ((0, 8 - CONV_W), (0, 0)))
    conv_b = ml_conv_b[i][None]
    q_conv, kt_conv = _conv(mlqk, conv_w, conv_b)

    hf, hb = _mlstm(q_conv, kt_conv, mlv, gate_t, cum_t)

    h = _merge(h, att, hf, hb, mlo, gates, ml_head_norm[i][None], bf(w_branch_mla[i]),
               bf(w_branch_ml[i]), bf(w_out[i]), mix_post_norm[i][None])

    h = _ffn(h, ffn2_pre_norm[i][None], ffn2_post_norm[i][None],
             bf(ffn2_w_gate[i]), bf(ffn2_w_up[i]), bf(ffn2_w_down[i]),
             ple=(p[i, 0], ple_pre_norm[i][None], ple_post_norm[i][None],
                  bf(ple_w_proj[i]), bf(ple_w_gate[i])))
    return h[None]
```

```python
import functools
import math

import jax
import jax.numpy as jnp
from jax import lax
from jax.experimental import pallas as pl
from jax.experimental.pallas import tpu as pltpu

D_MODEL = 1024
SEQ = 16384
PLE_DIM = 256
D_FF = 2816
EPS = 1e-6
MLA_HEADS = 8
MLA_Q_RANK = 256
MLA_KV_RANK = 256
MLA_NOPE = 64
MLA_ROPE = 32
MLA_V = 64
ROPE_THETA = 10000.0
ML_HEADS = 4
ML_QK = 64
ML_V = 128
ML_CHUNK = 128
CONV_W = 5
N_DIR = 2
MLA_OUT = MLA_HEADS * MLA_V
ML_OUT = ML_HEADS * ML_V

LANES = 128
HEAD_PAD = LANES
ROPE_LO = MLA_NOPE
ROPE_HALF = MLA_ROPE // 2
VMEM_LIMIT = 56 * 1024 * 1024

F32 = jnp.float32
BF16 = jnp.bfloat16
LOG2E = math.log2(math.e)
Q_SCALE = (MLA_NOPE + MLA_ROPE) ** -0.5 * LOG2E

_O_CQ = 0
_O_CKV = _O_CQ + MLA_Q_RANK
_O_KR = _O_CKV + MLA_KV_RANK
_O_MLQ = _O_KR + MLA_ROPE
_O_MLK = _O_MLQ + ML_HEADS * ML_QK
_O_MLV = _O_MLK + ML_HEADS * ML_QK
_O_MLI = _O_MLV + ML_OUT
_O_MLF = _O_MLI + N_DIR * ML_HEADS
_O_MLO = _O_MLF + N_DIR * ML_HEADS
_O_GATES = _O_MLO + ML_OUT
_O_END = _O_GATES + 2 * D_MODEL

N_GATE = N_DIR * ML_HEADS
QK_W = ML_HEADS * ML_QK
QK_PAD = ML_HEADS * HEAD_PAD

_P_CQ = 0
_P_CKV = _P_CQ + MLA_Q_RANK
_P_MLQK = _P_CKV + MLA_KV_RANK
_P_MLV = _P_MLQK + 2 * QK_W
_P_MLO = _P_MLV + ML_OUT
_P_GATES = _P_MLO + ML_OUT
_P_END = _P_GATES + 2 * D_MODEL


def _rms(x, w):
    ms = jnp.mean(x * x, axis=-1, keepdims=True)
    return x * lax.rsqrt(ms + EPS) * w


def _sigmoid(x):
    return 1.0 / (1.0 + jnp.exp(-x))


def _dot(a, b):
    return jnp.dot(a, b, preferred_element_type=F32)


def _const_spec(shape):
    nd = len(shape)
    return pl.BlockSpec(shape, lambda *_: (0,) * nd, pipeline_mode=pl.Buffered(1))


def _params(sem):
    return pltpu.CompilerParams(dimension_semantics=sem, vmem_limit_bytes=VMEM_LIMIT)


def _ffn_body(x, pre, post, wg, wu, wd):
    u = _rms(x, pre).astype(BF16)
    g = _dot(u, wg)
    up = _dot(u, wu)
    hid = (g * _sigmoid(g) * up).astype(BF16)
    y = _dot(hid, wd)
    return x + 0.5 * _rms(y, post)


STAGE_CHUNKS = 8


def _stage_weights(w_hbm, w_sc, stage, sem):
    jobs = [(w, sc, st, c) for w, sc, st in zip(w_hbm, w_sc, stage) for c in range(STAGE_CHUNKS)]

    def copy(i):
        w, _, st, c = jobs[i]
        rows = st.shape[1]
        return pltpu.make_async_copy(w.at[pl.ds(c * rows, rows), :], st.at[i % 2], sem.at[i % 2])

    copy(0).start()
    for i, (_, sc, st, c) in enumerate(jobs):
        if i + 1 < len(jobs):
            copy(i + 1).start()
        copy(i).wait()
        rows = st.shape[1]
        sc[c * rows:(c + 1) * rows, :] = st[i % 2].astype(BF16)


def _ffn_weights(w_hbm, scratch):
    wg_sc, wu_sc, wd_sc, stage_wide, stage_tall, sem = scratch

    @pl.when(pl.program_id(0) == 0)
    def _():
        _stage_weights(w_hbm, (wg_sc, wu_sc, wd_sc), (stage_wide, stage_wide, stage_tall), sem)

    return wg_sc[...], wu_sc[...], wd_sc[...]


def _ffn_kernel(x_ref, pre_ref, post_ref, wg_hbm, wu_hbm, wd_hbm, o_ref, *scratch):
    o_ref[...] = _ffn_body(x_ref[...], pre_ref[...], post_ref[...],
                           *_ffn_weights((wg_hbm, wu_hbm, wd_hbm), scratch))


def _ffn_ple_kernel(x_ref, pre_ref, post_ref, wg_hbm, wu_hbm, wd_hbm,
                    p_ref, ppre_ref, ppost_ref, wproj_ref, wgate_ref, o_ref, *scratch):
    h = _ffn_body(x_ref[...], pre_ref[...], post_ref[...],
                  *_ffn_weights((wg_hbm, wu_hbm, wd_hbm), scratch))
    e = _dot(p_ref[...].astype(BF16), wproj_ref[...])
    g = _sigmoid(_dot(_rms(h, ppre_ref[...]).astype(BF16), wgate_ref[...]))
    o_ref[...] = h + _rms(g * e, ppost_ref[...])


def _ffn(x, pre, post, wg, wu, wd, ple=None, tm=512):
    s, d = x.shape
    f = wg.shape[1]
    row = pl.BlockSpec((tm, d), lambda i: (i, 0))
    hbm = pl.BlockSpec(memory_space=pl.ANY)
    in_specs = [row, _const_spec((1, d)), _const_spec((1, d)), hbm, hbm, hbm]
    args = [x, pre, post, wg, wu, wd]
    kern = _ffn_kernel
    if ple is not None:
        p, ppre, ppost, wproj, wgate = ple
        in_specs += [pl.BlockSpec((tm, p.shape[1]), lambda i: (i, 0)),
                     _const_spec((1, d)), _const_spec((1, d)),
                     _const_spec(wproj.shape), _const_spec(wgate.shape)]
        args += [p, ppre, ppost, wproj, wgate]
        kern = _ffn_ple_kernel
    return pl.pallas_call(
        kern, grid=(s // tm,), in_specs=in_specs, out_specs=row,
        out_shape=jax.ShapeDtypeStruct((s, d), F32),
        scratch_shapes=[pltpu.VMEM((d, f), BF16), pltpu.VMEM((d, f), BF16), pltpu.VMEM((f, d), BF16),
                        pltpu.VMEM((2, d // STAGE_CHUNKS, f), F32),
                        pltpu.VMEM((2, f // STAGE_CHUNKS, d), F32),
                        pltpu.SemaphoreType.DMA((2,))],
        compiler_params=_params(("arbitrary",)),
        name="ffn" if ple is None else "ffn_ple",
    )(*args)


def _dot_nt(a, b):
    return lax.dot_general(a, b, (((1,), (1,)), ((), ())), preferred_element_type=F32)


def _rope_tables_t(pos_ref, invf_ref):
    ang = pos_ref[...].astype(F32) * invf_ref[...]
    return jnp.cos(ang), jnp.sin(ang)


def _rope_t(x, c, s):
    x1, x2 = x[:ROPE_HALF], x[ROPE_HALF:]
    return x1 * c - x2 * s, x2 * c + x1 * s


def _proj_kernel(h_ref, pos_ref, invf_ref, pre_ref, win_ref, wsm_ref, gbias_ref, qn_ref, kvn_ref,
                 wuqt_ref, wuk_ref, wuvt_ref, vone_ref,
                 qt_ref, k_ref, vt_ref, mlqk_ref, mlv_ref, gt_ref, cum_ref, mlo_ref, gates_ref):
    u = _rms(h_ref[...], pre_ref[...]).astype(BF16)
    zdot = lambda a, b: _dot(u, win_ref[:, a:b])
    c, s = _rope_tables_t(pos_ref, invf_ref)
    rope = slice(ROPE_LO, ROPE_LO + MLA_ROPE)
    small_t = _dot_nt(wsm_ref[...], u)
    z_c = zdot(_P_CQ, _P_MLQK)
    gates_ref[...] = _sigmoid(zdot(_P_GATES, _P_END)).astype(BF16)

    pre = small_t[0:2 * N_GATE, :] + gbias_ref[...]
    is_f = lax.broadcasted_iota(jnp.int32, (2 * N_GATE, 1), 0) >= N_GATE
    gate = jnp.where(is_f, jax.nn.log_sigmoid(pre), pre)
    gt_ref[...] = gate
    r = lax.broadcasted_iota(jnp.int32, (ML_CHUNK, ML_CHUNK), 0)
    cidx = lax.broadcasted_iota(jnp.int32, (ML_CHUNK, ML_CHUNK), 1)
    for d, tri in enumerate((r <= cidx, r >= cidx)):
        for ch in range(gate.shape[1] // ML_CHUNK):
            cs = slice(ch * ML_CHUNK, (ch + 1) * ML_CHUNK)
            cum_ref[d * 2 * N_GATE:(d + 1) * 2 * N_GATE, cs] = jnp.dot(
                gate[:, cs], tri.astype(F32), preferred_element_type=F32,
                precision=lax.Precision.HIGHEST)

    tm = small_t.shape[1]
    k_rope = jnp.concatenate(
        [jnp.zeros((ROPE_LO, tm), F32), *_rope_t(small_t[rope, :], c, s),
         jnp.zeros((HEAD_PAD - ROPE_LO - MLA_ROPE, tm), F32)], axis=0).T

    cq = _rms(z_c[:, :MLA_Q_RANK], qn_ref[...]).astype(BF16)
    ckv = _rms(z_c[:, MLA_Q_RANK:], kvn_ref[...]).astype(BF16)
    qpt = _dot_nt(wuqt_ref[...], cq)
    kp = _dot(ckv, wuk_ref[...])
    vt_ref[...] = (_dot_nt(wuvt_ref[...], ckv) + vone_ref[...]).astype(BF16)
    mlqk_ref[...] = zdot(_P_MLQK, _P_MLV)
    mlv_ref[...] = zdot(_P_MLV, _P_MLO).astype(BF16)
    mlo_ref[...] = _sigmoid(zdot(_P_MLO, _P_GATES)).astype(BF16)
    for hd in range(MLA_HEADS):
        blk = slice(hd * HEAD_PAD, (hd + 1) * HEAD_PAD)
        x = qpt[blk, :]
        r1, r2 = _rope_t(x[rope, :], c, s)
        qt_ref[blk, :] = jnp.concatenate(
            [x[:ROPE_LO] * Q_SCALE, r1 * Q_SCALE, r2 * Q_SCALE, x[ROPE_LO + MLA_ROPE:]],
            axis=0).astype(BF16)
        k_ref[hd] = (kp[:, blk] + k_rope).astype(BF16)


def _proj(h, pos, invf, pre, win, wsm, gbias, qn, kvn, wuqt, wuk, wuvt, vone, tm=512):
    s, d = h.shape
    ng = 2 * N_GATE
    row = lambda w: pl.BlockSpec((tm, w), lambda i: (i, 0))
    col = lambda r: pl.BlockSpec((r, tm), lambda i: (0, i))
    cshape = jax.ShapeDtypeStruct((MLA_HEADS * HEAD_PAD, s), BF16)
    return pl.pallas_call(
        _proj_kernel, grid=(s // tm,),
        in_specs=[row(d), col(1), _const_spec(invf.shape), _const_spec((1, d)),
                  _const_spec(win.shape), _const_spec(wsm.shape), _const_spec(gbias.shape),
                  _const_spec(qn.shape), _const_spec(kvn.shape),
                  _const_spec(wuqt.shape), _const_spec(wuk.shape), _const_spec(wuvt.shape),
                  _const_spec(vone.shape)],
        out_specs=[col(MLA_HEADS * HEAD_PAD),
                   pl.BlockSpec((MLA_HEADS, tm, HEAD_PAD), lambda i: (0, i, 0)),
                   col(MLA_HEADS * HEAD_PAD),
                   row(2 * QK_W), row(ML_OUT), col(ng), col(N_DIR * ng),
                   row(ML_OUT), row(2 * D_MODEL)],
        out_shape=[cshape, jax.ShapeDtypeStruct((MLA_HEADS, s, HEAD_PAD), BF16), cshape,
                   jax.ShapeDtypeStruct((s, 2 * QK_W), F32),
                   jax.ShapeDtypeStruct((s, ML_OUT), BF16),
                   jax.ShapeDtypeStruct((ng, s), F32),
                   jax.ShapeDtypeStruct((N_DIR * ng, s), F32),
                   jax.ShapeDtypeStruct((s, ML_OUT), BF16),
                   jax.ShapeDtypeStruct((s, 2 * D_MODEL), BF16)],
        compiler_params=_params(("parallel",)),
        name="proj",
    )(h, pos, invf, pre, win, wsm, gbias, qn, kvn, wuqt, wuk, wuvt, vone)


BF16_ROWS = 16
V_ROWS = -(-(MLA_V + 1) // BF16_ROWS) * BF16_ROWS


def _attn_kernel(qt_ref, k_ref, vt_ref, o_ref, *scratch, tk, tq, steps):
    n = k_ref.shape[1] // tk
    nq = qt_ref.shape[1] // tq

    def tile(a):
        acc_sc, st0, st1, p0, p1 = scratch[5 * a:5 * a + 5]
        st, pb = (st0, st1), (p0, p1)
        qt = qt_ref[:, a * tq:(a + 1) * tq]

        def accumulate(t_prev, slot, alpha):
            off = pl.multiple_of(t_prev * tk, tk)
            vt = vt_ref[0:V_ROWS, pl.ds(off, tk)]
            acc_sc[...] = alpha * acc_sc[...] + _dot(vt, pb[slot][...])

        def prologue():
            acc_sc[...] = jnp.zeros(acc_sc.shape, F32)
            p1[...] = jnp.zeros(p1.shape, BF16)
            st0[...] = _dot(k_ref[0, 0:tk, :], qt)

        def step(t, slot, m, alpha_prev):
            off = pl.multiple_of(jnp.minimum(t + 1, n - 1) * tk, tk)
            st[1 - slot][...] = _dot(k_ref[0, pl.ds(off, tk), :], qt)
            accumulate(jnp.maximum(t - 1, 0), 1 - slot, alpha_prev)
            x = st[slot][...]
            m_new = jnp.maximum(m, jnp.max(x, axis=0, keepdims=True))
            pb[slot][...] = jnp.exp2(x - m_new).astype(BF16)
            return m_new, jnp.exp2(m - m_new)

        def body(i, carry):
            m, alpha = carry
            for j in range(steps):
                m, alpha = step(i * steps + j, j % 2, m, alpha)
            return m, alpha

        def loop():
            init = (jnp.full((1, tq), -jnp.inf, F32), jnp.ones((1, tq), F32))
            return lax.fori_loop(0, n // steps, body, init)[1]

        def epilogue(alpha):
            accumulate(n - 1, (n - 1) % 2, alpha)
            acc = acc_sc[...]
            o_ref[:, a * tq:(a + 1) * tq] = (acc[0:MLA_V] / acc[MLA_V:MLA_V + 1]).astype(o_ref.dtype)

        return prologue, loop, epilogue

    tiles = [tile(a) for a in range(nq)]
    tiles[0][0]()
    for a in range(nq):
        alpha = tiles[a][1]()
        if a + 1 < nq:
            tiles[a + 1][0]()
        tiles[a][2](alpha)


def _attn(qt, k, vt, tq=512, nq=4, tk=256, steps=32):
    nh, s, w = k.shape
    assert steps % 2 == 0 and (s // tk) % steps == 0
    sub_scratch = [pltpu.VMEM((V_ROWS, tq), F32),
                   pltpu.VMEM((tk, tq), F32), pltpu.VMEM((tk, tq), F32),
                   pltpu.VMEM((tk, tq), BF16), pltpu.VMEM((tk, tq), BF16)]
    return pl.pallas_call(
        functools.partial(_attn_kernel, tk=tk, tq=tq, steps=steps), grid=(nh, s // (nq * tq)),
        in_specs=[pl.BlockSpec((w, nq * tq), lambda h, i: (h, i)),
                  pl.BlockSpec((1, s, w), lambda h, i: (h, 0, 0)),
                  pl.BlockSpec((w, s), lambda h, i: (h, 0))],
        out_specs=pl.BlockSpec((MLA_V, nq * tq), lambda h, i: (h, i)),
        out_shape=jax.ShapeDtypeStruct((nh * MLA_V, s), BF16),
        scratch_shapes=sub_scratch * nq,
        compiler_params=_params(("parallel", "parallel")),
        name="attn",
    )(qt, k, vt)


CONV_HALO = 8


def _conv_kernel(x_ref, prev_ref, next_ref, w_ref, b_ref, q_ref, kt_ref, buf, *, tm):
    i = pl.program_id(0)
    n = pl.num_programs(0)
    buf[CONV_HALO:CONV_HALO + tm, :] = x_ref[...]
    buf[0:CONV_HALO, :] = jnp.where(i > 0, prev_ref[...], 0.0)
    buf[CONV_HALO + tm:, :] = jnp.where(i < n - 1, next_ref[...], 0.0)
    acc = jnp.broadcast_to(b_ref[...], (tm, x_ref.shape[1]))
    for j in range(CONV_W):
        start = CONV_HALO - CONV_W // 2 + j
        acc = acc + w_ref[j:j + 1, :] * buf[start:start + tm, :]
    y = acc * _sigmoid(acc)
    pad = jnp.zeros((tm, HEAD_PAD - ML_QK), F32)
    q_ref[...] = jnp.concatenate(
        [piece for hd in range(ML_HEADS) for piece in (y[:, hd * ML_QK:(hd + 1) * ML_QK], pad)],
        axis=1).astype(BF16)
    kt_ref[...] = (y[:, QK_W:] * ML_QK ** -0.5).T


def _conv(x, w, b, tm=1024):
    s, c = x.shape
    r = tm // CONV_HALO
    nblk = s // CONV_HALO
    return pl.pallas_call(
        functools.partial(_conv_kernel, tm=tm), grid=(s // tm,),
        in_specs=[pl.BlockSpec((tm, c), lambda i: (i, 0)),
                  pl.BlockSpec((CONV_HALO, c), lambda i: (jnp.maximum(i * r - 1, 0), 0)),
                  pl.BlockSpec((CONV_HALO, c), lambda i: (jnp.minimum((i + 1) * r, nblk - 1), 0)),
                  _const_spec(w.shape), _const_spec(b.shape)],
        out_specs=[pl.BlockSpec((tm, QK_PAD), lambda i: (i, 0)),
                   pl.BlockSpec((QK_W, tm), lambda i: (0, i))],
        out_shape=[jax.ShapeDtypeStruct((s, QK_PAD), BF16),
                   jax.ShapeDtypeStruct((QK_W, s), F32)],
        scratch_shapes=[pltpu.VMEM((tm + 2 * CONV_HALO, c), F32)],
        compiler_params=_params(("parallel",)),
        name="conv",
    )(x, x, x, w, b)


def _mlstm_kernel(qf_ref, ktf_ref, vf_ref, gf_ref, cumf_ref, qb_ref, ktb_ref, vb_ref, gb_ref,
                  cumb_ref, hf_ref, hb_ref, c_sc, m_sc, *, chunks):
    L = ML_CHUNK

    @pl.when(pl.program_id(0) == 0)
    def _():
        c_sc[...] = jnp.zeros(c_sc.shape, F32)
        m_sc[...] = jnp.zeros(m_sc.shape, F32)

    row = lax.broadcasted_iota(jnp.int32, (L, L), 0)
    col = lax.broadcasted_iota(jnp.int32, (L, L), 1)
    mask = [col <= row, col >= row]
    ones_blk = jnp.ones((L, LANES), BF16)
    k_pad = jnp.zeros((HEAD_PAD - ML_QK, L), BF16)
    c_pad = jnp.zeros((HEAD_PAD - ML_QK, ML_V + LANES), BF16)
    refs = ((qf_ref, ktf_ref, vf_ref, gf_ref, hf_ref), (qb_ref, ktb_ref, vb_ref, gb_ref, hb_ref))
    cum_refs = (cumf_ref, cumb_ref)
    heads = [(d, hd) for d in range(N_DIR) for hd in range(ML_HEADS)]
    blk = lambda hd: slice(hd * HEAD_PAD, (hd + 1) * HEAD_PAD)
    kblk = lambda hd: slice(hd * ML_QK, (hd + 1) * ML_QK)

    wins = [[slice(lc * L, (lc + 1) * L) for lc in (step, chunks - 1 - step)]
            for step in range(chunks)]
    gates = [[refs[d][3][:, win[d]] for d in range(N_DIR)] for win in wins]
    cums = [[cum_refs[d][:, win[d]] for d in range(N_DIR)] for win in wins]

    q, kt, v_aug, qk, qc, m_prev = [{} for _ in range(6)]
    for step, win in enumerate(wins):
        gate, cum = gates[step], cums[step]
        for d, hd in heads:
            j = d * ML_HEADS + hd
            c = (step, j)
            q[c] = refs[d][0][win[d], blk(hd)]
            kt[c] = refs[d][1][kblk(hd), win[d]]
            v_aug[c] = jnp.concatenate([refs[d][2][win[d], blk(hd)], ones_blk], axis=1)
            qk[c] = _dot(q[c], jnp.concatenate([kt[c].astype(BF16), k_pad], axis=0))
            qc[c] = _dot(q[c], jnp.concatenate([c_sc[j].astype(BF16), c_pad], axis=0))
        for d, hd in heads:
            j = d * ML_HEADS + hd
            c = (step, j)
            br = cum[d][N_GATE + j:N_GATE + j + 1, :]
            ir = gate[d][j:j + 1, :]
            m_prev[c] = m_sc[j][0:1, 0:1]
            b_end = br[:, L - 1:L] if d == 0 else br[:, 0:1]
            g_row = b_end - br + ir
            m_new = jnp.maximum(b_end + m_prev[c], jnp.max(g_row, axis=1, keepdims=True))
            decay = jnp.exp(b_end + m_prev[c] - m_new)
            kw_t = (kt[c] * jnp.exp(g_row - m_new)).astype(BF16)
            c_sc[j] = decay * c_sc[j] + _dot(kw_t, v_aug[c])
            m_sc[j] = jnp.broadcast_to(m_new, m_sc.shape[1:])

    chains = [(step, d, hd) for step in range(chunks) for d, hd in heads]
    a, mm, mm_b, bc, sv = {}, {}, {}, {}, {}
    for step, d, hd in chains:
        j = d * ML_HEADS + hd
        c = (step, j)
        br = cums[step][d][N_GATE + j:N_GATE + j + 1, :]
        lf = gates[step][d][N_GATE + j:N_GATE + j + 1, :]
        a[c] = jnp.where(mask[d], gates[step][d][j:j + 1, :] - br, -jnp.inf)
        mm[c] = jnp.maximum(m_prev[c], jnp.max(a[c], axis=1, keepdims=True))
        bc[c] = jnp.sum(jnp.where(mask[d], lf, 0.0), axis=1, keepdims=True)
    for step, d, hd in chains:
        c = (step, d * ML_HEADS + hd)
        mm_b[c] = jnp.broadcast_to(mm[c], (L, L))
        smat = qk[c] * jnp.exp(a[c] - mm_b[c])
        sv[c] = _dot(smat.astype(BF16), v_aug[c])
    for step, d, hd in chains:
        c = (step, d * ML_HEADS + hd)
        inter = jnp.exp(m_prev[c] - mm_b[c])
        nv = sv[c] + jnp.concatenate([inter, inter], axis=1) * qc[c]
        den = nv[:, ML_V:ML_V + 1]
        inv = 1.0 / jnp.maximum(jnp.abs(den), jnp.exp(-(bc[c] + mm[c])))
        refs[d][4][wins[step][d], blk(hd)] = (nv[:, :ML_V] * inv).astype(BF16)


def _mlstm(q, kt, v, gate_t, cum_t, chunks=4):
    s = q.shape[0]
    tb = chunks * ML_CHUNK
    nb = s // tb
    ng = 2 * N_GATE
    rows = lambda w, f: pl.BlockSpec((tb, w), lambda c: (f(c), 0))
    cols = lambda h, f, r=0: pl.BlockSpec((h, tb), lambda c: (r, f(c)))
    fwd = lambda c: c
    bwd = lambda c: nb - 1 - c
    hshape = jax.ShapeDtypeStruct((s, ML_OUT), BF16)
    return pl.pallas_call(
        functools.partial(_mlstm_kernel, chunks=chunks), grid=(nb,),
        in_specs=[rows(QK_PAD, fwd), cols(QK_W, fwd), rows(ML_OUT, fwd),
                  cols(ng, fwd), cols(ng, fwd, 0),
                  rows(QK_PAD, bwd), cols(QK_W, bwd), rows(ML_OUT, bwd),
                  cols(ng, bwd), cols(ng, bwd, 1)],
        out_specs=[rows(ML_OUT, fwd), rows(ML_OUT, bwd)],
        out_shape=[hshape, hshape],
        scratch_shapes=[pltpu.VMEM((N_GATE, ML_QK, ML_V + LANES), F32),
                        pltpu.VMEM((N_GATE, 8, LANES), F32)],
        compiler_params=_params(("arbitrary",)),
        name="mlstm",
    )(q, kt, v, gate_t, cum_t, q, kt, v, gate_t, cum_t)


def _merge_kernel(h_ref, att_ref, hf_ref, hb_ref, mlo_ref, gates_ref, hn_ref,
                  wba_ref, wbm_ref, wout_ref, post_ref, o_ref):
    a = lax.dot_general(att_ref[...], wba_ref[...], (((0,), (0,)), ((), ())),
                        preferred_element_type=F32)
    hm = hf_ref[...].astype(F32) + hb_ref[...].astype(F32)
    heads = []
    for hd in range(ML_HEADS):
        blk = slice(hd * ML_V, (hd + 1) * ML_V)
        heads.append(_rms(hm[:, blk], hn_ref[:, blk]))
    hm = jnp.concatenate(heads, axis=1) * mlo_ref[...]
    bm = _dot(hm.astype(BF16), wbm_ref[...])
    g = gates_ref[...]
    mixed = _dot((g[:, :D_MODEL] * a + g[:, D_MODEL:] * bm).astype(BF16), wout_ref[...])
    o_ref[...] = h_ref[...] + _rms(mixed, post_ref[...])


def _merge(h, att, hf, hb, mlo, gates, hn, wba, wbm, wout, post, tm=512):
    s, d = h.shape
    row = lambda w: pl.BlockSpec((tm, w), lambda i: (i, 0))
    return pl.pallas_call(
        _merge_kernel, grid=(s // tm,),
        in_specs=[row(d), pl.BlockSpec((att.shape[0], tm), lambda i: (0, i)),
                  row(ML_OUT), row(ML_OUT), row(ML_OUT),
                  row(2 * d), _const_spec(hn.shape), _const_spec(wba.shape),
                  _const_spec(wbm.shape), _const_spec(wout.shape), _const_spec(post.shape)],
        out_specs=row(d),
        out_shape=jax.ShapeDtypeStruct((s, d), F32),
        compiler_params=_params(("parallel",)),
        name="merge",
    )(h, att, hf, hb, mlo, gates, hn, wba, wbm, wout, post)


def _pad_heads(w, nh, width):
    r = w.shape[0]
    w = w.reshape(r, nh, width)
    return jnp.pad(w, ((0, 0), (0, 0), (0, HEAD_PAD - width))).reshape(r, nh * HEAD_PAD)


def _permute_w_in(w_in):
    d = w_in.shape[0]
    small = jnp.zeros((LANES, d), w_in.dtype)
    small = small.at[0:N_GATE].set(w_in[:, _O_MLI:_O_MLF].T)
    small = small.at[N_GATE:2 * N_GATE].set(w_in[:, _O_MLF:_O_MLO].T)
    small = small.at[ROPE_LO:ROPE_LO + MLA_ROPE].set(w_in[:, _O_KR:_O_MLQ].T)
    main = jnp.concatenate([w_in[:, _O_CQ:_O_KR].astype(BF16), w_in[:, _O_MLQ:_O_MLI].astype(BF16),
                            w_in[:, _O_MLO:_O_END].astype(BF16)], axis=1)
    return main, small.astype(BF16)


def kernel(x, p, positions, ffn1_pre_norm, ffn1_post_norm, ffn1_w_gate, ffn1_w_up, ffn1_w_down, mix_pre_norm, mix_post_norm, w_in, mla_q_norm, mla_kv_norm, mla_w_uq, mla_w_uk, mla_w_uv, ml_conv_w, ml_conv_b, ml_i_bias, ml_f_bias, ml_head_norm, w_branch_mla, w_branch_ml, w_out, ffn2_pre_norm, ffn2_post_norm, ffn2_w_gate, ffn2_w_up, ffn2_w_down, ple_pre_norm, ple_post_norm, ple_w_proj, ple_w_gate):
    depth, batch = p.shape[0], x.shape[0]
    assert depth == 1 and batch == 1, "kernel is specialised to DEPTH == 1, BATCH == 1"
    bf = lambda w: w.astype(BF16)
    h = x[0]
    pos = positions

    inv_freq = ROPE_THETA ** (-jnp.arange(0, MLA_ROPE, 2, dtype=F32) / MLA_ROPE)
    invf = inv_freq[:, None]

    i = 0
    h = _ffn(h, ffn1_pre_norm[i][None], ffn1_post_norm[i][None],
             ffn1_w_gate[i], ffn1_w_up[i], ffn1_w_down[i])

    vone = jnp.zeros((MLA_HEADS, HEAD_PAD), F32).at[:, MLA_V].set(1.0).reshape(-1, 1)
    win_main, win_small = _permute_w_in(w_in[i])
    gate_bias = jnp.concatenate([ml_i_bias[i].reshape(-1), ml_f_bias[i].reshape(-1)])[:, None]
    qt, k, vt, mlqk, mlv, gate_t, cum_t, mlo, gates = _proj(
        h, pos, invf, mix_pre_norm[i][None], win_main, win_small,
        gate_bias, mla_q_norm[i][None], mla_kv_norm[i][None],
        bf(_pad_heads(mla_w_uq[i], MLA_HEADS, MLA_NOPE + MLA_ROPE)).T,
        bf(_pad_heads(mla_w_uk[i], MLA_HEADS, MLA_NOPE)),
        bf(_pad_heads(mla_w_uv[i], MLA_HEADS, MLA_V)).T, vone)

    att = _attn(qt, k, vt)

    conv_w = jnp.pad(ml_conv_w[i], ((0, 8 - CONV_W), (0, 0)))
    conv_b = ml_conv_b[i][None]
    q_conv, kt_conv = _conv(mlqk, conv_w, conv_b)

    hf, hb = _mlstm(q_conv, kt_conv, mlv, gate_t, cum_t)

    h = _merge(h, att, hf, hb, mlo, gates, ml_head_norm[i][None], bf(w_branch_mla[i]),
               bf(w_branch_ml[i]), bf(w_out[i]), mix_post_norm[i][None])

    h = _ffn(h, ffn2_pre_norm[i][None], ffn2_post_norm[i][None],
             ffn2_w_gate[i], ffn2_w_up[i], ffn2_w_down[i],
             ple=(p[i, 0], ple_pre_norm[i][None], ple_post_norm[i][None],
                  bf(ple_w_proj[i]), bf(ple_w_gate[i])))
    return h[None]
```

```python
import functools
import math

import jax
import jax.numpy as jnp
from jax import lax
from jax.experimental import pallas as pl
from jax.experimental.pallas import tpu as pltpu

D_MODEL = 1024
SEQ = 16384
PLE_DIM = 256
D_FF = 2816
EPS = 1e-6
MLA_HEADS = 8
MLA_Q_RANK = 256
MLA_KV_RANK = 256
MLA_NOPE = 64
MLA_ROPE = 32
MLA_V = 64
ROPE_THETA = 10000.0
ML_HEADS = 4
ML_QK = 64
ML_V = 128
ML_CHUNK = 128
CONV_W = 5
N_DIR = 2
MLA_OUT = MLA_HEADS * MLA_V
ML_OUT = ML_HEADS * ML_V

LANES = 128
HEAD_PAD = LANES
ROPE_LO = MLA_NOPE
ROPE_HALF = MLA_ROPE // 2
VMEM_LIMIT = 56 * 1024 * 1024

F32 = jnp.float32
BF16 = jnp.bfloat16
LOG2E = math.log2(math.e)
Q_SCALE = (MLA_NOPE + MLA_ROPE) ** -0.5 * LOG2E

_O_CQ = 0
_O_CKV = _O_CQ + MLA_Q_RANK
_O_KR = _O_CKV + MLA_KV_RANK
_O_MLQ = _O_KR + MLA_ROPE
_O_MLK = _O_MLQ + ML_HEADS * ML_QK
_O_MLV = _O_MLK + ML_HEADS * ML_QK
_O_MLI = _O_MLV + ML_OUT
_O_MLF = _O_MLI + N_DIR * ML_HEADS
_O_MLO = _O_MLF + N_DIR * ML_HEADS
_O_GATES = _O_MLO + ML_OUT
_O_END = _O_GATES + 2 * D_MODEL

N_GATE = N_DIR * ML_HEADS
QK_W = ML_HEADS * ML_QK
QK_PAD = ML_HEADS * HEAD_PAD

_P_CQ = 0
_P_CKV = _P_CQ + MLA_Q_RANK
_P_MLQK = _P_CKV + MLA_KV_RANK
_P_MLV = _P_MLQK + 2 * QK_W
_P_MLO = _P_MLV + ML_OUT
_P_GATES = _P_MLO + ML_OUT
_P_END = _P_GATES + 2 * D_MODEL


def _rms(x, w):
    ms = jnp.mean(x * x, axis=-1, keepdims=True)
    return x * lax.rsqrt(ms + EPS) * w


def _sigmoid(x):
    return 1.0 / (1.0 + jnp.exp(-x))


def _dot(a, b):
    return jnp.dot(a, b, preferred_element_type=F32)


def _const_spec(shape):
    nd = len(shape)
    return pl.BlockSpec(shape, lambda *_: (0,) * nd, pipeline_mode=pl.Buffered(1))


def _params(sem):
    return pltpu.CompilerParams(dimension_semantics=sem, vmem_limit_bytes=VMEM_LIMIT)


def _ffn_body(x, pre, post, wg, wu, wd):
    u = _rms(x, pre).astype(BF16)
    g = _dot(u, wg)
    up = _dot(u, wu)
    hid = (g * _sigmoid(g) * up).astype(BF16)
    y = _dot(hid, wd)
    return x + 0.5 * _rms(y, post)


STAGE_CHUNKS = 8
STAGE_SLOTS = 4


def _stage_weights(w_hbm, w_sc, stage, sem):
    jobs = [(w, sc, st, c) for w, sc, st in zip(w_hbm, w_sc, stage) for c in range(STAGE_CHUNKS)]
    ahead = STAGE_SLOTS - 1

    def copy(i):
        w, _, st, c = jobs[i]
        rows, slot = st.shape[1], i % STAGE_SLOTS
        return pltpu.make_async_copy(w.at[pl.ds(c * rows, rows), :], st.at[slot], sem.at[slot])

    for i in range(min(ahead, len(jobs))):
        copy(i).start()
    for i, (_, sc, st, c) in enumerate(jobs):
        if i + ahead < len(jobs):
            copy(i + ahead).start()
        copy(i).wait()
        rows = st.shape[1]
        sc[c * rows:(c + 1) * rows, :] = st[i % STAGE_SLOTS].astype(BF16)


def _ffn_weights(w_hbm, scratch):
    wg_sc, wu_sc, wd_sc, stage_wide, stage_tall, sem = scratch

    @pl.when(pl.program_id(0) == 0)
    def _():
        _stage_weights(w_hbm, (wg_sc, wu_sc, wd_sc), (stage_wide, stage_wide, stage_tall), sem)

    return wg_sc[...], wu_sc[...], wd_sc[...]


def _ffn_kernel(x_ref, pre_ref, post_ref, wg_hbm, wu_hbm, wd_hbm, o_ref, *scratch):
    o_ref[...] = _ffn_body(x_ref[...], pre_ref[...], post_ref[...],
                           *_ffn_weights((wg_hbm, wu_hbm, wd_hbm), scratch))


def _ffn_ple_kernel(x_ref, pre_ref, post_ref, wg_hbm, wu_hbm, wd_hbm,
                    p_ref, ppre_ref, ppost_ref, wproj_ref, wgate_ref, o_ref, *scratch):
    h = _ffn_body(x_ref[...], pre_ref[...], post_ref[...],
                  *_ffn_weights((wg_hbm, wu_hbm, wd_hbm), scratch))
    e = _dot(p_ref[...].astype(BF16), wproj_ref[...])
    g = _sigmoid(_dot(_rms(h, ppre_ref[...]).astype(BF16), wgate_ref[...]))
    o_ref[...] = h + _rms(g * e, ppost_ref[...])


def _ffn(x, pre, post, wg, wu, wd, ple=None, tm=512):
    s, d = x.shape
    f = wg.shape[1]
    row = pl.BlockSpec((tm, d), lambda i: (i, 0))
    hbm = pl.BlockSpec(memory_space=pl.ANY)
    in_specs = [row, _const_spec((1, d)), _const_spec((1, d)), hbm, hbm, hbm]
    args = [x, pre, post, wg, wu, wd]
    kern = _ffn_kernel
    if ple is not None:
        p, ppre, ppost, wproj, wgate = ple
        in_specs += [pl.BlockSpec((tm, p.shape[1]), lambda i: (i, 0)),
                     _const_spec((1, d)), _const_spec((1, d)),
                     _const_spec(wproj.shape), _const_spec(wgate.shape)]
        args += [p, ppre, ppost, wproj, wgate]
        kern = _ffn_ple_kernel
    return pl.pallas_call(
        kern, grid=(s // tm,), in_specs=in_specs, out_specs=row,
        out_shape=jax.ShapeDtypeStruct((s, d), F32),
        scratch_shapes=[pltpu.VMEM((d, f), BF16), pltpu.VMEM((d, f), BF16), pltpu.VMEM((f, d), BF16),
                        pltpu.VMEM((STAGE_SLOTS, d // STAGE_CHUNKS, f), F32),
                        pltpu.VMEM((STAGE_SLOTS, f // STAGE_CHUNKS, d), F32),
                        pltpu.SemaphoreType.DMA((STAGE_SLOTS,))],
        compiler_params=_params(("arbitrary",)),
        name="ffn" if ple is None else "ffn_ple",
    )(*args)


def _dot_nt(a, b):
    return lax.dot_general(a, b, (((1,), (1,)), ((), ())), preferred_element_type=F32)


def _rope_tables_t(pos_ref, invf_ref):
    ang = pos_ref[...].astype(F32) * invf_ref[...]
    return jnp.cos(ang), jnp.sin(ang)


def _rope_t(x, c, s):
    x1, x2 = x[:ROPE_HALF], x[ROPE_HALF:]
    return x1 * c - x2 * s, x2 * c + x1 * s


def _proj_kernel(h_ref, pos_ref, invf_ref, pre_ref, win_ref, wsm_ref, gbias_ref, qn_ref, kvn_ref,
                 wuqt_ref, wuk_ref, wuvt_ref, vone_ref,
                 qt_ref, k_ref, vt_ref, mlqk_ref, mlv_ref, gt_ref, cum_ref, mlo_ref, gates_ref):
    u = _rms(h_ref[...], pre_ref[...]).astype(BF16)
    zdot = lambda a, b: _dot(u, win_ref[:, a:b])
    c, s = _rope_tables_t(pos_ref, invf_ref)
    rope = slice(ROPE_LO, ROPE_LO + MLA_ROPE)
    small_t = _dot_nt(wsm_ref[...], u)
    z_c = zdot(_P_CQ, _P_MLQK)
    gates_ref[...] = _sigmoid(zdot(_P_GATES, _P_END)).astype(BF16)

    pre = small_t[0:2 * N_GATE, :] + gbias_ref[...]
    is_f = lax.broadcasted_iota(jnp.int32, (2 * N_GATE, 1), 0) >= N_GATE
    gate = jnp.where(is_f, jax.nn.log_sigmoid(pre), pre)
    gt_ref[...] = gate
    r = lax.broadcasted_iota(jnp.int32, (ML_CHUNK, ML_CHUNK), 0)
    cidx = lax.broadcasted_iota(jnp.int32, (ML_CHUNK, ML_CHUNK), 1)
    for d, tri in enumerate((r <= cidx, r >= cidx)):
        for ch in range(gate.shape[1] // ML_CHUNK):
            cs = slice(ch * ML_CHUNK, (ch + 1) * ML_CHUNK)
            cum_ref[d * 2 * N_GATE:(d + 1) * 2 * N_GATE, cs] = jnp.dot(
                gate[:, cs], tri.astype(F32), preferred_element_type=F32,
                precision=lax.Precision.HIGHEST)

    tm = small_t.shape[1]
    k_rope = jnp.concatenate(
        [jnp.zeros((ROPE_LO, tm), F32), *_rope_t(small_t[rope, :], c, s),
         jnp.zeros((HEAD_PAD - ROPE_LO - MLA_ROPE, tm), F32)], axis=0).T

    cq = _rms(z_c[:, :MLA_Q_RANK], qn_ref[...]).astype(BF16)
    ckv = _rms(z_c[:, MLA_Q_RANK:], kvn_ref[...]).astype(BF16)
    qpt = _dot_nt(wuqt_ref[...], cq)
    kp = _dot(ckv, wuk_ref[...])
    vt_ref[...] = (_dot_nt(wuvt_ref[...], ckv) + vone_ref[...]).astype(BF16)
    mlqk_ref[...] = zdot(_P_MLQK, _P_MLV)
    mlv_ref[...] = zdot(_P_MLV, _P_MLO).astype(BF16)
    mlo_ref[...] = _sigmoid(zdot(_P_MLO, _P_GATES)).astype(BF16)
    for hd in range(MLA_HEADS):
        blk = slice(hd * HEAD_PAD, (hd + 1) * HEAD_PAD)
        x = qpt[blk, :]
        r1, r2 = _rope_t(x[rope, :], c, s)
        qt_ref[blk, :] = jnp.concatenate(
            [x[:ROPE_LO] * Q_SCALE, r1 * Q_SCALE, r2 * Q_SCALE, x[ROPE_LO + MLA_ROPE:]],
            axis=0).astype(BF16)
        k_ref[hd] = (kp[:, blk] + k_rope).astype(BF16)


def _proj(h, pos, invf, pre, win, wsm, gbias, qn, kvn, wuqt, wuk, wuvt, vone, tm=512):
    s, d = h.shape
    ng = 2 * N_GATE
    row = lambda w: pl.BlockSpec((tm, w), lambda i: (i, 0))
    col = lambda r: pl.BlockSpec((r, tm), lambda i: (0, i))
    cshape = jax.ShapeDtypeStruct((MLA_HEADS * HEAD_PAD, s), BF16)
    return pl.pallas_call(
        _proj_kernel, grid=(s // tm,),
        in_specs=[row(d), col(1), _const_spec(invf.shape), _const_spec((1, d)),
                  _const_spec(win.shape), _const_spec(wsm.shape), _const_spec(gbias.shape),
                  _const_spec(qn.shape), _const_spec(kvn.shape),
                  _const_spec(wuqt.shape), _const_spec(wuk.shape), _const_spec(wuvt.shape),
                  _const_spec(vone.shape)],
        out_specs=[col(MLA_HEADS * HEAD_PAD),
                   pl.BlockSpec((MLA_HEADS, tm, HEAD_PAD), lambda i: (0, i, 0)),
                   col(MLA_HEADS * HEAD_PAD),
                   row(2 * QK_W), row(ML_OUT), col(ng), col(N_DIR * ng),
                   row(ML_OUT), row(2 * D_MODEL)],
        out_shape=[cshape, jax.ShapeDtypeStruct((MLA_HEADS, s, HEAD_PAD), BF16), cshape,
                   jax.ShapeDtypeStruct((s, 2 * QK_W), F32),
                   jax.ShapeDtypeStruct((s, ML_OUT), BF16),
                   jax.ShapeDtypeStruct((ng, s), F32),
                   jax.ShapeDtypeStruct((N_DIR * ng, s), F32),
                   jax.ShapeDtypeStruct((s, ML_OUT), BF16),
                   jax.ShapeDtypeStruct((s, 2 * D_MODEL), BF16)],
        compiler_params=_params(("parallel",)),
        name="proj",
    )(h, pos, invf, pre, win, wsm, gbias, qn, kvn, wuqt, wuk, wuvt, vone)


BF16_ROWS = 16
V_ROWS = -(-(MLA_V + 1) // BF16_ROWS) * BF16_ROWS


def _attn_kernel(qt_ref, k_ref, vt_ref, o_ref, *scratch, tk, tq, steps):
    n = k_ref.shape[1] // tk
    nq = qt_ref.shape[1] // tq

    def tile(a):
        acc_sc, st0, st1, p0, p1 = scratch[5 * a:5 * a + 5]
        st, pb = (st0, st1), (p0, p1)
        qt = qt_ref[:, a * tq:(a + 1) * tq]

        def accumulate(t_prev, slot, alpha):
            off = pl.multiple_of(t_prev * tk, tk)
            vt = vt_ref[0:V_ROWS, pl.ds(off, tk)]
            acc_sc[...] = alpha * acc_sc[...] + _dot(vt, pb[slot][...])

        def prologue():
            acc_sc[...] = jnp.zeros(acc_sc.shape, F32)
            p1[...] = jnp.zeros(p1.shape, BF16)
            st0[...] = _dot(k_ref[0, 0:tk, :], qt)

        def step(t, slot, m, alpha_prev):
            off = pl.multiple_of(jnp.minimum(t + 1, n - 1) * tk, tk)
            st[1 - slot][...] = _dot(k_ref[0, pl.ds(off, tk), :], qt)
            accumulate(jnp.maximum(t - 1, 0), 1 - slot, alpha_prev)
            x = st[slot][...]
            m_new = jnp.maximum(m, jnp.max(x, axis=0, keepdims=True))
            pb[slot][...] = jnp.exp2(x - m_new).astype(BF16)
            return m_new, jnp.exp2(m - m_new)

        def body(i, carry):
            m, alpha = carry
            for j in range(steps):
                m, alpha = step(i * steps + j, j % 2, m, alpha)
            return m, alpha

        def loop():
            init = (jnp.full((1, tq), -jnp.inf, F32), jnp.ones((1, tq), F32))
            return lax.fori_loop(0, n // steps, body, init)[1]

        def epilogue(alpha):
            accumulate(n - 1, (n - 1) % 2, alpha)
            acc = acc_sc[...]
            o_ref[:, a * tq:(a + 1) * tq] = (acc[0:MLA_V] / acc[MLA_V:MLA_V + 1]).astype(o_ref.dtype)

        return prologue, loop, epilogue

    tiles = [tile(a) for a in range(nq)]
    tiles[0][0]()
    for a in range(nq):
        alpha = tiles[a][1]()
        if a + 1 < nq:
            tiles[a + 1][0]()
        tiles[a][2](alpha)


def _attn(qt, k, vt, tq=512, nq=4, tk=256, steps=32):
    nh, s, w = k.shape
    assert steps % 2 == 0 and (s // tk) % steps == 0
    sub_scratch = [pltpu.VMEM((V_ROWS, tq), F32),
                   pltpu.VMEM((tk, tq), F32), pltpu.VMEM((tk, tq), F32),
                   pltpu.VMEM((tk, tq), BF16), pltpu.VMEM((tk, tq), BF16)]
    return pl.pallas_call(
        functools.partial(_attn_kernel, tk=tk, tq=tq, steps=steps), grid=(nh, s // (nq * tq)),
        in_specs=[pl.BlockSpec((w, nq * tq), lambda h, i: (h, i)),
                  pl.BlockSpec((1, s, w), lambda h, i: (h, 0, 0)),
                  pl.BlockSpec((w, s), lambda h, i: (h, 0))],
        out_specs=pl.BlockSpec((MLA_V, nq * tq), lambda h, i: (h, i)),
        out_shape=jax.ShapeDtypeStruct((nh * MLA_V, s), BF16),
        scratch_shapes=sub_scratch * nq,
        compiler_params=_params(("parallel", "parallel")),
        name="attn",
    )(qt, k, vt)


CONV_HALO = 8


def _conv_kernel(x_ref, prev_ref, next_ref, w_ref, b_ref, q_ref, kt_ref, buf, *, tm):
    i = pl.program_id(0)
    n = pl.num_programs(0)
    buf[CONV_HALO:CONV_HALO + tm, :] = x_ref[...]
    buf[0:CONV_HALO, :] = jnp.where(i > 0, prev_ref[...], 0.0)
    buf[CONV_HALO + tm:, :] = jnp.where(i < n - 1, next_ref[...], 0.0)
    acc = jnp.broadcast_to(b_ref[...], (tm, x_ref.shape[1]))
    for j in range(CONV_W):
        start = CONV_HALO - CONV_W // 2 + j
        acc = acc + w_ref[j:j + 1, :] * buf[start:start + tm, :]
    y = acc * _sigmoid(acc)
    pad = jnp.zeros((tm, HEAD_PAD - ML_QK), F32)
    q_ref[...] = jnp.concatenate(
        [piece for hd in range(ML_HEADS) for piece in (y[:, hd * ML_QK:(hd + 1) * ML_QK], pad)],
        axis=1).astype(BF16)
    kt_ref[...] = (y[:, QK_W:] * ML_QK ** -0.5).T


def _conv(x, w, b, tm=1024):
    s, c = x.shape
    r = tm // CONV_HALO
    nblk = s // CONV_HALO
    return pl.pallas_call(
        functools.partial(_conv_kernel, tm=tm), grid=(s // tm,),
        in_specs=[pl.BlockSpec((tm, c), lambda i: (i, 0)),
                  pl.BlockSpec((CONV_HALO, c), lambda i: (jnp.maximum(i * r - 1, 0), 0)),
                  pl.BlockSpec((CONV_HALO, c), lambda i: (jnp.minimum((i + 1) * r, nblk - 1), 0)),
                  _const_spec(w.shape), _const_spec(b.shape)],
        out_specs=[pl.BlockSpec((tm, QK_PAD), lambda i: (i, 0)),
                   pl.BlockSpec((QK_W, tm), lambda i: (0, i))],
        out_shape=[jax.ShapeDtypeStruct((s, QK_PAD), BF16),
                   jax.ShapeDtypeStruct((QK_W, s), F32)],
        scratch_shapes=[pltpu.VMEM((tm + 2 * CONV_HALO, c), F32)],
        compiler_params=_params(("parallel",)),
        name="conv",
    )(x, x, x, w, b)


def _mlstm_kernel(qf_ref, ktf_ref, vf_ref, gf_ref, cumf_ref, qb_ref, ktb_ref, vb_ref, gb_ref,
                  cumb_ref, hf_ref, hb_ref, c_sc, m_sc, *, chunks):
    L = ML_CHUNK

    @pl.when(pl.program_id(0) == 0)
    def _():
        c_sc[...] = jnp.zeros(c_sc.shape, F32)
        m_sc[...] = jnp.zeros(m_sc.shape, F32)

    row = lax.broadcasted_iota(jnp.int32, (L, L), 0)
    col = lax.broadcasted_iota(jnp.int32, (L, L), 1)
    mask = [col <= row, col >= row]
    ones_blk = jnp.ones((L, LANES), BF16)
    k_pad = jnp.zeros((HEAD_PAD - ML_QK, L), BF16)
    c_pad = jnp.zeros((HEAD_PAD - ML_QK, ML_V + LANES), BF16)
    refs = ((qf_ref, ktf_ref, vf_ref, gf_ref, hf_ref), (qb_ref, ktb_ref, vb_ref, gb_ref, hb_ref))
    cum_refs = (cumf_ref, cumb_ref)
    heads = [(d, hd) for d in range(N_DIR) for hd in range(ML_HEADS)]
    blk = lambda hd: slice(hd * HEAD_PAD, (hd + 1) * HEAD_PAD)
    kblk = lambda hd: slice(hd * ML_QK, (hd + 1) * ML_QK)

    wins = [[slice(lc * L, (lc + 1) * L) for lc in (step, chunks - 1 - step)]
            for step in range(chunks)]
    gates = [[refs[d][3][:, win[d]] for d in range(N_DIR)] for win in wins]
    cums = [[cum_refs[d][:, win[d]] for d in range(N_DIR)] for win in wins]

    q, kt, v_aug, qk, qc, m_prev = [{} for _ in range(6)]
    for step, win in enumerate(wins):
        gate, cum = gates[step], cums[step]
        for d, hd in heads:
            j = d * ML_HEADS + hd
            c = (step, j)
            q[c] = refs[d][0][win[d], blk(hd)]
            kt[c] = refs[d][1][kblk(hd), win[d]]
            v_aug[c] = jnp.concatenate([refs[d][2][win[d], blk(hd)], ones_blk], axis=1)
            qk[c] = _dot(q[c], jnp.concatenate([kt[c].astype(BF16), k_pad], axis=0))
            qc[c] = _dot(q[c], jnp.concatenate([c_sc[j].astype(BF16), c_pad], axis=0))
        for d, hd in heads:
            j = d * ML_HEADS + hd
            c = (step, j)
            br = cum[d][N_GATE + j:N_GATE + j + 1, :]
            ir = gate[d][j:j + 1, :]
            m_prev[c] = m_sc[j][0:1, 0:1]
            b_end = br[:, L - 1:L] if d == 0 else br[:, 0:1]
            g_row = b_end - br + ir
            m_new = jnp.maximum(b_end + m_prev[c], jnp.max(g_row, axis=1, keepdims=True))
            decay = jnp.exp(b_end + m_prev[c] - m_new)
            kw_t = (kt[c] * jnp.exp(g_row - m_new)).astype(BF16)
            c_sc[j] = decay * c_sc[j] + _dot(kw_t, v_aug[c])
            m_sc[j] = jnp.broadcast_to(m_new, m_sc.shape[1:])

    chains = [(step, d, hd) for step in range(chunks) for d, hd in heads]
    a, mm, mm_b, bc, sv = {}, {}, {}, {}, {}
    for step, d, hd in chains:
        j = d * ML_HEADS + hd
        c = (step, j)
        br = cums[step][d][N_GATE + j:N_GATE + j + 1, :]
        lf = gates[step][d][N_GATE + j:N_GATE + j + 1, :]
        a[c] = jnp.where(mask[d], gates[step][d][j:j + 1, :] - br, -jnp.inf)
        mm[c] = jnp.maximum(m_prev[c], jnp.max(a[c], axis=1, keepdims=True))
        bc[c] = jnp.sum(jnp.where(mask[d], lf, 0.0), axis=1, keepdims=True)
    for step, d, hd in chains:
        c = (step, d * ML_HEADS + hd)
        mm_b[c] = jnp.broadcast_to(mm[c], (L, L))
        smat = qk[c] * jnp.exp(a[c] - mm_b[c])
        sv[c] = _dot(smat.astype(BF16), v_aug[c])
    for step, d, hd in chains:
        c = (step, d * ML_HEADS + hd)
        inter = jnp.exp(m_prev[c] - mm_b[c])
        nv = sv[c] + jnp.concatenate([inter, inter], axis=1) * qc[c]
        den = nv[:, ML_V:ML_V + 1]
        inv = 1.0 / jnp.maximum(jnp.abs(den), jnp.exp(-(bc[c] + mm[c])))
        refs[d][4][wins[step][d], blk(hd)] = (nv[:, :ML_V] * inv).astype(BF16)


def _mlstm(q, kt, v, gate_t, cum_t, chunks=4):
    s = q.shape[0]
    tb = chunks * ML_CHUNK
    nb = s // tb
    ng = 2 * N_GATE
    rows = lambda w, f: pl.BlockSpec((tb, w), lambda c: (f(c), 0))
    cols = lambda h, f, r=0: pl.BlockSpec((h, tb), lambda c: (r, f(c)))
    fwd = lambda c: c
    bwd = lambda c: nb - 1 - c
    hshape = jax.ShapeDtypeStruct((s, ML_OUT), BF16)
    return pl.pallas_call(
        functools.partial(_mlstm_kernel, chunks=chunks), grid=(nb,),
        in_specs=[rows(QK_PAD, fwd), cols(QK_W, fwd), rows(ML_OUT, fwd),
                  cols(ng, fwd), cols(ng, fwd, 0),
                  rows(QK_PAD, bwd), cols(QK_W, bwd), rows(ML_OUT, bwd),
                  cols(ng, bwd), cols(ng, bwd, 1)],
        out_specs=[rows(ML_OUT, fwd), rows(ML_OUT, bwd)],
        out_shape=[hshape, hshape],
        scratch_shapes=[pltpu.VMEM((N_GATE, ML_QK, ML_V + LANES), F32),
                        pltpu.VMEM((N_GATE, 8, LANES), F32)],
        compiler_params=_params(("arbitrary",)),
        name="mlstm",
    )(q, kt, v, gate_t, cum_t, q, kt, v, gate_t, cum_t)


def _merge_kernel(h_ref, att_ref, hf_ref, hb_ref, mlo_ref, gates_ref, hn_ref,
                  wba_ref, wbm_ref, wout_ref, post_ref, o_ref):
    a = lax.dot_general(att_ref[...], wba_ref[...], (((0,), (0,)), ((), ())),
                        preferred_element_type=F32)
    hm = hf_ref[...].astype(F32) + hb_ref[...].astype(F32)
    heads = []
    for hd in range(ML_HEADS):
        blk = slice(hd * ML_V, (hd + 1) * ML_V)
        heads.append(_rms(hm[:, blk], hn_ref[:, blk]))
    hm = jnp.concatenate(heads, axis=1) * mlo_ref[...]
    bm = _dot(hm.astype(BF16), wbm_ref[...])
    g = gates_ref[...]
    mixed = _dot((g[:, :D_MODEL] * a + g[:, D_MODEL:] * bm).astype(BF16), wout_ref[...])
    o_ref[...] = h_ref[...] + _rms(mixed, post_ref[...])


def _merge(h, att, hf, hb, mlo, gates, hn, wba, wbm, wout, post, tm=512):
    s, d = h.shape
    row = lambda w: pl.BlockSpec((tm, w), lambda i: (i, 0))
    return pl.pallas_call(
        _merge_kernel, grid=(s // tm,),
        in_specs=[row(d), pl.BlockSpec((att.shape[0], tm), lambda i: (0, i)),
                  row(ML_OUT), row(ML_OUT), row(ML_OUT),
                  row(2 * d), _const_spec(hn.shape), _const_spec(wba.shape),
                  _const_spec(wbm.shape), _const_spec(wout.shape), _const_spec(post.shape)],
        out_specs=row(d),
        out_shape=jax.ShapeDtypeStruct((s, d), F32),
        compiler_params=_params(("parallel",)),
        name="merge",
    )(h, att, hf, hb, mlo, gates, hn, wba, wbm, wout, post)


def _pad_heads(w, nh, width):
    r = w.shape[0]
    w = w.reshape(r, nh, width)
    return jnp.pad(w, ((0, 0), (0, 0), (0, HEAD_PAD - width))).reshape(r, nh * HEAD_PAD)


def _permute_w_in(w_in):
    d = w_in.shape[0]
    small = jnp.zeros((LANES, d), w_in.dtype)
    small = small.at[0:N_GATE].set(w_in[:, _O_MLI:_O_MLF].T)
    small = small.at[N_GATE:2 * N_GATE].set(w_in[:, _O_MLF:_O_MLO].T)
    small = small.at[ROPE_LO:ROPE_LO + MLA_ROPE].set(w_in[:, _O_KR:_O_MLQ].T)
    main = jnp.concatenate([w_in[:, _O_CQ:_O_KR].astype(BF16), w_in[:, _O_MLQ:_O_MLI].astype(BF16),
                            w_in[:, _O_MLO:_O_END].astype(BF16)], axis=1)
    return main, small.astype(BF16)


def kernel(x, p, positions, ffn1_pre_norm, ffn1_post_norm, ffn1_w_gate, ffn1_w_up, ffn1_w_down, mix_pre_norm, mix_post_norm, w_in, mla_q_norm, mla_kv_norm, mla_w_uq, mla_w_uk, mla_w_uv, ml_conv_w, ml_conv_b, ml_i_bias, ml_f_bias, ml_head_norm, w_branch_mla, w_branch_ml, w_out, ffn2_pre_norm, ffn2_post_norm, ffn2_w_gate, ffn2_w_up, ffn2_w_down, ple_pre_norm, ple_post_norm, ple_w_proj, ple_w_gate):
    depth, batch = p.shape[0], x.shape[0]
    assert depth == 1 and batch == 1, "kernel is specialised to DEPTH == 1, BATCH == 1"
    bf = lambda w: w.astype(BF16)
    h = x[0]
    pos = positions

    inv_freq = ROPE_THETA ** (-jnp.arange(0, MLA_ROPE, 2, dtype=F32) / MLA_ROPE)
    invf = inv_freq[:, None]

    i = 0
    h = _ffn(h, ffn1_pre_norm[i][None], ffn1_post_norm[i][None],
             ffn1_w_gate[i], ffn1_w_up[i], ffn1_w_down[i])

    vone = jnp.zeros((MLA_HEADS, HEAD_PAD), F32).at[:, MLA_V].set(1.0).reshape(-1, 1)
    win_main, win_small = _permute_w_in(w_in[i])
    gate_bias = jnp.concatenate([ml_i_bias[i].reshape(-1), ml_f_bias[i].reshape(-1)])[:, None]
    qt, k, vt, mlqk, mlv, gate_t, cum_t, mlo, gates = _proj(
        h, pos, invf, mix_pre_norm[i][None], win_main, win_small,
        gate_bias, mla_q_norm[i][None], mla_kv_norm[i][None],
        bf(_pad_heads(mla_w_uq[i], MLA_HEADS, MLA_NOPE + MLA_ROPE)).T,
        bf(_pad_heads(mla_w_uk[i], MLA_HEADS, MLA_NOPE)),
        bf(_pad_heads(mla_w_uv[i], MLA_HEADS, MLA_V)).T, vone)

    att = _attn(qt, k, vt)

    conv_w = jnp.pad(ml_conv_w[i], ((0, 8 - CONV_W), (0, 0)))
    conv_b = ml_conv_b[i][None]
    q_conv, kt_conv = _conv(mlqk, conv_w, conv_b)

    hf, hb = _mlstm(q_conv, kt_conv, mlv, gate_t, cum_t)

    h = _merge(h, att, hf, hb, mlo, gates, ml_head_norm[i][None], bf(w_branch_mla[i]),
               bf(w_branch_ml[i]), bf(w_out[i]), mix_post_norm[i][None])

    h = _ffn(h, ffn2_pre_norm[i][None], ffn2_post_norm[i][None],
             ffn2_w_gate[i], ffn2_w_up[i], ffn2_w_down[i],
             ple=(p[i, 0], ple_pre_norm[i][None], ple_post_norm[i][None],
                  bf(ple_w_proj[i]), bf(ple_w_gate[i])))
    return h[None]
```

```python
import functools
import math

import jax
import jax.numpy as jnp
from jax import lax
from jax.experimental import pallas as pl
from jax.experimental.pallas import tpu as pltpu

D_MODEL = 1024
SEQ = 16384
PLE_DIM = 256
D_FF = 2816
EPS = 1e-6
MLA_HEADS = 8
MLA_Q_RANK = 256
MLA_KV_RANK = 256
MLA_NOPE = 64
MLA_ROPE = 32
MLA_V = 64
ROPE_THETA = 10000.0
ML_HEADS = 4
ML_QK = 64
ML_V = 128
ML_CHUNK = 128
CONV_W = 5
N_DIR = 2
MLA_OUT = MLA_HEADS * MLA_V
ML_OUT = ML_HEADS * ML_V

LANES = 128
HEAD_PAD = LANES
ROPE_LO = MLA_NOPE
ROPE_HALF = MLA_ROPE // 2
VMEM_LIMIT = 56 * 1024 * 1024

F32 = jnp.float32
BF16 = jnp.bfloat16
LOG2E = math.log2(math.e)
Q_SCALE = (MLA_NOPE + MLA_ROPE) ** -0.5 * LOG2E

_O_CQ = 0
_O_CKV = _O_CQ + MLA_Q_RANK
_O_KR = _O_CKV + MLA_KV_RANK
_O_MLQ = _O_KR + MLA_ROPE
_O_MLK = _O_MLQ + ML_HEADS * ML_QK
_O_MLV = _O_MLK + ML_HEADS * ML_QK
_O_MLI = _O_MLV + ML_OUT
_O_MLF = _O_MLI + N_DIR * ML_HEADS
_O_MLO = _O_MLF + N_DIR * ML_HEADS
_O_GATES = _O_MLO + ML_OUT
_O_END = _O_GATES + 2 * D_MODEL

N_GATE = N_DIR * ML_HEADS
QK_W = ML_HEADS * ML_QK
QK_PAD = ML_HEADS * HEAD_PAD

_P_CQ = 0
_P_CKV = _P_CQ + MLA_Q_RANK
_P_MLQK = _P_CKV + MLA_KV_RANK
_P_MLV = _P_MLQK + 2 * QK_W
_P_MLO = _P_MLV + ML_OUT
_P_GATES = _P_MLO + ML_OUT
_P_END = _P_GATES + 2 * D_MODEL


def _rms(x, w):
    ms = jnp.mean(x * x, axis=-1, keepdims=True)
    return x * lax.rsqrt(ms + EPS) * w


def _sigmoid(x):
    return 1.0 / (1.0 + jnp.exp(-x))


def _dot(a, b):
    return jnp.dot(a, b, preferred_element_type=F32)


def _const_spec(shape):
    nd = len(shape)
    return pl.BlockSpec(shape, lambda *_: (0,) * nd, pipeline_mode=pl.Buffered(1))


def _params(sem):
    return pltpu.CompilerParams(dimension_semantics=sem, vmem_limit_bytes=VMEM_LIMIT)


def _ffn_body(x, pre, post, wg, wu, wd):
    u = _rms(x, pre).astype(BF16)
    g = _dot(u, wg)
    up = _dot(u, wu)
    hid = (g * _sigmoid(g) * up).astype(BF16)
    y = _dot(hid, wd)
    return x + 0.5 * _rms(y, post)


STAGE_CHUNKS = 8
STAGE_SLOTS = 4


def _stage_weights(w_hbm, w_sc, stage, sem):
    jobs = [(w, sc, st, c) for w, sc, st in zip(w_hbm, w_sc, stage) for c in range(STAGE_CHUNKS)]
    ahead = STAGE_SLOTS - 1

    def copy(i):
        w, _, st, c = jobs[i]
        rows, slot = st.shape[1], i % STAGE_SLOTS
        return pltpu.make_async_copy(w.at[pl.ds(c * rows, rows), :], st.at[slot], sem.at[slot])

    for i in range(min(ahead, len(jobs))):
        copy(i).start()
    for i, (_, sc, st, c) in enumerate(jobs):
        if i + ahead < len(jobs):
            copy(i + ahead).start()
        copy(i).wait()
        rows = st.shape[1]
        sc[c * rows:(c + 1) * rows, :] = st[i % STAGE_SLOTS].astype(BF16)


def _ffn_weights(w_hbm, scratch):
    wg_sc, wu_sc, wd_sc, stage_wide, stage_tall, sem = scratch

    @pl.when(pl.program_id(0) == 0)
    def _():
        _stage_weights(w_hbm, (wg_sc, wu_sc, wd_sc), (stage_wide, stage_wide, stage_tall), sem)

    return wg_sc[...], wu_sc[...], wd_sc[...]


def _ffn_kernel(x_ref, pre_ref, post_ref, wg_hbm, wu_hbm, wd_hbm, o_ref, *scratch):
    o_ref[...] = _ffn_body(x_ref[...], pre_ref[...], post_ref[...],
                           *_ffn_weights((wg_hbm, wu_hbm, wd_hbm), scratch))


def _ffn_ple_kernel(x_ref, pre_ref, post_ref, wg_hbm, wu_hbm, wd_hbm,
                    p_ref, ppre_ref, ppost_ref, wproj_ref, wgate_ref, o_ref, *scratch):
    h = _ffn_body(x_ref[...], pre_ref[...], post_ref[...],
                  *_ffn_weights((wg_hbm, wu_hbm, wd_hbm), scratch))
    e = _dot(p_ref[...].astype(BF16), wproj_ref[...])
    g = _sigmoid(_dot(_rms(h, ppre_ref[...]).astype(BF16), wgate_ref[...]))
    o_ref[...] = h + _rms(g * e, ppost_ref[...])


def _ffn(x, pre, post, wg, wu, wd, ple=None, tm=512):
    s, d = x.shape
    f = wg.shape[1]
    row = pl.BlockSpec((tm, d), lambda i: (i, 0))
    hbm = pl.BlockSpec(memory_space=pl.ANY)
    in_specs = [row, _const_spec((1, d)), _const_spec((1, d)), hbm, hbm, hbm]
    args = [x, pre, post, wg, wu, wd]
    kern = _ffn_kernel
    if ple is not None:
        p, ppre, ppost, wproj, wgate = ple
        in_specs += [pl.BlockSpec((tm, p.shape[1]), lambda i: (i, 0)),
                     _const_spec((1, d)), _const_spec((1, d)),
                     _const_spec(wproj.shape), _const_spec(wgate.shape)]
        args += [p, ppre, ppost, wproj, wgate]
        kern = _ffn_ple_kernel
    return pl.pallas_call(
        kern, grid=(s // tm,), in_specs=in_specs, out_specs=row,
        out_shape=jax.ShapeDtypeStruct((s, d), F32),
        scratch_shapes=[pltpu.VMEM((d, f), BF16), pltpu.VMEM((d, f), BF16), pltpu.VMEM((f, d), BF16),
                        pltpu.VMEM((STAGE_SLOTS, d // STAGE_CHUNKS, f), F32),
                        pltpu.VMEM((STAGE_SLOTS, f // STAGE_CHUNKS, d), F32),
                        pltpu.SemaphoreType.DMA((STAGE_SLOTS,))],
        compiler_params=_params(("arbitrary",)),
        name="ffn" if ple is None else "ffn_ple",
    )(*args)


def _dot_nt(a, b):
    return lax.dot_general(a, b, (((1,), (1,)), ((), ())), preferred_element_type=F32)


def _rope_tables_t(pos_ref, invf_ref):
    ang = pos_ref[...].astype(F32) * invf_ref[...]
    return jnp.cos(ang), jnp.sin(ang)


def _rope_t(x, c, s):
    x1, x2 = x[:ROPE_HALF], x[ROPE_HALF:]
    return x1 * c - x2 * s, x2 * c + x1 * s


def _proj_kernel(h_ref, pos_ref, invf_ref, pre_ref, win_ref, wsm_ref, gbias_ref, qn_ref, kvn_ref,
                 wuqt_ref, wuk_ref, wuvt_ref, vone_ref,
                 qt_ref, k_ref, vt_ref, mlqk_ref, mlv_ref, gt_ref, cum_ref, mlo_ref, gates_ref):
    u = _rms(h_ref[...], pre_ref[...]).astype(BF16)
    zdot = lambda a, b: _dot(u, win_ref[:, a:b])
    c, s = _rope_tables_t(pos_ref, invf_ref)
    rope = slice(ROPE_LO, ROPE_LO + MLA_ROPE)
    small_t = _dot_nt(wsm_ref[...], u)
    z_c = zdot(_P_CQ, _P_MLQK)
    gates_ref[...] = _sigmoid(zdot(_P_GATES, _P_END)).astype(BF16)

    pre = small_t[0:2 * N_GATE, :] + gbias_ref[...]
    is_f = lax.broadcasted_iota(jnp.int32, (2 * N_GATE, 1), 0) >= N_GATE
    gate = jnp.where(is_f, jax.nn.log_sigmoid(pre), pre)
    gt_ref[...] = gate
    r = lax.broadcasted_iota(jnp.int32, (ML_CHUNK, ML_CHUNK), 0)
    cidx = lax.broadcasted_iota(jnp.int32, (ML_CHUNK, ML_CHUNK), 1)
    for d, tri in enumerate((r <= cidx, r >= cidx)):
        for ch in range(gate.shape[1] // ML_CHUNK):
            cs = slice(ch * ML_CHUNK, (ch + 1) * ML_CHUNK)
            cum_ref[d * 2 * N_GATE:(d + 1) * 2 * N_GATE, cs] = jnp.dot(
                gate[:, cs], tri.astype(F32), preferred_element_type=F32,
                precision=lax.Precision.HIGHEST)

    tm = small_t.shape[1]
    k_rope = jnp.concatenate(
        [jnp.zeros((ROPE_LO, tm), F32), *_rope_t(small_t[rope, :], c, s),
         jnp.zeros((HEAD_PAD - ROPE_LO - MLA_ROPE, tm), F32)], axis=0).T

    cq = _rms(z_c[:, :MLA_Q_RANK], qn_ref[...]).astype(BF16)
    ckv = _rms(z_c[:, MLA_Q_RANK:], kvn_ref[...]).astype(BF16)
    qpt = _dot_nt(wuqt_ref[...], cq)
    kp = _dot(ckv, wuk_ref[...])
    vt_ref[...] = (_dot_nt(wuvt_ref[...], ckv) + vone_ref[...]).astype(BF16)
    mlqk_ref[...] = zdot(_P_MLQK, _P_MLV)
    mlv_ref[...] = zdot(_P_MLV, _P_MLO).astype(BF16)
    mlo_ref[...] = _sigmoid(zdot(_P_MLO, _P_GATES)).astype(BF16)
    for hd in range(MLA_HEADS):
        blk = slice(hd * HEAD_PAD, (hd + 1) * HEAD_PAD)
        x = qpt[blk, :]
        r1, r2 = _rope_t(x[rope, :], c, s)
        qt_ref[blk, :] = jnp.concatenate(
            [x[:ROPE_LO] * Q_SCALE, r1 * Q_SCALE, r2 * Q_SCALE, x[ROPE_LO + MLA_ROPE:]],
            axis=0).astype(BF16)
        k_ref[hd] = (kp[:, blk] + k_rope).astype(BF16)


def _proj(h, pos, invf, pre, win, wsm, gbias, qn, kvn, wuqt, wuk, wuvt, vone, tm=1024):
    s, d = h.shape
    ng = 2 * N_GATE
    row = lambda w: pl.BlockSpec((tm, w), lambda i: (i, 0))
    col = lambda r: pl.BlockSpec((r, tm), lambda i: (0, i))
    cshape = jax.ShapeDtypeStruct((MLA_HEADS * HEAD_PAD, s), BF16)
    return pl.pallas_call(
        _proj_kernel, grid=(s // tm,),
        in_specs=[row(d), col(1), _const_spec(invf.shape), _const_spec((1, d)),
                  _const_spec(win.shape), _const_spec(wsm.shape), _const_spec(gbias.shape),
                  _const_spec(qn.shape), _const_spec(kvn.shape),
                  _const_spec(wuqt.shape), _const_spec(wuk.shape), _const_spec(wuvt.shape),
                  _const_spec(vone.shape)],
        out_specs=[col(MLA_HEADS * HEAD_PAD),
                   pl.BlockSpec((MLA_HEADS, tm, HEAD_PAD), lambda i: (0, i, 0)),
                   col(MLA_HEADS * HEAD_PAD),
                   row(2 * QK_W), row(ML_OUT), col(ng), col(N_DIR * ng),
                   row(ML_OUT), row(2 * D_MODEL)],
        out_shape=[cshape, jax.ShapeDtypeStruct((MLA_HEADS, s, HEAD_PAD), BF16), cshape,
                   jax.ShapeDtypeStruct((s, 2 * QK_W), F32),
                   jax.ShapeDtypeStruct((s, ML_OUT), BF16),
                   jax.ShapeDtypeStruct((ng, s), F32),
                   jax.ShapeDtypeStruct((N_DIR * ng, s), F32),
                   jax.ShapeDtypeStruct((s, ML_OUT), BF16),
                   jax.ShapeDtypeStruct((s, 2 * D_MODEL), BF16)],
        compiler_params=_params(("parallel",)),
        name="proj",
    )(h, pos, invf, pre, win, wsm, gbias, qn, kvn, wuqt, wuk, wuvt, vone)


BF16_ROWS = 16
V_ROWS = -(-(MLA_V + 1) // BF16_ROWS) * BF16_ROWS


def _attn_kernel(qt_ref, k_ref, vt_ref, o_ref, *scratch, tk, tq, steps):
    n = k_ref.shape[1] // tk
    nq = qt_ref.shape[1] // tq

    def tile(a):
        acc_sc, st0, st1, p0, p1 = scratch[5 * a:5 * a + 5]
        st, pb = (st0, st1), (p0, p1)
        qt = qt_ref[:, a * tq:(a + 1) * tq]

        def accumulate(t_prev, slot, alpha):
            off = pl.multiple_of(t_prev * tk, tk)
            vt = vt_ref[0:V_ROWS, pl.ds(off, tk)]
            acc_sc[...] = alpha * acc_sc[...] + _dot(vt, pb[slot][...])

        def prologue():
            acc_sc[...] = jnp.zeros(acc_sc.shape, F32)
            p1[...] = jnp.zeros(p1.shape, BF16)
            st0[...] = _dot(k_ref[0, 0:tk, :], qt)

        def step(t, slot, m, alpha_prev):
            off = pl.multiple_of(jnp.minimum(t + 1, n - 1) * tk, tk)
            st[1 - slot][...] = _dot(k_ref[0, pl.ds(off, tk), :], qt)
            accumulate(jnp.maximum(t - 1, 0), 1 - slot, alpha_prev)
            x = st[slot][...]
            m_new = jnp.maximum(m, jnp.max(x, axis=0, keepdims=True))
            pb[slot][...] = jnp.exp2(x - m_new).astype(BF16)
            return m_new, jnp.exp2(m - m_new)

        def body(i, carry):
            m, alpha = carry
            for j in range(steps):
                m, alpha = step(i * steps + j, j % 2, m, alpha)
            return m, alpha

        def loop():
            init = (jnp.full((1, tq), -jnp.inf, F32), jnp.ones((1, tq), F32))
            return lax.fori_loop(0, n // steps, body, init)[1]

        def epilogue(alpha):
            accumulate(n - 1, (n - 1) % 2, alpha)
            acc = acc_sc[...]
            o_ref[:, a * tq:(a + 1) * tq] = (acc[0:MLA_V] / acc[MLA_V:MLA_V + 1]).astype(o_ref.dtype)

        return prologue, loop, epilogue

    tiles = [tile(a) for a in range(nq)]
    tiles[0][0]()
    for a in range(nq):
        alpha = tiles[a][1]()
        if a + 1 < nq:
            tiles[a + 1][0]()
        tiles[a][2](alpha)


def _attn(qt, k, vt, tq=512, nq=4, tk=256, steps=32):
    nh, s, w = k.shape
    assert steps % 2 == 0 and (s // tk) % steps == 0
    sub_scratch = [pltpu.VMEM((V_ROWS, tq), F32),
                   pltpu.VMEM((tk, tq), F32), pltpu.VMEM((tk, tq), F32),
                   pltpu.VMEM((tk, tq), BF16), pltpu.VMEM((tk, tq), BF16)]
    return pl.pallas_call(
        functools.partial(_attn_kernel, tk=tk, tq=tq, steps=steps), grid=(nh, s // (nq * tq)),
        in_specs=[pl.BlockSpec((w, nq * tq), lambda h, i: (h, i)),
                  pl.BlockSpec((1, s, w), lambda h, i: (h, 0, 0)),
                  pl.BlockSpec((w, s), lambda h, i: (h, 0))],
        out_specs=pl.BlockSpec((MLA_V, nq * tq), lambda h, i: (h, i)),
        out_shape=jax.ShapeDtypeStruct((nh * MLA_V, s), BF16),
        scratch_shapes=sub_scratch * nq,
        compiler_params=_params(("parallel", "parallel")),
        name="attn",
    )(qt, k, vt)


CONV_HALO = 8


def _conv_kernel(x_ref, prev_ref, next_ref, w_ref, b_ref, q_ref, kt_ref, buf, *, tm):
    i = pl.program_id(0)
    n = pl.num_programs(0)
    buf[CONV_HALO:CONV_HALO + tm, :] = x_ref[...]
    buf[0:CONV_HALO, :] = jnp.where(i > 0, prev_ref[...], 0.0)
    buf[CONV_HALO + tm:, :] = jnp.where(i < n - 1, next_ref[...], 0.0)
    acc = jnp.broadcast_to(b_ref[...], (tm, x_ref.shape[1]))
    for j in range(CONV_W):
        start = CONV_HALO - CONV_W // 2 + j
        acc = acc + w_ref[j:j + 1, :] * buf[start:start + tm, :]
    y = acc * _sigmoid(acc)
    pad = jnp.zeros((tm, HEAD_PAD - ML_QK), F32)
    q_ref[...] = jnp.concatenate(
        [piece for hd in range(ML_HEADS) for piece in (y[:, hd * ML_QK:(hd + 1) * ML_QK], pad)],
        axis=1).astype(BF16)
    kt_ref[...] = (y[:, QK_W:] * ML_QK ** -0.5).T


def _conv(x, w, b, tm=1024):
    s, c = x.shape
    r = tm // CONV_HALO
    nblk = s // CONV_HALO
    return pl.pallas_call(
        functools.partial(_conv_kernel, tm=tm), grid=(s // tm,),
        in_specs=[pl.BlockSpec((tm, c), lambda i: (i, 0)),
                  pl.BlockSpec((CONV_HALO, c), lambda i: (jnp.maximum(i * r - 1, 0), 0)),
                  pl.BlockSpec((CONV_HALO, c), lambda i: (jnp.minimum((i + 1) * r, nblk - 1), 0)),
                  _const_spec(w.shape), _const_spec(b.shape)],
        out_specs=[pl.BlockSpec((tm, QK_PAD), lambda i: (i, 0)),
                   pl.BlockSpec((QK_W, tm), lambda i: (0, i))],
        out_shape=[jax.ShapeDtypeStruct((s, QK_PAD), BF16),
                   jax.ShapeDtypeStruct((QK_W, s), F32)],
        scratch_shapes=[pltpu.VMEM((tm + 2 * CONV_HALO, c), F32)],
        compiler_params=_params(("parallel",)),
        name="conv",
    )(x, x, x, w, b)


def _mlstm_kernel(qf_ref, ktf_ref, vf_ref, gf_ref, cumf_ref, qb_ref, ktb_ref, vb_ref, gb_ref,
                  cumb_ref, hf_ref, hb_ref, c_sc, m_sc, *, chunks):
    L = ML_CHUNK

    @pl.when(pl.program_id(0) == 0)
    def _():
        c_sc[...] = jnp.zeros(c_sc.shape, F32)
        m_sc[...] = jnp.zeros(m_sc.shape, F32)

    row = lax.broadcasted_iota(jnp.int32, (L, L), 0)
    col = lax.broadcasted_iota(jnp.int32, (L, L), 1)
    mask = [col <= row, col >= row]
    ones_blk = jnp.ones((L, LANES), BF16)
    k_pad = jnp.zeros((HEAD_PAD - ML_QK, L), BF16)
    c_pad = jnp.zeros((HEAD_PAD - ML_QK, ML_V + LANES), BF16)
    refs = ((qf_ref, ktf_ref, vf_ref, gf_ref, hf_ref), (qb_ref, ktb_ref, vb_ref, gb_ref, hb_ref))
    cum_refs = (cumf_ref, cumb_ref)
    heads = [(d, hd) for d in range(N_DIR) for hd in range(ML_HEADS)]
    blk = lambda hd: slice(hd * HEAD_PAD, (hd + 1) * HEAD_PAD)
    kblk = lambda hd: slice(hd * ML_QK, (hd + 1) * ML_QK)

    wins = [[slice(lc * L, (lc + 1) * L) for lc in (step, chunks - 1 - step)]
            for step in range(chunks)]
    gates = [[refs[d][3][:, win[d]] for d in range(N_DIR)] for win in wins]
    cums = [[cum_refs[d][:, win[d]] for d in range(N_DIR)] for win in wins]

    q, kt, v_aug, qk, qc, m_prev = [{} for _ in range(6)]
    for step, win in enumerate(wins):
        gate, cum = gates[step], cums[step]
        for d, hd in heads:
            j = d * ML_HEADS + hd
            c = (step, j)
            q[c] = refs[d][0][win[d], blk(hd)]
            kt[c] = refs[d][1][kblk(hd), win[d]]
            v_aug[c] = jnp.concatenate([refs[d][2][win[d], blk(hd)], ones_blk], axis=1)
            qk[c] = _dot(q[c], jnp.concatenate([kt[c].astype(BF16), k_pad], axis=0))
            qc[c] = _dot(q[c], jnp.concatenate([c_sc[j].astype(BF16), c_pad], axis=0))
        for d, hd in heads:
            j = d * ML_HEADS + hd
            c = (step, j)
            br = cum[d][N_GATE + j:N_GATE + j + 1, :]
            ir = gate[d][j:j + 1, :]
            m_prev[c] = m_sc[j][0:1, 0:1]
            b_end = br[:, L - 1:L] if d == 0 else br[:, 0:1]
            g_row = b_end - br + ir
            m_new = jnp.maximum(b_end + m_prev[c], jnp.max(g_row, axis=1, keepdims=True))
            decay = jnp.exp(b_end + m_prev[c] - m_new)
            kw_t = (kt[c] * jnp.exp(g_row - m_new)).astype(BF16)
            c_sc[j] = decay * c_sc[j] + _dot(kw_t, v_aug[c])
            m_sc[j] = jnp.broadcast_to(m_new, m_sc.shape[1:])

    chains = [(step, d, hd) for step in range(chunks) for d, hd in heads]
    a, mm, mm_b, bc, sv = {}, {}, {}, {}, {}
    for step, d, hd in chains:
        j = d * ML_HEADS + hd
        c = (step, j)
        br = cums[step][d][N_GATE + j:N_GATE + j + 1, :]
        lf = gates[step][d][N_GATE + j:N_GATE + j + 1, :]
        a[c] = jnp.where(mask[d], gates[step][d][j:j + 1, :] - br, -jnp.inf)
        mm[c] = jnp.maximum(m_prev[c], jnp.max(a[c], axis=1, keepdims=True))
        bc[c] = jnp.sum(jnp.where(mask[d], lf, 0.0), axis=1, keepdims=True)
    for step, d, hd in chains:
        c = (step, d * ML_HEADS + hd)
        mm_b[c] = jnp.broadcast_to(mm[c], (L, L))
        smat = qk[c] * jnp.exp(a[c] - mm_b[c])
        sv[c] = _dot(smat.astype(BF16), v_aug[c])
    for step, d, hd in chains:
        c = (step, d * ML_HEADS + hd)
        inter = jnp.exp(m_prev[c] - mm_b[c])
        nv = sv[c] + jnp.concatenate([inter, inter], axis=1) * qc[c]
        den = nv[:, ML_V:ML_V + 1]
        inv = 1.0 / jnp.maximum(jnp.abs(den), jnp.exp(-(bc[c] + mm[c])))
        refs[d][4][wins[step][d], blk(hd)] = (nv[:, :ML_V] * inv).astype(BF16)


def _mlstm(q, kt, v, gate_t, cum_t, chunks=4):
    s = q.shape[0]
    tb = chunks * ML_CHUNK
    nb = s // tb
    ng = 2 * N_GATE
    rows = lambda w, f: pl.BlockSpec((tb, w), lambda c: (f(c), 0))
    cols = lambda h, f, r=0: pl.BlockSpec((h, tb), lambda c: (r, f(c)))
    fwd = lambda c: c
    bwd = lambda c: nb - 1 - c
    hshape = jax.ShapeDtypeStruct((s, ML_OUT), BF16)
    return pl.pallas_call(
        functools.partial(_mlstm_kernel, chunks=chunks), grid=(nb,),
        in_specs=[rows(QK_PAD, fwd), cols(QK_W, fwd), rows(ML_OUT, fwd),
                  cols(ng, fwd), cols(ng, fwd, 0),
                  rows(QK_PAD, bwd), cols(QK_W, bwd), rows(ML_OUT, bwd),
                  cols(ng, bwd), cols(ng, bwd, 1)],
        out_specs=[rows(ML_OUT, fwd), rows(ML_OUT, bwd)],
        out_shape=[hshape, hshape],
        scratch_shapes=[pltpu.VMEM((N_GATE, ML_QK, ML_V + LANES), F32),
                        pltpu.VMEM((N_GATE, 8, LANES), F32)],
        compiler_params=_params(("arbitrary",)),
        name="mlstm",
    )(q, kt, v, gate_t, cum_t, q, kt, v, gate_t, cum_t)


def _merge_kernel(h_ref, att_ref, hf_ref, hb_ref, mlo_ref, gates_ref, hn_ref,
                  wba_ref, wbm_ref, wout_ref, post_ref, o_ref):
    a = lax.dot_general(att_ref[...], wba_ref[...], (((0,), (0,)), ((), ())),
                        preferred_element_type=F32)
    hm = hf_ref[...].astype(F32) + hb_ref[...].astype(F32)
    heads = []
    for hd in range(ML_HEADS):
        blk = slice(hd * ML_V, (hd + 1) * ML_V)
        heads.append(_rms(hm[:, blk], hn_ref[:, blk]))
    hm = jnp.concatenate(heads, axis=1) * mlo_ref[...]
    bm = _dot(hm.astype(BF16), wbm_ref[...])
    g = gates_ref[...]
    mixed = _dot((g[:, :D_MODEL] * a + g[:, D_MODEL:] * bm).astype(BF16), wout_ref[...])
    o_ref[...] = h_ref[...] + _rms(mixed, post_ref[...])


def _merge(h, att, hf, hb, mlo, gates, hn, wba, wbm, wout, post, tm=1024):
    s, d = h.shape
    row = lambda w: pl.BlockSpec((tm, w), lambda i: (i, 0))
    return pl.pallas_call(
        _merge_kernel, grid=(s // tm,),
        in_specs=[row(d), pl.BlockSpec((att.shape[0], tm), lambda i: (0, i)),
                  row(ML_OUT), row(ML_OUT), row(ML_OUT),
                  row(2 * d), _const_spec(hn.shape), _const_spec(wba.shape),
                  _const_spec(wbm.shape), _const_spec(wout.shape), _const_spec(post.shape)],
        out_specs=row(d),
        out_shape=jax.ShapeDtypeStruct((s, d), F32),
        compiler_params=_params(("parallel",)),
        name="merge",
    )(h, att, hf, hb, mlo, gates, hn, wba, wbm, wout, post)


def _pad_heads(w, nh, width):
    r = w.shape[0]
    w = w.reshape(r, nh, width)
    return jnp.pad(w, ((0, 0), (0, 0), (0, HEAD_PAD - width))).reshape(r, nh * HEAD_PAD)


def _permute_w_in(w_in):
    d = w_in.shape[0]
    small = jnp.zeros((LANES, d), w_in.dtype)
    small = small.at[0:N_GATE].set(w_in[:, _O_MLI:_O_MLF].T)
    small = small.at[N_GATE:2 * N_GATE].set(w_in[:, _O_MLF:_O_MLO].T)
    small = small.at[ROPE_LO:ROPE_LO + MLA_ROPE].set(w_in[:, _O_KR:_O_MLQ].T)
    main = jnp.concatenate([w_in[:, _O_CQ:_O_KR].astype(BF16), w_in[:, _O_MLQ:_O_MLI].astype(BF16),
                            w_in[:, _O_MLO:_O_END].astype(BF16)], axis=1)
    return main, small.astype(BF16)


def kernel(x, p, positions, ffn1_pre_norm, ffn1_post_norm, ffn1_w_gate, ffn1_w_up, ffn1_w_down, mix_pre_norm, mix_post_norm, w_in, mla_q_norm, mla_kv_norm, mla_w_uq, mla_w_uk, mla_w_uv, ml_conv_w, ml_conv_b, ml_i_bias, ml_f_bias, ml_head_norm, w_branch_mla, w_branch_ml, w_out, ffn2_pre_norm, ffn2_post_norm, ffn2_w_gate, ffn2_w_up, ffn2_w_down, ple_pre_norm, ple_post_norm, ple_w_proj, ple_w_gate):
    depth, batch = p.shape[0], x.shape[0]
    assert depth == 1 and batch == 1, "kernel is specialised to DEPTH == 1, BATCH == 1"
    bf = lambda w: w.astype(BF16)
    h = x[0]
    pos = positions

    inv_freq = ROPE_THETA ** (-jnp.arange(0, MLA_ROPE, 2, dtype=F32) / MLA_ROPE)
    invf = inv_freq[:, None]

    i = 0
    h = _ffn(h, ffn1_pre_norm[i][None], ffn1_post_norm[i][None],
             ffn1_w_gate[i], ffn1_w_up[i], ffn1_w_down[i])

    vone = jnp.zeros((MLA_HEADS, HEAD_PAD), F32).at[:, MLA_V].set(1.0).reshape(-1, 1)
    win_main, win_small = _permute_w_in(w_in[i])
    gate_bias = jnp.concatenate([ml_i_bias[i].reshape(-1), ml_f_bias[i].reshape(-1)])[:, None]
    qt, k, vt, mlqk, mlv, gate_t, cum_t, mlo, gates = _proj(
        h, pos, invf, mix_pre_norm[i][None], win_main, win_small,
        gate_bias, mla_q_norm[i][None], mla_kv_norm[i][None],
        bf(_pad_heads(mla_w_uq[i], MLA_HEADS, MLA_NOPE + MLA_ROPE)).T,
        bf(_pad_heads(mla_w_uk[i], MLA_HEADS, MLA_NOPE)),
        bf(_pad_heads(mla_w_uv[i], MLA_HEADS, MLA_V)).T, vone)

    att = _attn(qt, k, vt)

    conv_w = jnp.pad(ml_conv_w[i], ((0, 8 - CONV_W), (0, 0)))
    conv_b = ml_conv_b[i][None]
    q_conv, kt_conv = _conv(mlqk, conv_w, conv_b)

    hf, hb = _mlstm(q_conv, kt_conv, mlv, gate_t, cum_t)

    h = _merge(h, att, hf, hb, mlo, gates, ml_head_norm[i][None], bf(w_branch_mla[i]),
               bf(w_branch_ml[i]), bf(w_out[i]), mix_post_norm[i][None])

    h = _ffn(h, ffn2_pre_norm[i][None], ffn2_post_norm[i][None],
             ffn2_w_gate[i], ffn2_w_up[i], ffn2_w_down[i],
             ple=(p[i, 0], ple_pre_norm[i][None], ple_post_norm[i][None],
                  bf(ple_w_proj[i]), bf(ple_w_gate[i])))
    return h[None]
```

```python
import functools
import math

import jax
import jax.numpy as jnp
from jax import lax
from jax.experimental import pallas as pl
from jax.experimental.pallas import tpu as pltpu

D_MODEL = 1024
EPS = 1e-6
MLA_HEADS = 8
MLA_Q_RANK = 256
MLA_KV_RANK = 256
MLA_NOPE = 64
MLA_ROPE = 32
MLA_V = 64
ROPE_THETA = 10000.0
ML_HEADS = 4
ML_QK = 64
ML_V = 128
ML_CHUNK = 128
CONV_W = 5
N_DIR = 2
MLA_OUT = MLA_HEADS * MLA_V
ML_OUT = ML_HEADS * ML_V

LANES = 128
HEAD_PAD = LANES
ROPE_LO = MLA_NOPE
ROPE_HALF = MLA_ROPE // 2
VMEM_LIMIT = 56 * 1024 * 1024

F32 = jnp.float32
BF16 = jnp.bfloat16
LOG2E = math.log2(math.e)
Q_SCALE = (MLA_NOPE + MLA_ROPE) ** -0.5 * LOG2E

_O_CQ = 0
_O_CKV = _O_CQ + MLA_Q_RANK
_O_KR = _O_CKV + MLA_KV_RANK
_O_MLQ = _O_KR + MLA_ROPE
_O_MLK = _O_MLQ + ML_HEADS * ML_QK
_O_MLV = _O_MLK + ML_HEADS * ML_QK
_O_MLI = _O_MLV + ML_OUT
_O_MLF = _O_MLI + N_DIR * ML_HEADS
_O_MLO = _O_MLF + N_DIR * ML_HEADS
_O_GATES = _O_MLO + ML_OUT
_O_END = _O_GATES + 2 * D_MODEL

N_GATE = N_DIR * ML_HEADS
QK_W = ML_HEADS * ML_QK
QK_PAD = ML_HEADS * HEAD_PAD

_P_CQ = 0
_P_CKV = _P_CQ + MLA_Q_RANK
_P_MLQK = _P_CKV + MLA_KV_RANK
_P_MLV = _P_MLQK + 2 * QK_W
_P_MLO = _P_MLV + ML_OUT
_P_GATES = _P_MLO + ML_OUT
_P_END = _P_GATES + 2 * D_MODEL


def _rms(x, w):
    ms = jnp.mean(x * x, axis=-1, keepdims=True)
    return x * lax.rsqrt(ms + EPS) * w


def _sigmoid(x):
    return 1.0 / (1.0 + jnp.exp(-x))


def _dot(a, b):
    return jnp.dot(a, b, preferred_element_type=F32)


def _const_spec(shape):
    nd = len(shape)
    return pl.BlockSpec(shape, lambda *_: (0,) * nd, pipeline_mode=pl.Buffered(1))


def _params(sem):
    return pltpu.CompilerParams(dimension_semantics=sem, vmem_limit_bytes=VMEM_LIMIT)


def _ffn_body(x, pre, post, wg, wu, wd):
    u = _rms(x, pre).astype(BF16)
    g = _dot(u, wg)
    up = _dot(u, wu)
    hid = (g * _sigmoid(g) * up).astype(BF16)
    y = _dot(hid, wd)
    return x + 0.5 * _rms(y, post)


STAGE_CHUNKS = 8
STAGE_SLOTS = 4


def _stage_weights(w_hbm, w_sc, stage, sem):
    jobs = [(w, sc, st, c) for w, sc, st in zip(w_hbm, w_sc, stage) for c in range(STAGE_CHUNKS)]
    ahead = STAGE_SLOTS - 1

    def copy(i):
        w, _, st, c = jobs[i]
        rows, slot = st.shape[1], i % STAGE_SLOTS
        return pltpu.make_async_copy(w.at[pl.ds(c * rows, rows), :], st.at[slot], sem.at[slot])

    for i in range(min(ahead, len(jobs))):
        copy(i).start()
    for i, (_, sc, st, c) in enumerate(jobs):
        if i + ahead < len(jobs):
            copy(i + ahead).start()
        copy(i).wait()
        rows = st.shape[1]
        sc[c * rows:(c + 1) * rows, :] = st[i % STAGE_SLOTS].astype(BF16)


def _ffn_weights(w_hbm, scratch):
    wg_sc, wu_sc, wd_sc, stage_wide, stage_tall, sem = scratch

    @pl.when(pl.program_id(0) == 0)
    def _():
        _stage_weights(w_hbm, (wg_sc, wu_sc, wd_sc), (stage_wide, stage_wide, stage_tall), sem)

    return wg_sc[...], wu_sc[...], wd_sc[...]


def _ffn_kernel(x_ref, pre_ref, post_ref, wg_hbm, wu_hbm, wd_hbm, o_ref, *scratch):
    o_ref[...] = _ffn_body(x_ref[...], pre_ref[...], post_ref[...],
                           *_ffn_weights((wg_hbm, wu_hbm, wd_hbm), scratch))


def _ffn_ple_kernel(x_ref, pre_ref, post_ref, wg_hbm, wu_hbm, wd_hbm,
                    p_ref, ppre_ref, ppost_ref, wproj_ref, wgate_ref, o_ref, *scratch):
    h = _ffn_body(x_ref[...], pre_ref[...], post_ref[...],
                  *_ffn_weights((wg_hbm, wu_hbm, wd_hbm), scratch))
    e = _dot(p_ref[...].astype(BF16), wproj_ref[...])
    g = _sigmoid(_dot(_rms(h, ppre_ref[...]).astype(BF16), wgate_ref[...]))
    o_ref[...] = h + _rms(g * e, ppost_ref[...])


def _ffn(x, pre, post, wg, wu, wd, ple=None, tm=512):
    s, d = x.shape
    f = wg.shape[1]
    row = pl.BlockSpec((tm, d), lambda i: (i, 0))
    hbm = pl.BlockSpec(memory_space=pl.ANY)
    in_specs = [row, _const_spec((1, d)), _const_spec((1, d)), hbm, hbm, hbm]
    args = [x, pre, post, wg, wu, wd]
    kern = _ffn_kernel
    if ple is not None:
        p, ppre, ppost, wproj, wgate = ple
        in_specs += [pl.BlockSpec((tm, p.shape[1]), lambda i: (i, 0)),
                     _const_spec((1, d)), _const_spec((1, d)),
                     _const_spec(wproj.shape), _const_spec(wgate.shape)]
        args += [p, ppre, ppost, wproj, wgate]
        kern = _ffn_ple_kernel
    return pl.pallas_call(
        kern, grid=(s // tm,), in_specs=in_specs, out_specs=row,
        out_shape=jax.ShapeDtypeStruct((s, d), F32),
        scratch_shapes=[pltpu.VMEM((d, f), BF16), pltpu.VMEM((d, f), BF16), pltpu.VMEM((f, d), BF16),
                        pltpu.VMEM((STAGE_SLOTS, d // STAGE_CHUNKS, f), F32),
                        pltpu.VMEM((STAGE_SLOTS, f // STAGE_CHUNKS, d), F32),
                        pltpu.SemaphoreType.DMA((STAGE_SLOTS,))],
        compiler_params=_params(("arbitrary",)),
        name="ffn" if ple is None else "ffn_ple",
    )(*args)


def _dot_nt(a, b):
    return lax.dot_general(a, b, (((1,), (1,)), ((), ())), preferred_element_type=F32)


def _rope_tables_t(pos_ref, invf_ref):
    ang = pos_ref[...].astype(F32) * invf_ref[...]
    return jnp.cos(ang), jnp.sin(ang)


def _rope_t(x, c, s):
    x1, x2 = x[:ROPE_HALF], x[ROPE_HALF:]
    return x1 * c - x2 * s, x2 * c + x1 * s


def _proj_kernel(h_ref, pos_ref, invf_ref, pre_ref, win_ref, wsm_ref, gbias_ref, qn_ref, kvn_ref,
                 wuqt_ref, wuk_ref, wuvt_ref, vone_ref,
                 qt_ref, k_ref, vt_ref, mlqk_ref, mlv_ref, gt_ref, cum_ref, mlo_ref, gates_ref):
    u = _rms(h_ref[...], pre_ref[...]).astype(BF16)
    zdot = lambda a, b: _dot(u, win_ref[:, a:b])
    c, s = _rope_tables_t(pos_ref, invf_ref)
    rope = slice(ROPE_LO, ROPE_LO + MLA_ROPE)
    small_t = _dot_nt(wsm_ref[...], u)
    z_c = zdot(_P_CQ, _P_MLQK)
    gates_ref[...] = _sigmoid(zdot(_P_GATES, _P_END)).astype(BF16)

    pre = small_t[0:2 * N_GATE, :] + gbias_ref[...]
    is_f = lax.broadcasted_iota(jnp.int32, (2 * N_GATE, 1), 0) >= N_GATE
    gate = jnp.where(is_f, jax.nn.log_sigmoid(pre), pre)
    gt_ref[...] = gate
    r = lax.broadcasted_iota(jnp.int32, (ML_CHUNK, ML_CHUNK), 0)
    cidx = lax.broadcasted_iota(jnp.int32, (ML_CHUNK, ML_CHUNK), 1)
    for d, tri in enumerate((r <= cidx, r >= cidx)):
        for ch in range(gate.shape[1] // ML_CHUNK):
            cs = slice(ch * ML_CHUNK, (ch + 1) * ML_CHUNK)
            cum_ref[d * 2 * N_GATE:(d + 1) * 2 * N_GATE, cs] = jnp.dot(
                gate[:, cs], tri.astype(F32), preferred_element_type=F32,
                precision=lax.Precision.HIGHEST)

    tm = small_t.shape[1]
    k_rope = jnp.concatenate(
        [jnp.zeros((ROPE_LO, tm), F32), *_rope_t(small_t[rope, :], c, s),
         jnp.zeros((HEAD_PAD - ROPE_LO - MLA_ROPE, tm), F32)], axis=0).T

    cq = _rms(z_c[:, :MLA_Q_RANK], qn_ref[...]).astype(BF16)
    ckv = _rms(z_c[:, MLA_Q_RANK:], kvn_ref[...]).astype(BF16)
    qpt = _dot_nt(wuqt_ref[...], cq)
    kp = _dot(ckv, wuk_ref[...])
    vt_ref[...] = (_dot_nt(wuvt_ref[...], ckv) + vone_ref[...]).astype(BF16)
    mlqk_ref[...] = zdot(_P_MLQK, _P_MLV)
    mlv_ref[...] = zdot(_P_MLV, _P_MLO).astype(BF16)
    mlo_ref[...] = _sigmoid(zdot(_P_MLO, _P_GATES)).astype(BF16)
    for hd in range(MLA_HEADS):
        blk = slice(hd * HEAD_PAD, (hd + 1) * HEAD_PAD)
        x = qpt[blk, :]
        r1, r2 = _rope_t(x[rope, :], c, s)
        qt_ref[blk, :] = jnp.concatenate(
            [x[:ROPE_LO] * Q_SCALE, r1 * Q_SCALE, r2 * Q_SCALE, x[ROPE_LO + MLA_ROPE:]],
            axis=0).astype(BF16)
        k_ref[hd] = (kp[:, blk] + k_rope).astype(BF16)


def _proj(h, pos, invf, pre, win, wsm, gbias, qn, kvn, wuqt, wuk, wuvt, vone, tm=1024):
    s, d = h.shape
    ng = 2 * N_GATE
    row = lambda w: pl.BlockSpec((tm, w), lambda i: (i, 0))
    col = lambda r: pl.BlockSpec((r, tm), lambda i: (0, i))
    cshape = jax.ShapeDtypeStruct((MLA_HEADS * HEAD_PAD, s), BF16)
    return pl.pallas_call(
        _proj_kernel, grid=(s // tm,),
        in_specs=[row(d), col(1), _const_spec(invf.shape), _const_spec((1, d)),
                  _const_spec(win.shape), _const_spec(wsm.shape), _const_spec(gbias.shape),
                  _const_spec(qn.shape), _const_spec(kvn.shape),
                  _const_spec(wuqt.shape), _const_spec(wuk.shape), _const_spec(wuvt.shape),
                  _const_spec(vone.shape)],
        out_specs=[col(MLA_HEADS * HEAD_PAD),
                   pl.BlockSpec((MLA_HEADS, tm, HEAD_PAD), lambda i: (0, i, 0)),
                   col(MLA_HEADS * HEAD_PAD),
                   row(2 * QK_W), row(ML_OUT), col(ng), col(N_DIR * ng),
                   row(ML_OUT), row(2 * D_MODEL)],
        out_shape=[cshape, jax.ShapeDtypeStruct((MLA_HEADS, s, HEAD_PAD), BF16), cshape,
                   jax.ShapeDtypeStruct((s, 2 * QK_W), F32),
                   jax.ShapeDtypeStruct((s, ML_OUT), BF16),
                   jax.ShapeDtypeStruct((ng, s), F32),
                   jax.ShapeDtypeStruct((N_DIR * ng, s), F32),
                   jax.ShapeDtypeStruct((s, ML_OUT), BF16),
                   jax.ShapeDtypeStruct((s, 2 * D_MODEL), BF16)],
        compiler_params=_params(("parallel",)),
        name="proj",
    )(h, pos, invf, pre, win, wsm, gbias, qn, kvn, wuqt, wuk, wuvt, vone)


BF16_ROWS = 16
V_ROWS = -(-(MLA_V + 1) // BF16_ROWS) * BF16_ROWS


def _attn_kernel(qt_ref, k_ref, vt_ref, o_ref, *scratch, tk, tq, steps):
    n = k_ref.shape[1] // tk
    nq = qt_ref.shape[1] // tq

    def tile(a):
        acc_sc, st0, st1, p0, p1 = scratch[5 * a:5 * a + 5]
        st, pb = (st0, st1), (p0, p1)
        qt = qt_ref[:, a * tq:(a + 1) * tq]

        def accumulate(t_prev, slot, alpha):
            off = pl.multiple_of(t_prev * tk, tk)
            vt = vt_ref[0:V_ROWS, pl.ds(off, tk)]
            acc_sc[...] = alpha * acc_sc[...] + _dot(vt, pb[slot][...])

        def prologue():
            acc_sc[...] = jnp.zeros(acc_sc.shape, F32)
            p1[...] = jnp.zeros(p1.shape, BF16)
            st0[...] = _dot(k_ref[0, 0:tk, :], qt)

        def step(t, slot, m, alpha_prev):
            off = pl.multiple_of(jnp.minimum(t + 1, n - 1) * tk, tk)
            st[1 - slot][...] = _dot(k_ref[0, pl.ds(off, tk), :], qt)
            accumulate(jnp.maximum(t - 1, 0), 1 - slot, alpha_prev)
            x = st[slot][...]
            m_new = jnp.maximum(m, jnp.max(x, axis=0, keepdims=True))
            pb[slot][...] = jnp.exp2(x - m_new).astype(BF16)
            return m_new, jnp.exp2(m - m_new)

        def body(i, carry):
            m, alpha = carry
            for j in range(steps):
                m, alpha = step(i * steps + j, j % 2, m, alpha)
            return m, alpha

        def loop():
            init = (jnp.full((1, tq), -jnp.inf, F32), jnp.ones((1, tq), F32))
            return lax.fori_loop(0, n // steps, body, init)[1]

        def epilogue(alpha):
            accumulate(n - 1, (n - 1) % 2, alpha)
            acc = acc_sc[...]
            o_ref[:, a * tq:(a + 1) * tq] = (acc[0:MLA_V] / acc[MLA_V:MLA_V + 1]).astype(o_ref.dtype)

        return prologue, loop, epilogue

    tiles = [tile(a) for a in range(nq)]
    tiles[0][0]()
    for a in range(nq):
        alpha = tiles[a][1]()
        if a + 1 < nq:
            tiles[a + 1][0]()
        tiles[a][2](alpha)


def _attn(qt, k, vt, tq=512, nq=4, tk=256, steps=32):
    nh, s, w = k.shape
    assert steps % 2 == 0 and (s // tk) % steps == 0
    sub_scratch = [pltpu.VMEM((V_ROWS, tq), F32),
                   pltpu.VMEM((tk, tq), F32), pltpu.VMEM((tk, tq), F32),
                   pltpu.VMEM((tk, tq), BF16), pltpu.VMEM((tk, tq), BF16)]
    return pl.pallas_call(
        functools.partial(_attn_kernel, tk=tk, tq=tq, steps=steps), grid=(nh, s // (nq * tq)),
        in_specs=[pl.BlockSpec((w, nq * tq), lambda h, i: (h, i)),
                  pl.BlockSpec((1, s, w), lambda h, i: (h, 0, 0)),
                  pl.BlockSpec((w, s), lambda h, i: (h, 0))],
        out_specs=pl.BlockSpec((MLA_V, nq * tq), lambda h, i: (h, i)),
        out_shape=jax.ShapeDtypeStruct((nh * MLA_V, s), BF16),
        scratch_shapes=sub_scratch * nq,
        compiler_params=_params(("parallel", "parallel")),
        name="attn",
    )(qt, k, vt)


CONV_HALO = 8


def _conv_kernel(x_ref, prev_ref, next_ref, w_ref, b_ref, q_ref, kt_ref, buf, *, tm):
    i = pl.program_id(0)
    n = pl.num_programs(0)
    buf[CONV_HALO:CONV_HALO + tm, :] = x_ref[...]
    buf[0:CONV_HALO, :] = jnp.where(i > 0, prev_ref[...], 0.0)
    buf[CONV_HALO + tm:, :] = jnp.where(i < n - 1, next_ref[...], 0.0)
    acc = jnp.broadcast_to(b_ref[...], (tm, x_ref.shape[1]))
    for j in range(CONV_W):
        start = CONV_HALO - CONV_W // 2 + j
        acc = acc + w_ref[j:j + 1, :] * buf[start:start + tm, :]
    y = acc * _sigmoid(acc)
    pad = jnp.zeros((tm, HEAD_PAD - ML_QK), F32)
    q_ref[...] = jnp.concatenate(
        [piece for hd in range(ML_HEADS) for piece in (y[:, hd * ML_QK:(hd + 1) * ML_QK], pad)],
        axis=1).astype(BF16)
    kt_ref[...] = (y[:, QK_W:] * ML_QK ** -0.5).T


def _conv(x, w, b, tm=1024):
    s, c = x.shape
    r = tm // CONV_HALO
    nblk = s // CONV_HALO
    return pl.pallas_call(
        functools.partial(_conv_kernel, tm=tm), grid=(s // tm,),
        in_specs=[pl.BlockSpec((tm, c), lambda i: (i, 0)),
                  pl.BlockSpec((CONV_HALO, c), lambda i: (jnp.maximum(i * r - 1, 0), 0)),
                  pl.BlockSpec((CONV_HALO, c), lambda i: (jnp.minimum((i + 1) * r, nblk - 1), 0)),
                  _const_spec(w.shape), _const_spec(b.shape)],
        out_specs=[pl.BlockSpec((tm, QK_PAD), lambda i: (i, 0)),
                   pl.BlockSpec((QK_W, tm), lambda i: (0, i))],
        out_shape=[jax.ShapeDtypeStruct((s, QK_PAD), BF16),
                   jax.ShapeDtypeStruct((QK_W, s), F32)],
        scratch_shapes=[pltpu.VMEM((tm + 2 * CONV_HALO, c), F32)],
        compiler_params=_params(("parallel",)),
        name="conv",
    )(x, x, x, w, b)


def _mlstm_kernel(qf_ref, ktf_ref, vf_ref, gf_ref, cumf_ref, qb_ref, ktb_ref, vb_ref, gb_ref,
                  cumb_ref, hf_ref, hb_ref, c_sc, m_sc, *, chunks):
    L = ML_CHUNK

    @pl.when(pl.program_id(0) == 0)
    def _():
        c_sc[...] = jnp.zeros(c_sc.shape, F32)
        m_sc[...] = jnp.zeros(m_sc.shape, F32)

    row = lax.broadcasted_iota(jnp.int32, (L, L), 0)
    col = lax.broadcasted_iota(jnp.int32, (L, L), 1)
    mask = [col <= row, col >= row]
    ones_blk = jnp.ones((L, LANES), BF16)
    k_pad = jnp.zeros((HEAD_PAD - ML_QK, L), BF16)
    c_pad = jnp.zeros((HEAD_PAD - ML_QK, ML_V + LANES), BF16)
    refs = ((qf_ref, ktf_ref, vf_ref, gf_ref, hf_ref), (qb_ref, ktb_ref, vb_ref, gb_ref, hb_ref))
    cum_refs = (cumf_ref, cumb_ref)
    heads = [(d, hd) for d in range(N_DIR) for hd in range(ML_HEADS)]
    blk = lambda hd: slice(hd * HEAD_PAD, (hd + 1) * HEAD_PAD)
    kblk = lambda hd: slice(hd * ML_QK, (hd + 1) * ML_QK)

    wins = [[slice(lc * L, (lc + 1) * L) for lc in (step, chunks - 1 - step)]
            for step in range(chunks)]
    gates = [[refs[d][3][:, win[d]] for d in range(N_DIR)] for win in wins]
    cums = [[cum_refs[d][:, win[d]] for d in range(N_DIR)] for win in wins]

    q, kt, v_aug, qk, qc, m_prev = [{} for _ in range(6)]
    for step, win in enumerate(wins):
        gate, cum = gates[step], cums[step]
        for d, hd in heads:
            j = d * ML_HEADS + hd
            c = (step, j)
            q[c] = refs[d][0][win[d], blk(hd)]
            kt[c] = refs[d][1][kblk(hd), win[d]]
            v_aug[c] = jnp.concatenate([refs[d][2][win[d], blk(hd)], ones_blk], axis=1)
            qk[c] = _dot(q[c], jnp.concatenate([kt[c].astype(BF16), k_pad], axis=0))
            qc[c] = _dot(q[c], jnp.concatenate([c_sc[j].astype(BF16), c_pad], axis=0))
        for d, hd in heads:
            j = d * ML_HEADS + hd
            c = (step, j)
            br = cum[d][N_GATE + j:N_GATE + j + 1, :]
            ir = gate[d][j:j + 1, :]
            m_prev[c] = m_sc[j][0:1, 0:1]
            b_end = br[:, L - 1:L] if d == 0 else br[:, 0:1]
            g_row = b_end - br + ir
            m_new = jnp.maximum(b_end + m_prev[c], jnp.max(g_row, axis=1, keepdims=True))
            decay = jnp.exp(b_end + m_prev[c] - m_new)
            kw_t = (kt[c] * jnp.exp(g_row - m_new)).astype(BF16)
            c_sc[j] = decay * c_sc[j] + _dot(kw_t, v_aug[c])
            m_sc[j] = jnp.broadcast_to(m_new, m_sc.shape[1:])

    chains = [(step, d, hd) for step in range(chunks) for d, hd in heads]
    a, mm, mm_b, bc, sv = {}, {}, {}, {}, {}
    for step, d, hd in chains:
        j = d * ML_HEADS + hd
        c = (step, j)
        br = cums[step][d][N_GATE + j:N_GATE + j + 1, :]
        lf = gates[step][d][N_GATE + j:N_GATE + j + 1, :]
        a[c] = jnp.where(mask[d], gates[step][d][j:j + 1, :] - br, -jnp.inf)
        mm[c] = jnp.maximum(m_prev[c], jnp.max(a[c], axis=1, keepdims=True))
        bc[c] = jnp.sum(jnp.where(mask[d], lf, 0.0), axis=1, keepdims=True)
    for step, d, hd in chains:
        c = (step, d * ML_HEADS + hd)
        mm_b[c] = jnp.broadcast_to(mm[c], (L, L))
        smat = qk[c] * jnp.exp(a[c] - mm_b[c])
        sv[c] = _dot(smat.astype(BF16), v_aug[c])
    for step, d, hd in chains:
        c = (step, d * ML_HEADS + hd)
        inter = jnp.exp(m_prev[c] - mm_b[c])
        nv = sv[c] + jnp.concatenate([inter, inter], axis=1) * qc[c]
        den = nv[:, ML_V:ML_V + 1]
        inv = 1.0 / jnp.maximum(jnp.abs(den), jnp.exp(-(bc[c] + mm[c])))
        refs[d][4][wins[step][d], blk(hd)] = (nv[:, :ML_V] * inv).astype(BF16)


def _mlstm(q, kt, v, gate_t, cum_t, chunks=4):
    s = q.shape[0]
    tb = chunks * ML_CHUNK
    nb = s // tb
    ng = 2 * N_GATE
    rows = lambda w, f: pl.BlockSpec((tb, w), lambda c: (f(c), 0))
    cols = lambda h, f, r=0: pl.BlockSpec((h, tb), lambda c: (r, f(c)))
    fwd = lambda c: c
    bwd = lambda c: nb - 1 - c
    hshape = jax.ShapeDtypeStruct((s, ML_OUT), BF16)
    return pl.pallas_call(
        functools.partial(_mlstm_kernel, chunks=chunks), grid=(nb,),
        in_specs=[rows(QK_PAD, fwd), cols(QK_W, fwd), rows(ML_OUT, fwd),
                  cols(ng, fwd), cols(ng, fwd, 0),
                  rows(QK_PAD, bwd), cols(QK_W, bwd), rows(ML_OUT, bwd),
                  cols(ng, bwd), cols(ng, bwd, 1)],
        out_specs=[rows(ML_OUT, fwd), rows(ML_OUT, bwd)],
        out_shape=[hshape, hshape],
        scratch_shapes=[pltpu.VMEM((N_GATE, ML_QK, ML_V + LANES), F32),
                        pltpu.VMEM((N_GATE, 8, LANES), F32)],
        compiler_params=_params(("arbitrary",)),
        name="mlstm",
    )(q, kt, v, gate_t, cum_t, q, kt, v, gate_t, cum_t)


def _merge_kernel(h_ref, att_ref, hf_ref, hb_ref, mlo_ref, gates_ref, hn_ref,
                  wba_ref, wbm_ref, wout_ref, post_ref, o_ref):
    a = lax.dot_general(att_ref[...], wba_ref[...], (((0,), (0,)), ((), ())),
                        preferred_element_type=F32)
    hm = hf_ref[...].astype(F32) + hb_ref[...].astype(F32)
    heads = []
    for hd in range(ML_HEADS):
        blk = slice(hd * ML_V, (hd + 1) * ML_V)
        heads.append(_rms(hm[:, blk], hn_ref[:, blk]))
    hm = jnp.concatenate(heads, axis=1) * mlo_ref[...]
    bm = _dot(hm.astype(BF16), wbm_ref[...])
    g = gates_ref[...]
    mixed = _dot((g[:, :D_MODEL] * a + g[:, D_MODEL:] * bm).astype(BF16), wout_ref[...])
    o_ref[...] = h_ref[...] + _rms(mixed, post_ref[...])


def _merge(h, att, hf, hb, mlo, gates, hn, wba, wbm, wout, post, tm=1024):
    s, d = h.shape
    row = lambda w: pl.BlockSpec((tm, w), lambda i: (i, 0))
    return pl.pallas_call(
        _merge_kernel, grid=(s // tm,),
        in_specs=[row(d), pl.BlockSpec((att.shape[0], tm), lambda i: (0, i)),
                  row(ML_OUT), row(ML_OUT), row(ML_OUT),
                  row(2 * d), _const_spec(hn.shape), _const_spec(wba.shape),
                  _const_spec(wbm.shape), _const_spec(wout.shape), _const_spec(post.shape)],
        out_specs=row(d),
        out_shape=jax.ShapeDtypeStruct((s, d), F32),
        compiler_params=_params(("parallel",)),
        name="merge",
    )(h, att, hf, hb, mlo, gates, hn, wba, wbm, wout, post)


def _pad_heads(w, nh, width):
    r = w.shape[0]
    w = w.reshape(r, nh, width)
    return jnp.pad(w, ((0, 0), (0, 0), (0, HEAD_PAD - width))).reshape(r, nh * HEAD_PAD)


def _permute_w_in(w_in):
    d = w_in.shape[0]
    small = jnp.zeros((LANES, d), w_in.dtype)
    small = small.at[0:N_GATE].set(w_in[:, _O_MLI:_O_MLF].T)
    small = small.at[N_GATE:2 * N_GATE].set(w_in[:, _O_MLF:_O_MLO].T)
    small = small.at[ROPE_LO:ROPE_LO + MLA_ROPE].set(w_in[:, _O_KR:_O_MLQ].T)
    main = jnp.concatenate([w_in[:, _O_CQ:_O_KR].astype(BF16), w_in[:, _O_MLQ:_O_MLI].astype(BF16),
                            w_in[:, _O_MLO:_O_END].astype(BF16)], axis=1)
    return main, small.astype(BF16)


def kernel(x, p, positions, ffn1_pre_norm, ffn1_post_norm, ffn1_w_gate, ffn1_w_up, ffn1_w_down, mix_pre_norm, mix_post_norm, w_in, mla_q_norm, mla_kv_norm, mla_w_uq, mla_w_uk, mla_w_uv, ml_conv_w, ml_conv_b, ml_i_bias, ml_f_bias, ml_head_norm, w_branch_mla, w_branch_ml, w_out, ffn2_pre_norm, ffn2_post_norm, ffn2_w_gate, ffn2_w_up, ffn2_w_down, ple_pre_norm, ple_post_norm, ple_w_proj, ple_w_gate):
    depth, batch = p.shape[0], x.shape[0]
    assert depth == 1 and batch == 1, "kernel is specialised to DEPTH == 1, BATCH == 1"
    bf = lambda w: w.astype(BF16)
    h = x[0]
    pos = positions

    inv_freq = ROPE_THETA ** (-jnp.arange(0, MLA_ROPE, 2, dtype=F32) / MLA_ROPE)
    invf = inv_freq[:, None]

    i = 0
    h = _ffn(h, ffn1_pre_norm[i][None], ffn1_post_norm[i][None],
             ffn1_w_gate[i], ffn1_w_up[i], ffn1_w_down[i])

    vone = jnp.zeros((MLA_HEADS, HEAD_PAD), F32).at[:, MLA_V].set(1.0).reshape(-1, 1)
    win_main, win_small = _permute_w_in(w_in[i])
    gate_bias = jnp.concatenate([ml_i_bias[i].reshape(-1), ml_f_bias[i].reshape(-1)])[:, None]
    qt, k, vt, mlqk, mlv, gate_t, cum_t, mlo, gates = _proj(
        h, pos, invf, mix_pre_norm[i][None], win_main, win_small,
        gate_bias, mla_q_norm[i][None], mla_kv_norm[i][None],
        bf(_pad_heads(mla_w_uq[i], MLA_HEADS, MLA_NOPE + MLA_ROPE)).T,
        bf(_pad_heads(mla_w_uk[i], MLA_HEADS, MLA_NOPE)),
        bf(_pad_heads(mla_w_uv[i], MLA_HEADS, MLA_V)).T, vone)

    att = _attn(qt, k, vt)

    conv_w = jnp.pad(ml_conv_w[i], ((0, 8 - CONV_W), (0, 0)))
    conv_b = ml_conv_b[i][None]
    q_conv, kt_conv = _conv(mlqk, conv_w, conv_b)

    hf, hb = _mlstm(q_conv, kt_conv, mlv, gate_t, cum_t)

    h = _merge(h, att, hf, hb, mlo, gates, ml_head_norm[i][None], bf(w_branch_mla[i]),
               bf(w_branch_ml[i]), bf(w_out[i]), mix_post_norm[i][None])

    h = _ffn(h, ffn2_pre_norm[i][None], ffn2_post_norm[i][None],
             ffn2_w_gate[i], ffn2_w_up[i], ffn2_w_down[i],
             ple=(p[i, 0], ple_pre_norm[i][None], ple_post_norm[i][None],
                  bf(ple_w_proj[i]), bf(ple_w_gate[i])))
    return h[None]
```

```python
import functools
import math

import jax
import jax.numpy as jnp
from jax import lax
from jax.experimental import pallas as pl
from jax.experimental.pallas import tpu as pltpu

D_MODEL = 1024
EPS = 1e-6
MLA_HEADS = 8
MLA_Q_RANK = 256
MLA_KV_RANK = 256
MLA_NOPE = 64
MLA_ROPE = 32
MLA_V = 64
ROPE_THETA = 10000.0
ML_HEADS = 4
ML_QK = 64
ML_V = 128
ML_CHUNK = 128
CONV_W = 5
N_DIR = 2
MLA_OUT = MLA_HEADS * MLA_V
ML_OUT = ML_HEADS * ML_V

LANES = 128
HEAD_PAD = LANES
ROPE_LO = MLA_NOPE
ROPE_HALF = MLA_ROPE // 2
VMEM_LIMIT = 56 * 1024 * 1024

F32 = jnp.float32
BF16 = jnp.bfloat16
LOG2E = math.log2(math.e)
Q_SCALE = (MLA_NOPE + MLA_ROPE) ** -0.5 * LOG2E

_O_CQ = 0
_O_CKV = _O_CQ + MLA_Q_RANK
_O_KR = _O_CKV + MLA_KV_RANK
_O_MLQ = _O_KR + MLA_ROPE
_O_MLK = _O_MLQ + ML_HEADS * ML_QK
_O_MLV = _O_MLK + ML_HEADS * ML_QK
_O_MLI = _O_MLV + ML_OUT
_O_MLF = _O_MLI + N_DIR * ML_HEADS
_O_MLO = _O_MLF + N_DIR * ML_HEADS
_O_GATES = _O_MLO + ML_OUT
_O_END = _O_GATES + 2 * D_MODEL

N_GATE = N_DIR * ML_HEADS
QK_W = ML_HEADS * ML_QK
QK_PAD = ML_HEADS * HEAD_PAD

_P_CQ = 0
_P_CKV = _P_CQ + MLA_Q_RANK
_P_MLQK = _P_CKV + MLA_KV_RANK
_P_MLV = _P_MLQK + 2 * QK_W
_P_MLO = _P_MLV + ML_OUT
_P_GATES = _P_MLO + ML_OUT
_P_END = _P_GATES + 2 * D_MODEL


def _rms(x, w):
    ms = jnp.mean(x * x, axis=-1, keepdims=True)
    return x * lax.rsqrt(ms + EPS) * w


def _sigmoid(x):
    return 1.0 / (1.0 + jnp.exp(-x))


def _dot(a, b):
    return jnp.dot(a, b, preferred_element_type=F32)


def _const_spec(shape):
    nd = len(shape)
    return pl.BlockSpec(shape, lambda *_: (0,) * nd, pipeline_mode=pl.Buffered(1))


def _params(sem):
    return pltpu.CompilerParams(dimension_semantics=sem, vmem_limit_bytes=VMEM_LIMIT)


def _ffn_body(x, pre, post, wg, wu, wd):
    u = _rms(x, pre).astype(BF16)
    g = _dot(u, wg)
    up = _dot(u, wu)
    hid = (g * _sigmoid(g) * up).astype(BF16)
    y = _dot(hid, wd)
    return x + 0.5 * _rms(y, post)


STAGE_CHUNKS = 8
STAGE_SLOTS = 4


def _stage_weights(w_hbm, w_sc, stage, sem):
    jobs = [(w, sc, st, c) for w, sc, st in zip(w_hbm, w_sc, stage) for c in range(STAGE_CHUNKS)]
    ahead = STAGE_SLOTS - 1

    def copy(i):
        w, _, st, c = jobs[i]
        rows, slot = st.shape[1], i % STAGE_SLOTS
        return pltpu.make_async_copy(w.at[pl.ds(c * rows, rows), :], st.at[slot], sem.at[slot])

    for i in range(min(ahead, len(jobs))):
        copy(i).start()
    for i, (_, sc, st, c) in enumerate(jobs):
        if i + ahead < len(jobs):
            copy(i + ahead).start()
        copy(i).wait()
        rows = st.shape[1]
        sc[c * rows:(c + 1) * rows, :] = st[i % STAGE_SLOTS].astype(BF16)


def _ffn_weights(w_hbm, scratch):
    wg_sc, wu_sc, wd_sc, stage_wide, stage_tall, sem = scratch

    @pl.when(pl.program_id(0) == 0)
    def _():
        _stage_weights(w_hbm, (wg_sc, wu_sc, wd_sc), (stage_wide, stage_wide, stage_tall), sem)

    return wg_sc[...], wu_sc[...], wd_sc[...]


def _ffn_kernel(x_ref, pre_ref, post_ref, wg_hbm, wu_hbm, wd_hbm, o_ref, *scratch):
    o_ref[...] = _ffn_body(x_ref[...], pre_ref[...], post_ref[...],
                           *_ffn_weights((wg_hbm, wu_hbm, wd_hbm), scratch))


def _ffn_ple_kernel(x_ref, pre_ref, post_ref, wg_hbm, wu_hbm, wd_hbm,
                    p_ref, ppre_ref, ppost_ref, wproj_ref, wgate_ref, o_ref, *scratch):
    h = _ffn_body(x_ref[...], pre_ref[...], post_ref[...],
                  *_ffn_weights((wg_hbm, wu_hbm, wd_hbm), scratch))
    e = _dot(p_ref[...].astype(BF16), wproj_ref[...])
    g = _sigmoid(_dot(_rms(h, ppre_ref[...]).astype(BF16), wgate_ref[...]))
    o_ref[...] = h + _rms(g * e, ppost_ref[...])


def _ffn(x, pre, post, wg, wu, wd, ple=None, tm=512):
    s, d = x.shape
    f = wg.shape[1]
    row = pl.BlockSpec((tm, d), lambda i: (i, 0))
    hbm = pl.BlockSpec(memory_space=pl.ANY)
    in_specs = [row, _const_spec((1, d)), _const_spec((1, d)), hbm, hbm, hbm]
    args = [x, pre, post, wg, wu, wd]
    kern = _ffn_kernel
    if ple is not None:
        p, ppre, ppost, wproj, wgate = ple
        in_specs += [pl.BlockSpec((tm, p.shape[1]), lambda i: (i, 0)),
                     _const_spec((1, d)), _const_spec((1, d)),
                     _const_spec(wproj.shape), _const_spec(wgate.shape)]
        args += [p, ppre, ppost, wproj, wgate]
        kern = _ffn_ple_kernel
    return pl.pallas_call(
        kern, grid=(s // tm,), in_specs=in_specs, out_specs=row,
        out_shape=jax.ShapeDtypeStruct((s, d), F32),
        scratch_shapes=[pltpu.VMEM((d, f), BF16), pltpu.VMEM((d, f), BF16), pltpu.VMEM((f, d), BF16),
                        pltpu.VMEM((STAGE_SLOTS, d // STAGE_CHUNKS, f), F32),
                        pltpu.VMEM((STAGE_SLOTS, f // STAGE_CHUNKS, d), F32),
                        pltpu.SemaphoreType.DMA((STAGE_SLOTS,))],
        compiler_params=_params(("arbitrary",)),
        name="ffn" if ple is None else "ffn_ple",
    )(*args)


def _dot_nt(a, b):
    return lax.dot_general(a, b, (((1,), (1,)), ((), ())), preferred_element_type=F32)


def _rope_tables_t(pos_ref, invf_ref):
    ang = pos_ref[...].astype(F32) * invf_ref[...]
    return jnp.cos(ang), jnp.sin(ang)


def _rope_t(x, c, s):
    x1, x2 = x[:ROPE_HALF], x[ROPE_HALF:]
    return x1 * c - x2 * s, x2 * c + x1 * s


def _proj_kernel(h_ref, pos_ref, invf_ref, pre_ref, win_ref, wsm_ref, gbias_ref, qn_ref, kvn_ref,
                 wuqt_ref, wuk_ref, wuvt_ref, vone_ref,
                 qt_ref, k_ref, vt_ref, mlqk_ref, mlv_ref, gt_ref, cum_ref, mlo_ref, gates_ref):
    u = _rms(h_ref[...], pre_ref[...]).astype(BF16)
    zdot = lambda a, b: _dot(u, win_ref[:, a:b])
    c, s = _rope_tables_t(pos_ref, invf_ref)
    rope = slice(ROPE_LO, ROPE_LO + MLA_ROPE)
    small_t = _dot_nt(wsm_ref[...], u)
    z_c = zdot(_P_CQ, _P_MLQK)
    gates_ref[...] = _sigmoid(zdot(_P_GATES, _P_END)).astype(BF16)

    pre = small_t[0:2 * N_GATE, :] + gbias_ref[...]
    is_f = lax.broadcasted_iota(jnp.int32, (2 * N_GATE, 1), 0) >= N_GATE
    gate = jnp.where(is_f, jax.nn.log_sigmoid(pre), pre)
    gt_ref[...] = gate
    r = lax.broadcasted_iota(jnp.int32, (ML_CHUNK, ML_CHUNK), 0)
    cidx = lax.broadcasted_iota(jnp.int32, (ML_CHUNK, ML_CHUNK), 1)
    for d, tri in enumerate((r <= cidx, r >= cidx)):
        for ch in range(gate.shape[1] // ML_CHUNK):
            cs = slice(ch * ML_CHUNK, (ch + 1) * ML_CHUNK)
            cum_ref[d * 2 * N_GATE:(d + 1) * 2 * N_GATE, cs] = jnp.dot(
                gate[:, cs], tri.astype(F32), preferred_element_type=F32,
                precision=lax.Precision.HIGHEST)

    tm = small_t.shape[1]
    k_rope = jnp.concatenate(
        [jnp.zeros((ROPE_LO, tm), F32), *_rope_t(small_t[rope, :], c, s),
         jnp.zeros((HEAD_PAD - ROPE_LO - MLA_ROPE, tm), F32)], axis=0).T

    cq = _rms(z_c[:, :MLA_Q_RANK], qn_ref[...]).astype(BF16)
    ckv = _rms(z_c[:, MLA_Q_RANK:], kvn_ref[...]).astype(BF16)
    qpt = _dot_nt(wuqt_ref[...], cq)
    kp = _dot(ckv, wuk_ref[...])
    vt_ref[...] = (_dot_nt(wuvt_ref[...], ckv) + vone_ref[...]).astype(BF16)
    mlqk_ref[...] = zdot(_P_MLQK, _P_MLV)
    mlv_ref[...] = zdot(_P_MLV, _P_MLO).astype(BF16)
    mlo_ref[...] = _sigmoid(zdot(_P_MLO, _P_GATES)).astype(BF16)
    for hd in range(MLA_HEADS):
        blk = slice(hd * HEAD_PAD, (hd + 1) * HEAD_PAD)
        x = qpt[blk, :]
        r1, r2 = _rope_t(x[rope, :], c, s)
        qt_ref[blk, :] = jnp.concatenate(
            [x[:ROPE_LO] * Q_SCALE, r1 * Q_SCALE, r2 * Q_SCALE, x[ROPE_LO + MLA_ROPE:]],
            axis=0).astype(BF16)
        k_ref[hd] = (kp[:, blk] + k_rope).astype(BF16)


def _proj(h, pos, invf, pre, win, wsm, gbias, qn, kvn, wuqt, wuk, wuvt, vone, tm=1024):
    s, d = h.shape
    ng = 2 * N_GATE
    row = lambda w: pl.BlockSpec((tm, w), lambda i: (i, 0))
    col = lambda r: pl.BlockSpec((r, tm), lambda i: (0, i))
    cshape = jax.ShapeDtypeStruct((MLA_HEADS * HEAD_PAD, s), BF16)
    return pl.pallas_call(
        _proj_kernel, grid=(s // tm,),
        in_specs=[row(d), col(1), _const_spec(invf.shape), _const_spec((1, d)),
                  _const_spec(win.shape), _const_spec(wsm.shape), _const_spec(gbias.shape),
                  _const_spec(qn.shape), _const_spec(kvn.shape),
                  _const_spec(wuqt.shape), _const_spec(wuk.shape), _const_spec(wuvt.shape),
                  _const_spec(vone.shape)],
        out_specs=[col(MLA_HEADS * HEAD_PAD),
                   pl.BlockSpec((MLA_HEADS, tm, HEAD_PAD), lambda i: (0, i, 0)),
                   col(MLA_HEADS * HEAD_PAD),
                   row(2 * QK_W), row(ML_OUT), col(ng), col(N_DIR * ng),
                   row(ML_OUT), row(2 * D_MODEL)],
        out_shape=[cshape, jax.ShapeDtypeStruct((MLA_HEADS, s, HEAD_PAD), BF16), cshape,
                   jax.ShapeDtypeStruct((s, 2 * QK_W), F32),
                   jax.ShapeDtypeStruct((s, ML_OUT), BF16),
                   jax.ShapeDtypeStruct((ng, s), F32),
                   jax.ShapeDtypeStruct((N_DIR * ng, s), F32),
                   jax.ShapeDtypeStruct((s, ML_OUT), BF16),
                   jax.ShapeDtypeStruct((s, 2 * D_MODEL), BF16)],
        compiler_params=_params(("parallel",)),
        name="proj",
    )(h, pos, invf, pre, win, wsm, gbias, qn, kvn, wuqt, wuk, wuvt, vone)


BF16_ROWS = 16
V_ROWS = -(-(MLA_V + 1) // BF16_ROWS) * BF16_ROWS


def _attn_kernel(qt_ref, k_ref, vt_ref, o_ref, *scratch, tk, tq, steps):
    n = k_ref.shape[1] // tk
    nq = qt_ref.shape[1] // tq

    def tile(a):
        acc_sc, st0, st1, p0, p1 = scratch[5 * a:5 * a + 5]
        st, pb = (st0, st1), (p0, p1)
        qt = qt_ref[:, a * tq:(a + 1) * tq]

        def accumulate(t_prev, slot, alpha):
            off = pl.multiple_of(t_prev * tk, tk)
            vt = vt_ref[0:V_ROWS, pl.ds(off, tk)]
            acc_sc[...] = alpha * acc_sc[...] + _dot(vt, pb[slot][...])

        def prologue():
            acc_sc[...] = jnp.zeros(acc_sc.shape, F32)
            p1[...] = jnp.zeros(p1.shape, BF16)
            st0[...] = _dot(k_ref[0, 0:tk, :], qt)

        def step(t, slot, m, alpha_prev):
            off = pl.multiple_of(jnp.minimum(t + 1, n - 1) * tk, tk)
            st[1 - slot][...] = _dot(k_ref[0, pl.ds(off, tk), :], qt)
            accumulate(jnp.maximum(t - 1, 0), 1 - slot, alpha_prev)
            x = st[slot][...]
            m_new = jnp.maximum(m, jnp.max(x, axis=0, keepdims=True))
            pb[slot][...] = jnp.exp2(x - m_new).astype(BF16)
            return m_new, jnp.exp2(m - m_new)

        def body(i, carry):
            m, alpha = carry
            for j in range(steps):
                m, alpha = step(i * steps + j, j % 2, m, alpha)
            return m, alpha

        def loop():
            init = (jnp.full((1, tq), -jnp.inf, F32), jnp.ones((1, tq), F32))
            return lax.fori_loop(0, n // steps, body, init)[1]

        def epilogue(alpha):
            accumulate(n - 1, (n - 1) % 2, alpha)
            acc = acc_sc[...]
            o_ref[:, a * tq:(a + 1) * tq] = (acc[0:MLA_V] / acc[MLA_V:MLA_V + 1]).astype(o_ref.dtype)

        return prologue, loop, epilogue

    tiles = [tile(a) for a in range(nq)]
    tiles[0][0]()
    for a in range(nq):
        alpha = tiles[a][1]()
        if a + 1 < nq:
            tiles[a + 1][0]()
        tiles[a][2](alpha)


def _attn(qt, k, vt, tq=512, nq=4, tk=256, steps=32):
    nh, s, w = k.shape
    assert steps % 2 == 0 and (s // tk) % steps == 0
    sub_scratch = [pltpu.VMEM((V_ROWS, tq), F32),
                   pltpu.VMEM((tk, tq), F32), pltpu.VMEM((tk, tq), F32),
                   pltpu.VMEM((tk, tq), BF16), pltpu.VMEM((tk, tq), BF16)]
    return pl.pallas_call(
        functools.partial(_attn_kernel, tk=tk, tq=tq, steps=steps), grid=(nh, s // (nq * tq)),
        in_specs=[pl.BlockSpec((w, nq * tq), lambda h, i: (h, i)),
                  pl.BlockSpec((1, s, w), lambda h, i: (h, 0, 0)),
                  pl.BlockSpec((w, s), lambda h, i: (h, 0))],
        out_specs=pl.BlockSpec((MLA_V, nq * tq), lambda h, i: (h, i)),
        out_shape=jax.ShapeDtypeStruct((nh * MLA_V, s), BF16),
        scratch_shapes=sub_scratch * nq,
        compiler_params=_params(("parallel", "parallel")),
        name="attn",
    )(qt, k, vt)


CONV_HALO = 8


def _conv_kernel(x_ref, prev_ref, next_ref, w_ref, b_ref, q_ref, kt_ref, buf, *, tm):
    i = pl.program_id(0)
    n = pl.num_programs(0)
    buf[CONV_HALO:CONV_HALO + tm, :] = x_ref[...]
    buf[0:CONV_HALO, :] = jnp.where(i > 0, prev_ref[...], 0.0)
    buf[CONV_HALO + tm:, :] = jnp.where(i < n - 1, next_ref[...], 0.0)
    acc = jnp.broadcast_to(b_ref[...], (tm, x_ref.shape[1]))
    for j in range(CONV_W):
        start = CONV_HALO - CONV_W // 2 + j
        acc = acc + w_ref[j:j + 1, :] * buf[start:start + tm, :]
    y = acc * _sigmoid(acc)
    pad = jnp.zeros((tm, HEAD_PAD - ML_QK), F32)
    q_ref[...] = jnp.concatenate(
        [piece for hd in range(ML_HEADS) for piece in (y[:, hd * ML_QK:(hd + 1) * ML_QK], pad)],
        axis=1).astype(BF16)
    kt_ref[...] = (y[:, QK_W:] * ML_QK ** -0.5).T


def _conv(x, w, b, tm=2048):
    s, c = x.shape
    r = tm // CONV_HALO
    nblk = s // CONV_HALO
    return pl.pallas_call(
        functools.partial(_conv_kernel, tm=tm), grid=(s // tm,),
        in_specs=[pl.BlockSpec((tm, c), lambda i: (i, 0)),
                  pl.BlockSpec((CONV_HALO, c), lambda i: (jnp.maximum(i * r - 1, 0), 0)),
                  pl.BlockSpec((CONV_HALO, c), lambda i: (jnp.minimum((i + 1) * r, nblk - 1), 0)),
                  _const_spec(w.shape), _const_spec(b.shape)],
        out_specs=[pl.BlockSpec((tm, QK_PAD), lambda i: (i, 0)),
                   pl.BlockSpec((QK_W, tm), lambda i: (0, i))],
        out_shape=[jax.ShapeDtypeStruct((s, QK_PAD), BF16),
                   jax.ShapeDtypeStruct((QK_W, s), F32)],
        scratch_shapes=[pltpu.VMEM((tm + 2 * CONV_HALO, c), F32)],
        compiler_params=_params(("parallel",)),
        name="conv",
    )(x, x, x, w, b)


def _mlstm_kernel(qf_ref, ktf_ref, vf_ref, gf_ref, cumf_ref, qb_ref, ktb_ref, vb_ref, gb_ref,
                  cumb_ref, hf_ref, hb_ref, c_sc, m_sc, *, chunks):
    L = ML_CHUNK

    @pl.when(pl.program_id(0) == 0)
    def _():
        c_sc[...] = jnp.zeros(c_sc.shape, F32)
        m_sc[...] = jnp.zeros(m_sc.shape, F32)

    row = lax.broadcasted_iota(jnp.int32, (L, L), 0)
    col = lax.broadcasted_iota(jnp.int32, (L, L), 1)
    mask = [col <= row, col >= row]
    ones_blk = jnp.ones((L, LANES), BF16)
    k_pad = jnp.zeros((HEAD_PAD - ML_QK, L), BF16)
    c_pad = jnp.zeros((HEAD_PAD - ML_QK, ML_V + LANES), BF16)
    refs = ((qf_ref, ktf_ref, vf_ref, gf_ref, hf_ref), (qb_ref, ktb_ref, vb_ref, gb_ref, hb_ref))
    cum_refs = (cumf_ref, cumb_ref)
    heads = [(d, hd) for d in range(N_DIR) for hd in range(ML_HEADS)]
    blk = lambda hd: slice(hd * HEAD_PAD, (hd + 1) * HEAD_PAD)
    kblk = lambda hd: slice(hd * ML_QK, (hd + 1) * ML_QK)

    wins = [[slice(lc * L, (lc + 1) * L) for lc in (step, chunks - 1 - step)]
            for step in range(chunks)]
    gates = [[refs[d][3][:, win[d]] for d in range(N_DIR)] for win in wins]
    cums = [[cum_refs[d][:, win[d]] for d in range(N_DIR)] for win in wins]

    q, kt, v_aug, qk, qc, m_prev = [{} for _ in range(6)]
    for step, win in enumerate(wins):
        gate, cum = gates[step], cums[step]
        for d, hd in heads:
            j = d * ML_HEADS + hd
            c = (step, j)
            q[c] = refs[d][0][win[d], blk(hd)]
            kt[c] = refs[d][1][kblk(hd), win[d]]
            v_aug[c] = jnp.concatenate([refs[d][2][win[d], blk(hd)], ones_blk], axis=1)
            qk[c] = _dot(q[c], jnp.concatenate([kt[c].astype(BF16), k_pad], axis=0))
            qc[c] = _dot(q[c], jnp.concatenate([c_sc[j].astype(BF16), c_pad], axis=0))
        for d, hd in heads:
            j = d * ML_HEADS + hd
            c = (step, j)
            br = cum[d][N_GATE + j:N_GATE + j + 1, :]
            ir = gate[d][j:j + 1, :]
            m_prev[c] = m_sc[j][0:1, 0:1]
            b_end = br[:, L - 1:L] if d == 0 else br[:, 0:1]
            g_row = b_end - br + ir
            m_new = jnp.maximum(b_end + m_prev[c], jnp.max(g_row, axis=1, keepdims=True))
            decay = jnp.exp(b_end + m_prev[c] - m_new)
            kw_t = (kt[c] * jnp.exp(g_row - m_new)).astype(BF16)
            c_sc[j] = decay * c_sc[j] + _dot(kw_t, v_aug[c])
            m_sc[j] = jnp.broadcast_to(m_new, m_sc.shape[1:])

    chains = [(step, d, hd) for step in range(chunks) for d, hd in heads]
    a, mm, mm_b, bc, sv = {}, {}, {}, {}, {}
    for step, d, hd in chains:
        j = d * ML_HEADS + hd
        c = (step, j)
        br = cums[step][d][N_GATE + j:N_GATE + j + 1, :]
        lf = gates[step][d][N_GATE + j:N_GATE + j + 1, :]
        a[c] = jnp.where(mask[d], gates[step][d][j:j + 1, :] - br, -jnp.inf)
        mm[c] = jnp.maximum(m_prev[c], jnp.max(a[c], axis=1, keepdims=True))
        bc[c] = jnp.sum(jnp.where(mask[d], lf, 0.0), axis=1, keepdims=True)
    for step, d, hd in chains:
        c = (step, d * ML_HEADS + hd)
        mm_b[c] = jnp.broadcast_to(mm[c], (L, L))
        smat = qk[c] * jnp.exp(a[c] - mm_b[c])
        sv[c] = _dot(smat.astype(BF16), v_aug[c])
    for step, d, hd in chains:
        c = (step, d * ML_HEADS + hd)
        inter = jnp.exp(m_prev[c] - mm_b[c])
        nv = sv[c] + jnp.concatenate([inter, inter], axis=1) * qc[c]
        den = nv[:, ML_V:ML_V + 1]
        inv = 1.0 / jnp.maximum(jnp.abs(den), jnp.exp(-(bc[c] + mm[c])))
        refs[d][4][wins[step][d], blk(hd)] = (nv[:, :ML_V] * inv).astype(BF16)


def _mlstm(q, kt, v, gate_t, cum_t, chunks=8):
    s = q.shape[0]
    tb = chunks * ML_CHUNK
    nb = s // tb
    ng = 2 * N_GATE
    rows = lambda w, f: pl.BlockSpec((tb, w), lambda c: (f(c), 0))
    cols = lambda h, f, r=0: pl.BlockSpec((h, tb), lambda c: (r, f(c)))
    fwd = lambda c: c
    bwd = lambda c: nb - 1 - c
    hshape = jax.ShapeDtypeStruct((s, ML_OUT), BF16)
    return pl.pallas_call(
        functools.partial(_mlstm_kernel, chunks=chunks), grid=(nb,),
        in_specs=[rows(QK_PAD, fwd), cols(QK_W, fwd), rows(ML_OUT, fwd),
                  cols(ng, fwd), cols(ng, fwd, 0),
                  rows(QK_PAD, bwd), cols(QK_W, bwd), rows(ML_OUT, bwd),
                  cols(ng, bwd), cols(ng, bwd, 1)],
        out_specs=[rows(ML_OUT, fwd), rows(ML_OUT, bwd)],
        out_shape=[hshape, hshape],
        scratch_shapes=[pltpu.VMEM((N_GATE, ML_QK, ML_V + LANES), F32),
                        pltpu.VMEM((N_GATE, 8, LANES), F32)],
        compiler_params=_params(("arbitrary",)),
        name="mlstm",
    )(q, kt, v, gate_t, cum_t, q, kt, v, gate_t, cum_t)


def _merge_kernel(h_ref, att_ref, hf_ref, hb_ref, mlo_ref, gates_ref, hn_ref,
                  wba_ref, wbm_ref, wout_ref, post_ref, o_ref):
    a = lax.dot_general(att_ref[...], wba_ref[...], (((0,), (0,)), ((), ())),
                        preferred_element_type=F32)
    hm = hf_ref[...].astype(F32) + hb_ref[...].astype(F32)
    heads = []
    for hd in range(ML_HEADS):
        blk = slice(hd * ML_V, (hd + 1) * ML_V)
        heads.append(_rms(hm[:, blk], hn_ref[:, blk]))
    hm = jnp.concatenate(heads, axis=1) * mlo_ref[...]
    bm = _dot(hm.astype(BF16), wbm_ref[...])
    g = gates_ref[...]
    mixed = _dot((g[:, :D_MODEL] * a + g[:, D_MODEL:] * bm).astype(BF16), wout_ref[...])
    o_ref[...] = h_ref[...] + _rms(mixed, post_ref[...])


def _merge(h, att, hf, hb, mlo, gates, hn, wba, wbm, wout, post, tm=1024):
    s, d = h.shape
    row = lambda w: pl.BlockSpec((tm, w), lambda i: (i, 0))
    return pl.pallas_call(
        _merge_kernel, grid=(s // tm,),
        in_specs=[row(d), pl.BlockSpec((att.shape[0], tm), lambda i: (0, i)),
                  row(ML_OUT), row(ML_OUT), row(ML_OUT),
                  row(2 * d), _const_spec(hn.shape), _const_spec(wba.shape),
                  _const_spec(wbm.shape), _const_spec(wout.shape), _const_spec(post.shape)],
        out_specs=row(d),
        out_shape=jax.ShapeDtypeStruct((s, d), F32),
        compiler_params=_params(("parallel",)),
        name="merge",
    )(h, att, hf, hb, mlo, gates, hn, wba, wbm, wout, post)


def _pad_heads(w, nh, width):
    r = w.shape[0]
    w = w.reshape(r, nh, width)
    return jnp.pad(w, ((0, 0), (0, 0), (0, HEAD_PAD - width))).reshape(r, nh * HEAD_PAD)


def _permute_w_in(w_in):
    d = w_in.shape[0]
    small = jnp.zeros((LANES, d), w_in.dtype)
    small = small.at[0:N_GATE].set(w_in[:, _O_MLI:_O_MLF].T)
    small = small.at[N_GATE:2 * N_GATE].set(w_in[:, _O_MLF:_O_MLO].T)
    small = small.at[ROPE_LO:ROPE_LO + MLA_ROPE].set(w_in[:, _O_KR:_O_MLQ].T)
    main = jnp.concatenate([w_in[:, _O_CQ:_O_KR].astype(BF16), w_in[:, _O_MLQ:_O_MLI].astype(BF16),
                            w_in[:, _O_MLO:_O_END].astype(BF16)], axis=1)
    return main, small.astype(BF16)


def kernel(x, p, positions, ffn1_pre_norm, ffn1_post_norm, ffn1_w_gate, ffn1_w_up, ffn1_w_down, mix_pre_norm, mix_post_norm, w_in, mla_q_norm, mla_kv_norm, mla_w_uq, mla_w_uk, mla_w_uv, ml_conv_w, ml_conv_b, ml_i_bias, ml_f_bias, ml_head_norm, w_branch_mla, w_branch_ml, w_out, ffn2_pre_norm, ffn2_post_norm, ffn2_w_gate, ffn2_w_up, ffn2_w_down, ple_pre_norm, ple_post_norm, ple_w_proj, ple_w_gate):
    depth, batch = p.shape[0], x.shape[0]
    assert depth == 1 and batch == 1, "kernel is specialised to DEPTH == 1, BATCH == 1"
    bf = lambda w: w.astype(BF16)
    h = x[0]
    pos = positions

    inv_freq = ROPE_THETA ** (-jnp.arange(0, MLA_ROPE, 2, dtype=F32) / MLA_ROPE)
    invf = inv_freq[:, None]

    i = 0
    h = _ffn(h, ffn1_pre_norm[i][None], ffn1_post_norm[i][None],
             ffn1_w_gate[i], ffn1_w_up[i], ffn1_w_down[i])

    vone = jnp.zeros((MLA_HEADS, HEAD_PAD), F32).at[:, MLA_V].set(1.0).reshape(-1, 1)
    win_main, win_small = _permute_w_in(w_in[i])
    gate_bias = jnp.concatenate([ml_i_bias[i].reshape(-1), ml_f_bias[i].reshape(-1)])[:, None]
    qt, k, vt, mlqk, mlv, gate_t, cum_t, mlo, gates = _proj(
        h, pos, invf, mix_pre_norm[i][None], win_main, win_small,
        gate_bias, mla_q_norm[i][None], mla_kv_norm[i][None],
        bf(_pad_heads(mla_w_uq[i], MLA_HEADS, MLA_NOPE + MLA_ROPE)).T,
        bf(_pad_heads(mla_w_uk[i], MLA_HEADS, MLA_NOPE)),
        bf(_pad_heads(mla_w_uv[i], MLA_HEADS, MLA_V)).T, vone)

    att = _attn(qt, k, vt)

    conv_w = jnp.pad(ml_conv_w[i], ((0, 8 - CONV_W), (0, 0)))
    conv_b = ml_conv_b[i][None]
    q_conv, kt_conv = _conv(mlqk, conv_w, conv_b)

    hf, hb = _mlstm(q_conv, kt_conv, mlv, gate_t, cum_t)

    h = _merge(h, att, hf, hb, mlo, gates, ml_head_norm[i][None], bf(w_branch_mla[i]),
               bf(w_branch_ml[i]), bf(w_out[i]), mix_post_norm[i][None])

    h = _ffn(h, ffn2_pre_norm[i][None], ffn2_post_norm[i][None],
             ffn2_w_gate[i], ffn2_w_up[i], ffn2_w_down[i],
             ple=(p[i, 0], ple_pre_norm[i][None], ple_post_norm[i][None],
                  bf(ple_w_proj[i]), bf(ple_w_gate[i])))
    return h[None]
```

```python
import functools
import math

import jax
import jax.numpy as jnp
from jax import lax
from jax.experimental import pallas as pl
from jax.experimental.pallas import tpu as pltpu

D_MODEL = 1024
EPS = 1e-6
MLA_HEADS = 8
MLA_Q_RANK = 256
MLA_KV_RANK = 256
MLA_NOPE = 64
MLA_ROPE = 32
MLA_V = 64
ROPE_THETA = 10000.0
ML_HEADS = 4
ML_QK = 64
ML_V = 128
ML_CHUNK = 128
CONV_W = 5
N_DIR = 2
MLA_OUT = MLA_HEADS * MLA_V
ML_OUT = ML_HEADS * ML_V

LANES = 128
HEAD_PAD = LANES
ROPE_LO = MLA_NOPE
ROPE_HALF = MLA_ROPE // 2
VMEM_LIMIT = 56 * 1024 * 1024

F32 = jnp.float32
BF16 = jnp.bfloat16
LOG2E = math.log2(math.e)
Q_SCALE = (MLA_NOPE + MLA_ROPE) ** -0.5 * LOG2E

_O_CQ = 0
_O_CKV = _O_CQ + MLA_Q_RANK
_O_KR = _O_CKV + MLA_KV_RANK
_O_MLQ = _O_KR + MLA_ROPE
_O_MLK = _O_MLQ + ML_HEADS * ML_QK
_O_MLV = _O_MLK + ML_HEADS * ML_QK
_O_MLI = _O_MLV + ML_OUT
_O_MLF = _O_MLI + N_DIR * ML_HEADS
_O_MLO = _O_MLF + N_DIR * ML_HEADS
_O_GATES = _O_MLO + ML_OUT
_O_END = _O_GATES + 2 * D_MODEL

N_GATE = N_DIR * ML_HEADS
QK_W = ML_HEADS * ML_QK
QK_PAD = ML_HEADS * HEAD_PAD

_P_CQ = 0
_P_CKV = _P_CQ + MLA_Q_RANK
_P_MLQK = _P_CKV + MLA_KV_RANK
_P_MLV = _P_MLQK + 2 * QK_W
_P_MLO = _P_MLV + ML_OUT
_P_GATES = _P_MLO + ML_OUT
_P_END = _P_GATES + 2 * D_MODEL


def _rms(x, w):
    ms = jnp.mean(x * x, axis=-1, keepdims=True)
    return x * lax.rsqrt(ms + EPS) * w


def _sigmoid(x):
    return 1.0 / (1.0 + jnp.exp(-x))


def _dot(a, b):
    return jnp.dot(a, b, preferred_element_type=F32)


def _const_spec(shape):
    nd = len(shape)
    return pl.BlockSpec(shape, lambda *_: (0,) * nd, pipeline_mode=pl.Buffered(1))


def _params(sem):
    return pltpu.CompilerParams(dimension_semantics=sem, vmem_limit_bytes=VMEM_LIMIT)


def _ffn_body(x, pre, post, wg, wu, wd):
    u = _rms(x, pre).astype(BF16)
    g = _dot(u, wg)
    up = _dot(u, wu)
    hid = (g * _sigmoid(g) * up).astype(BF16)
    y = _dot(hid, wd)
    return x + 0.5 * _rms(y, post)


STAGE_CHUNKS = 8
STAGE_SLOTS = 4


def _stage_weights(w_hbm, w_sc, stage, sem):
    jobs = [(w, sc, st, c) for w, sc, st in zip(w_hbm, w_sc, stage) for c in range(STAGE_CHUNKS)]
    ahead = STAGE_SLOTS - 1

    def copy(i):
        w, _, st, c = jobs[i]
        rows, slot = st.shape[1], i % STAGE_SLOTS
        return pltpu.make_async_copy(w.at[pl.ds(c * rows, rows), :], st.at[slot], sem.at[slot])

    for i in range(min(ahead, len(jobs))):
        copy(i).start()
    for i, (_, sc, st, c) in enumerate(jobs):
        if i + ahead < len(jobs):
            copy(i + ahead).start()
        copy(i).wait()
        rows = st.shape[1]
        sc[c * rows:(c + 1) * rows, :] = st[i % STAGE_SLOTS].astype(BF16)


def _ffn_weights(w_hbm, scratch):
    wg_sc, wu_sc, wd_sc, stage_wide, stage_tall, sem = scratch

    @pl.when(pl.program_id(0) == 0)
    def _():
        _stage_weights(w_hbm, (wg_sc, wu_sc, wd_sc), (stage_wide, stage_wide, stage_tall), sem)

    return wg_sc[...], wu_sc[...], wd_sc[...]


def _ffn_kernel(x_ref, pre_ref, post_ref, wg_hbm, wu_hbm, wd_hbm, o_ref, *scratch):
    o_ref[...] = _ffn_body(x_ref[...], pre_ref[...], post_ref[...],
                           *_ffn_weights((wg_hbm, wu_hbm, wd_hbm), scratch))


def _ffn_ple_kernel(x_ref, pre_ref, post_ref, wg_hbm, wu_hbm, wd_hbm,
                    p_ref, ppre_ref, ppost_ref, wproj_ref, wgate_ref, o_ref, *scratch):
    h = _ffn_body(x_ref[...], pre_ref[...], post_ref[...],
                  *_ffn_weights((wg_hbm, wu_hbm, wd_hbm), scratch))
    e = _dot(p_ref[...].astype(BF16), wproj_ref[...])
    g = _sigmoid(_dot(_rms(h, ppre_ref[...]).astype(BF16), wgate_ref[...]))
    o_ref[...] = h + _rms(g * e, ppost_ref[...])


def _ffn(x, pre, post, wg, wu, wd, ple=None, tm=512):
    s, d = x.shape
    f = wg.shape[1]
    row = pl.BlockSpec((tm, d), lambda i: (i, 0))
    hbm = pl.BlockSpec(memory_space=pl.ANY)
    in_specs = [row, _const_spec((1, d)), _const_spec((1, d)), hbm, hbm, hbm]
    args = [x, pre, post, wg, wu, wd]
    kern = _ffn_kernel
    if ple is not None:
        p, ppre, ppost, wproj, wgate = ple
        in_specs += [pl.BlockSpec((tm, p.shape[1]), lambda i: (i, 0)),
                     _const_spec((1, d)), _const_spec((1, d)),
                     _const_spec(wproj.shape), _const_spec(wgate.shape)]
        args += [p, ppre, ppost, wproj, wgate]
        kern = _ffn_ple_kernel
    return pl.pallas_call(
        kern, grid=(s // tm,), in_specs=in_specs, out_specs=row,
        out_shape=jax.ShapeDtypeStruct((s, d), F32),
        scratch_shapes=[pltpu.VMEM((d, f), BF16), pltpu.VMEM((d, f), BF16), pltpu.VMEM((f, d), BF16),
                        pltpu.VMEM((STAGE_SLOTS, d // STAGE_CHUNKS, f), F32),
                        pltpu.VMEM((STAGE_SLOTS, f // STAGE_CHUNKS, d), F32),
                        pltpu.SemaphoreType.DMA((STAGE_SLOTS,))],
        compiler_params=_params(("arbitrary",)),
        name="ffn" if ple is None else "ffn_ple",
    )(*args)


def _dot_nt(a, b):
    return lax.dot_general(a, b, (((1,), (1,)), ((), ())), preferred_element_type=F32)


def _rope_tables_t(pos_ref, invf_ref):
    ang = pos_ref[...].astype(F32) * invf_ref[...]
    return jnp.cos(ang), jnp.sin(ang)


def _rope_t(x, c, s):
    x1, x2 = x[:ROPE_HALF], x[ROPE_HALF:]
    return x1 * c - x2 * s, x2 * c + x1 * s


def _proj_kernel(h_ref, pos_ref, invf_ref, pre_ref, win_ref, wsm_ref, gbias_ref, qn_ref, kvn_ref,
                 wuqt_ref, wuk_ref, wuvt_ref, vone_ref,
                 qt_ref, k_ref, vt_ref, mlqk_ref, mlv_ref, gt_ref, cum_ref, mlo_ref, gates_ref):
    u = _rms(h_ref[...], pre_ref[...]).astype(BF16)
    zdot = lambda a, b: _dot(u, win_ref[:, a:b])
    c, s = _rope_tables_t(pos_ref, invf_ref)
    rope = slice(ROPE_LO, ROPE_LO + MLA_ROPE)
    small_t = _dot_nt(wsm_ref[...], u)
    z_c = zdot(_P_CQ, _P_MLQK)
    gates_ref[...] = _sigmoid(zdot(_P_GATES, _P_END)).astype(BF16)

    pre = small_t[0:2 * N_GATE, :] + gbias_ref[...]
    is_f = lax.broadcasted_iota(jnp.int32, (2 * N_GATE, 1), 0) >= N_GATE
    gate = jnp.where(is_f, jax.nn.log_sigmoid(pre), pre)
    gt_ref[...] = gate
    r = lax.broadcasted_iota(jnp.int32, (ML_CHUNK, ML_CHUNK), 0)
    cidx = lax.broadcasted_iota(jnp.int32, (ML_CHUNK, ML_CHUNK), 1)
    for d, tri in enumerate((r <= cidx, r >= cidx)):
        for ch in range(gate.shape[1] // ML_CHUNK):
            cs = slice(ch * ML_CHUNK, (ch + 1) * ML_CHUNK)
            cum_ref[d * 2 * N_GATE:(d + 1) * 2 * N_GATE, cs] = jnp.dot(
                gate[:, cs], tri.astype(F32), preferred_element_type=F32,
                precision=lax.Precision.HIGHEST)

    tm = small_t.shape[1]
    k_rope = jnp.concatenate(
        [jnp.zeros((ROPE_LO, tm), F32), *_rope_t(small_t[rope, :], c, s),
         jnp.zeros((HEAD_PAD - ROPE_LO - MLA_ROPE, tm), F32)], axis=0).T

    cq = _rms(z_c[:, :MLA_Q_RANK], qn_ref[...]).astype(BF16)
    ckv = _rms(z_c[:, MLA_Q_RANK:], kvn_ref[...]).astype(BF16)
    qpt = _dot_nt(wuqt_ref[...], cq)
    kp = _dot(ckv, wuk_ref[...])
    vt_ref[...] = (_dot_nt(wuvt_ref[...], ckv) + vone_ref[...]).astype(BF16)
    mlqk_ref[...] = zdot(_P_MLQK, _P_MLV)
    mlv_ref[...] = zdot(_P_MLV, _P_MLO).astype(BF16)
    mlo_ref[...] = _sigmoid(zdot(_P_MLO, _P_GATES)).astype(BF16)
    for hd in range(MLA_HEADS):
        blk = slice(hd * HEAD_PAD, (hd + 1) * HEAD_PAD)
        x = qpt[blk, :]
        r1, r2 = _rope_t(x[rope, :], c, s)
        qt_ref[blk, :] = jnp.concatenate(
            [x[:ROPE_LO] * Q_SCALE, r1 * Q_SCALE, r2 * Q_SCALE, x[ROPE_LO + MLA_ROPE:]],
            axis=0).astype(BF16)
        k_ref[hd] = (kp[:, blk] + k_rope).astype(BF16)


def _proj(h, pos, invf, pre, win, wsm, gbias, qn, kvn, wuqt, wuk, wuvt, vone, tm=1024):
    s, d = h.shape
    ng = 2 * N_GATE
    row = lambda w: pl.BlockSpec((tm, w), lambda i: (i, 0))
    col = lambda r: pl.BlockSpec((r, tm), lambda i: (0, i))
    cshape = jax.ShapeDtypeStruct((MLA_HEADS * HEAD_PAD, s), BF16)
    return pl.pallas_call(
        _proj_kernel, grid=(s // tm,),
        in_specs=[row(d), col(1), _const_spec(invf.shape), _const_spec((1, d)),
                  _const_spec(win.shape), _const_spec(wsm.shape), _const_spec(gbias.shape),
                  _const_spec(qn.shape), _const_spec(kvn.shape),
                  _const_spec(wuqt.shape), _const_spec(wuk.shape), _const_spec(wuvt.shape),
                  _const_spec(vone.shape)],
        out_specs=[col(MLA_HEADS * HEAD_PAD),
                   pl.BlockSpec((MLA_HEADS, tm, HEAD_PAD), lambda i: (0, i, 0)),
                   col(MLA_HEADS * HEAD_PAD),
                   row(2 * QK_W), row(ML_OUT), col(ng), col(N_DIR * ng),
                   row(ML_OUT), row(2 * D_MODEL)],
        out_shape=[cshape, jax.ShapeDtypeStruct((MLA_HEADS, s, HEAD_PAD), BF16), cshape,
                   jax.ShapeDtypeStruct((s, 2 * QK_W), F32),
                   jax.ShapeDtypeStruct((s, ML_OUT), BF16),
                   jax.ShapeDtypeStruct((ng, s), F32),
                   jax.ShapeDtypeStruct((N_DIR * ng, s), F32),
                   jax.ShapeDtypeStruct((s, ML_OUT), BF16),
                   jax.ShapeDtypeStruct((s, 2 * D_MODEL), BF16)],
        compiler_params=_params(("parallel",)),
        name="proj",
    )(h, pos, invf, pre, win, wsm, gbias, qn, kvn, wuqt, wuk, wuvt, vone)


BF16_ROWS = 16
V_ROWS = -(-(MLA_V + 1) // BF16_ROWS) * BF16_ROWS


def _attn_kernel(qt_ref, k_ref, vt_ref, o_ref, *scratch, tk, tq, steps):
    n = k_ref.shape[1] // tk
    nq = qt_ref.shape[1] // tq

    def tile(a):
        acc_sc, st0, st1, p0, p1 = scratch[5 * a:5 * a + 5]
        st, pb = (st0, st1), (p0, p1)
        qt = qt_ref[:, a * tq:(a + 1) * tq]

        def accumulate(t_prev, slot, alpha):
            off = pl.multiple_of(t_prev * tk, tk)
            vt = vt_ref[0:V_ROWS, pl.ds(off, tk)]
            acc_sc[...] = alpha * acc_sc[...] + _dot(vt, pb[slot][...])

        def prologue():
            acc_sc[...] = jnp.zeros(acc_sc.shape, F32)
            p1[...] = jnp.zeros(p1.shape, BF16)
            s0 = _dot(k_ref[0, 0:tk, :], qt)
            st0[...] = s0
            return jnp.max(s0, axis=0, keepdims=True)

        def step(t, slot, m, alpha_prev, cmax):
            off = pl.multiple_of(jnp.minimum(t + 1, n - 1) * tk, tk)
            s_next = _dot(k_ref[0, pl.ds(off, tk), :], qt)
            st[1 - slot][...] = s_next
            cmax_next = jnp.max(s_next, axis=0, keepdims=True)
            accumulate(jnp.maximum(t - 1, 0), 1 - slot, alpha_prev)
            m_new = jnp.maximum(m, cmax)
            pb[slot][...] = jnp.exp2(st[slot][...] - m_new).astype(BF16)
            return m_new, jnp.exp2(m - m_new), cmax_next

        def body(i, carry):
            m, alpha, cmax = carry
            for j in range(steps):
                m, alpha, cmax = step(i * steps + j, j % 2, m, alpha, cmax)
            return m, alpha, cmax

        def loop(cmax0):
            init = (jnp.full((1, tq), -jnp.inf, F32), jnp.ones((1, tq), F32), cmax0)
            return lax.fori_loop(0, n // steps, body, init)[1]

        def epilogue(alpha):
            accumulate(n - 1, (n - 1) % 2, alpha)
            acc = acc_sc[...]
            o_ref[:, a * tq:(a + 1) * tq] = (acc[0:MLA_V] / acc[MLA_V:MLA_V + 1]).astype(o_ref.dtype)

        return prologue, loop, epilogue

    tiles = [tile(a) for a in range(nq)]
    cmax0 = tiles[0][0]()
    for a in range(nq):
        alpha = tiles[a][1](cmax0)
        if a + 1 < nq:
            cmax0 = tiles[a + 1][0]()
        tiles[a][2](alpha)


def _attn(qt, k, vt, tq=512, nq=4, tk=256, steps=32):
    nh, s, w = k.shape
    assert steps % 2 == 0 and (s // tk) % steps == 0
    sub_scratch = [pltpu.VMEM((V_ROWS, tq), F32),
                   pltpu.VMEM((tk, tq), F32), pltpu.VMEM((tk, tq), F32),
                   pltpu.VMEM((tk, tq), BF16), pltpu.VMEM((tk, tq), BF16)]
    return pl.pallas_call(
        functools.partial(_attn_kernel, tk=tk, tq=tq, steps=steps), grid=(nh, s // (nq * tq)),
        in_specs=[pl.BlockSpec((w, nq * tq), lambda h, i: (h, i)),
                  pl.BlockSpec((1, s, w), lambda h, i: (h, 0, 0)),
                  pl.BlockSpec((w, s), lambda h, i: (h, 0))],
        out_specs=pl.BlockSpec((MLA_V, nq * tq), lambda h, i: (h, i)),
        out_shape=jax.ShapeDtypeStruct((nh * MLA_V, s), BF16),
        scratch_shapes=sub_scratch * nq,
        compiler_params=_params(("parallel", "parallel")),
        name="attn",
    )(qt, k, vt)


CONV_HALO = 8


def _conv_kernel(x_ref, prev_ref, next_ref, w_ref, b_ref, q_ref, kt_ref, buf, *, tm):
    i = pl.program_id(0)
    n = pl.num_programs(0)
    buf[CONV_HALO:CONV_HALO + tm, :] = x_ref[...]
    buf[0:CONV_HALO, :] = jnp.where(i > 0, prev_ref[...], 0.0)
    buf[CONV_HALO + tm:, :] = jnp.where(i < n - 1, next_ref[...], 0.0)
    acc = jnp.broadcast_to(b_ref[...], (tm, x_ref.shape[1]))
    for j in range(CONV_W):
        start = CONV_HALO - CONV_W // 2 + j
        acc = acc + w_ref[j:j + 1, :] * buf[start:start + tm, :]
    y = acc * _sigmoid(acc)
    pad = jnp.zeros((tm, HEAD_PAD - ML_QK), F32)
    q_ref[...] = jnp.concatenate(
        [piece for hd in range(ML_HEADS) for piece in (y[:, hd * ML_QK:(hd + 1) * ML_QK], pad)],
        axis=1).astype(BF16)
    kt_ref[...] = (y[:, QK_W:] * ML_QK ** -0.5).T


def _conv(x, w, b, tm=2048):
    s, c = x.shape
    r = tm // CONV_HALO
    nblk = s // CONV_HALO
    return pl.pallas_call(
        functools.partial(_conv_kernel, tm=tm), grid=(s // tm,),
        in_specs=[pl.BlockSpec((tm, c), lambda i: (i, 0)),
                  pl.BlockSpec((CONV_HALO, c), lambda i: (jnp.maximum(i * r - 1, 0), 0)),
                  pl.BlockSpec((CONV_HALO, c), lambda i: (jnp.minimum((i + 1) * r, nblk - 1), 0)),
                  _const_spec(w.shape), _const_spec(b.shape)],
        out_specs=[pl.BlockSpec((tm, QK_PAD), lambda i: (i, 0)),
                   pl.BlockSpec((QK_W, tm), lambda i: (0, i))],
        out_shape=[jax.ShapeDtypeStruct((s, QK_PAD), BF16),
                   jax.ShapeDtypeStruct((QK_W, s), F32)],
        scratch_shapes=[pltpu.VMEM((tm + 2 * CONV_HALO, c), F32)],
        compiler_params=_params(("parallel",)),
        name="conv",
    )(x, x, x, w, b)


def _mlstm_kernel(qf_ref, ktf_ref, vf_ref, gf_ref, cumf_ref, qb_ref, ktb_ref, vb_ref, gb_ref,
                  cumb_ref, hf_ref, hb_ref, c_sc, m_sc, *, chunks):
    L = ML_CHUNK

    @pl.when(pl.program_id(0) == 0)
    def _():
        c_sc[...] = jnp.zeros(c_sc.shape, F32)
        m_sc[...] = jnp.zeros(m_sc.shape, F32)

    row = lax.broadcasted_iota(jnp.int32, (L, L), 0)
    col = lax.broadcasted_iota(jnp.int32, (L, L), 1)
    mask = [col <= row, col >= row]
    ones_blk = jnp.ones((L, LANES), BF16)
    k_pad = jnp.zeros((HEAD_PAD - ML_QK, L), BF16)
    c_pad = jnp.zeros((HEAD_PAD - ML_QK, ML_V + LANES), BF16)
    refs = ((qf_ref, ktf_ref, vf_ref, gf_ref, hf_ref), (qb_ref, ktb_ref, vb_ref, gb_ref, hb_ref))
    cum_refs = (cumf_ref, cumb_ref)
    heads = [(d, hd) for d in range(N_DIR) for hd in range(ML_HEADS)]
    blk = lambda hd: slice(hd * HEAD_PAD, (hd + 1) * HEAD_PAD)
    kblk = lambda hd: slice(hd * ML_QK, (hd + 1) * ML_QK)

    wins = [[slice(lc * L, (lc + 1) * L) for lc in (step, chunks - 1 - step)]
            for step in range(chunks)]
    gates = [[refs[d][3][:, win[d]] for d in range(N_DIR)] for win in wins]
    cums = [[cum_refs[d][:, win[d]] for d in range(N_DIR)] for win in wins]

    q, kt, v_aug, qk, qc, m_prev = [{} for _ in range(6)]
    for step, win in enumerate(wins):
        gate, cum = gates[step], cums[step]
        for d, hd in heads:
            j = d * ML_HEADS + hd
            c = (step, j)
            q[c] = refs[d][0][win[d], blk(hd)]
            kt[c] = refs[d][1][kblk(hd), win[d]]
            v_aug[c] = jnp.concatenate([refs[d][2][win[d], blk(hd)], ones_blk], axis=1)
            qk[c] = _dot(q[c], jnp.concatenate([kt[c].astype(BF16), k_pad], axis=0))
            qc[c] = _dot(q[c], jnp.concatenate([c_sc[j].astype(BF16), c_pad], axis=0))
        for d, hd in heads:
            j = d * ML_HEADS + hd
            c = (step, j)
            br = cum[d][N_GATE + j:N_GATE + j + 1, :]
            ir = gate[d][j:j + 1, :]
            m_prev[c] = m_sc[j][0:1, 0:1]
            b_end = br[:, L - 1:L] if d == 0 else br[:, 0:1]
            g_row = b_end - br + ir
            m_new = jnp.maximum(b_end + m_prev[c], jnp.max(g_row, axis=1, keepdims=True))
            decay = jnp.exp(b_end + m_prev[c] - m_new)
            kw_t = (kt[c] * jnp.exp(g_row - m_new)).astype(BF16)
            c_sc[j] = decay * c_sc[j] + _dot(kw_t, v_aug[c])
            m_sc[j] = jnp.broadcast_to(m_new, m_sc.shape[1:])

    chains = [(step, d, hd) for step in range(chunks) for d, hd in heads]
    a, mm, mm_b, bc, sv = {}, {}, {}, {}, {}
    for step, d, hd in chains:
        j = d * ML_HEADS + hd
        c = (step, j)
        br = cums[step][d][N_GATE + j:N_GATE + j + 1, :]
        lf = gates[step][d][N_GATE + j:N_GATE + j + 1, :]
        a[c] = jnp.where(mask[d], gates[step][d][j:j + 1, :] - br, -jnp.inf)
        mm[c] = jnp.maximum(m_prev[c], jnp.max(a[c], axis=1, keepdims=True))
        bc[c] = jnp.sum(jnp.where(mask[d], lf, 0.0), axis=1, keepdims=True)
    for step, d, hd in chains:
        c = (step, d * ML_HEADS + hd)
        mm_b[c] = jnp.broadcast_to(mm[c], (L, L))
        smat = qk[c] * jnp.exp(a[c] - mm_b[c])
        sv[c] = _dot(smat.astype(BF16), v_aug[c])
    for step, d, hd in chains:
        c = (step, d * ML_HEADS + hd)
        inter = jnp.exp(m_prev[c] - mm_b[c])
        nv = sv[c] + jnp.concatenate([inter, inter], axis=1) * qc[c]
        den = nv[:, ML_V:ML_V + 1]
        inv = 1.0 / jnp.maximum(jnp.abs(den), jnp.exp(-(bc[c] + mm[c])))
        refs[d][4][wins[step][d], blk(hd)] = (nv[:, :ML_V] * inv).astype(BF16)


def _mlstm(q, kt, v, gate_t, cum_t, chunks=8):
    s = q.shape[0]
    tb = chunks * ML_CHUNK
    nb = s // tb
    ng = 2 * N_GATE
    rows = lambda w, f: pl.BlockSpec((tb, w), lambda c: (f(c), 0))
    cols = lambda h, f, r=0: pl.BlockSpec((h, tb), lambda c: (r, f(c)))
    fwd = lambda c: c
    bwd = lambda c: nb - 1 - c
    hshape = jax.ShapeDtypeStruct((s, ML_OUT), BF16)
    return pl.pallas_call(
        functools.partial(_mlstm_kernel, chunks=chunks), grid=(nb,),
        in_specs=[rows(QK_PAD, fwd), cols(QK_W, fwd), rows(ML_OUT, fwd),
                  cols(ng, fwd), cols(ng, fwd, 0),
                  rows(QK_PAD, bwd), cols(QK_W, bwd), rows(ML_OUT, bwd),
                  cols(ng, bwd), cols(ng, bwd, 1)],
        out_specs=[rows(ML_OUT, fwd), rows(ML_OUT, bwd)],
        out_shape=[hshape, hshape],
        scratch_shapes=[pltpu.VMEM((N_GATE, ML_QK, ML_V + LANES), F32),
                        pltpu.VMEM((N_GATE, 8, LANES), F32)],
        compiler_params=_params(("arbitrary",)),
        name="mlstm",
    )(q, kt, v, gate_t, cum_t, q, kt, v, gate_t, cum_t)


def _merge_kernel(h_ref, att_ref, hf_ref, hb_ref, mlo_ref, gates_ref, hn_ref,
                  wba_ref, wbm_ref, wout_ref, post_ref, o_ref):
    a = lax.dot_general(att_ref[...], wba_ref[...], (((0,), (0,)), ((), ())),
                        preferred_element_type=F32)
    hm = hf_ref[...].astype(F32) + hb_ref[...].astype(F32)
    heads = []
    for hd in range(ML_HEADS):
        blk = slice(hd * ML_V, (hd + 1) * ML_V)
        heads.append(_rms(hm[:, blk], hn_ref[:, blk]))
    hm = jnp.concatenate(heads, axis=1) * mlo_ref[...]
    bm = _dot(hm.astype(BF16), wbm_ref[...])
    g = gates_ref[...]
    mixed = _dot((g[:, :D_MODEL] * a + g[:, D_MODEL:] * bm).astype(BF16), wout_ref[...])
    o_ref[...] = h_ref[...] + _rms(mixed, post_ref[...])


def _merge(h, att, hf, hb, mlo, gates, hn, wba, wbm, wout, post, tm=1024):
    s, d = h.shape
    row = lambda w: pl.BlockSpec((tm, w), lambda i: (i, 0))
    return pl.pallas_call(
        _merge_kernel, grid=(s // tm,),
        in_specs=[row(d), pl.BlockSpec((att.shape[0], tm), lambda i: (0, i)),
                  row(ML_OUT), row(ML_OUT), row(ML_OUT),
                  row(2 * d), _const_spec(hn.shape), _const_spec(wba.shape),
                  _const_spec(wbm.shape), _const_spec(wout.shape), _const_spec(post.shape)],
        out_specs=row(d),
        out_shape=jax.ShapeDtypeStruct((s, d), F32),
        compiler_params=_params(("parallel",)),
        name="merge",
    )(h, att, hf, hb, mlo, gates, hn, wba, wbm, wout, post)


def _pad_heads(w, nh, width):
    r = w.shape[0]
    w = w.reshape(r, nh, width)
    return jnp.pad(w, ((0, 0), (0, 0), (0, HEAD_PAD - width))).reshape(r, nh * HEAD_PAD)


def _permute_w_in(w_in):
    d = w_in.shape[0]
    small = jnp.zeros((LANES, d), w_in.dtype)
    small = small.at[0:N_GATE].set(w_in[:, _O_MLI:_O_MLF].T)
    small = small.at[N_GATE:2 * N_GATE].set(w_in[:, _O_MLF:_O_MLO].T)
    small = small.at[ROPE_LO:ROPE_LO + MLA_ROPE].set(w_in[:, _O_KR:_O_MLQ].T)
    main = jnp.concatenate([w_in[:, _O_CQ:_O_KR].astype(BF16), w_in[:, _O_MLQ:_O_MLI].astype(BF16),
                            w_in[:, _O_MLO:_O_END].astype(BF16)], axis=1)
    return main, small.astype(BF16)


def kernel(x, p, positions, ffn1_pre_norm, ffn1_post_norm, ffn1_w_gate, ffn1_w_up, ffn1_w_down, mix_pre_norm, mix_post_norm, w_in, mla_q_norm, mla_kv_norm, mla_w_uq, mla_w_uk, mla_w_uv, ml_conv_w, ml_conv_b, ml_i_bias, ml_f_bias, ml_head_norm, w_branch_mla, w_branch_ml, w_out, ffn2_pre_norm, ffn2_post_norm, ffn2_w_gate, ffn2_w_up, ffn2_w_down, ple_pre_norm, ple_post_norm, ple_w_proj, ple_w_gate):
    depth, batch = p.shape[0], x.shape[0]
    assert depth == 1 and batch == 1, "kernel is specialised to DEPTH == 1, BATCH == 1"
    bf = lambda w: w.astype(BF16)
    h = x[0]
    pos = positions

    inv_freq = ROPE_THETA ** (-jnp.arange(0, MLA_ROPE, 2, dtype=F32) / MLA_ROPE)
    invf = inv_freq[:, None]

    i = 0
    h = _ffn(h, ffn1_pre_norm[i][None], ffn1_post_norm[i][None],
             ffn1_w_gate[i], ffn1_w_up[i], ffn1_w_down[i])

    vone = jnp.zeros((MLA_HEADS, HEAD_PAD), F32).at[:, MLA_V].set(1.0).reshape(-1, 1)
    win_main, win_small = _permute_w_in(w_in[i])
    gate_bias = jnp.concatenate([ml_i_bias[i].reshape(-1), ml_f_bias[i].reshape(-1)])[:, None]
    qt, k, vt, mlqk, mlv, gate_t, cum_t, mlo, gates = _proj(
        h, pos, invf, mix_pre_norm[i][None], win_main, win_small,
        gate_bias, mla_q_norm[i][None], mla_kv_norm[i][None],
        bf(_pad_heads(mla_w_uq[i], MLA_HEADS, MLA_NOPE + MLA_ROPE)).T,
        bf(_pad_heads(mla_w_uk[i], MLA_HEADS, MLA_NOPE)),
        bf(_pad_heads(mla_w_uv[i], MLA_HEADS, MLA_V)).T, vone)

    att = _attn(qt, k, vt)

    conv_w = jnp.pad(ml_conv_w[i], ((0, 8 - CONV_W), (0, 0)))
    conv_b = ml_conv_b[i][None]
    q_conv, kt_conv = _conv(mlqk, conv_w, conv_b)

    hf, hb = _mlstm(q_conv, kt_conv, mlv, gate_t, cum_t)

    h = _merge(h, att, hf, hb, mlo, gates, ml_head_norm[i][None], bf(w_branch_mla[i]),
               bf(w_branch_ml[i]), bf(w_out[i]), mix_post_norm[i][None])

    h = _ffn(h, ffn2_pre_norm[i][None], ffn2_post_norm[i][None],
             ffn2_w_gate[i], ffn2_w_up[i], ffn2_w_down[i],
             ple=(p[i, 0], ple_pre_norm[i][None], ple_post_norm[i][None],
                  bf(ple_w_proj[i]), bf(ple_w_gate[i])))
    return h[None]
```

```python
import functools
import math

import jax
import jax.numpy as jnp
from jax import lax
from jax.experimental import pallas as pl
from jax.experimental.pallas import tpu as pltpu

D_MODEL = 1024
EPS = 1e-6
MLA_HEADS = 8
MLA_Q_RANK = 256
MLA_KV_RANK = 256
MLA_NOPE = 64
MLA_ROPE = 32
MLA_V = 64
ROPE_THETA = 10000.0
ML_HEADS = 4
ML_QK = 64
ML_V = 128
ML_CHUNK = 128
CONV_W = 5
N_DIR = 2
MLA_OUT = MLA_HEADS * MLA_V
ML_OUT = ML_HEADS * ML_V

LANES = 128
HEAD_PAD = LANES
ROPE_LO = MLA_NOPE
ROPE_HALF = MLA_ROPE // 2
VMEM_LIMIT = 56 * 1024 * 1024

F32 = jnp.float32
BF16 = jnp.bfloat16
LOG2E = math.log2(math.e)
Q_SCALE = (MLA_NOPE + MLA_ROPE) ** -0.5 * LOG2E

_O_CQ = 0
_O_CKV = _O_CQ + MLA_Q_RANK
_O_KR = _O_CKV + MLA_KV_RANK
_O_MLQ = _O_KR + MLA_ROPE
_O_MLK = _O_MLQ + ML_HEADS * ML_QK
_O_MLV = _O_MLK + ML_HEADS * ML_QK
_O_MLI = _O_MLV + ML_OUT
_O_MLF = _O_MLI + N_DIR * ML_HEADS
_O_MLO = _O_MLF + N_DIR * ML_HEADS
_O_GATES = _O_MLO + ML_OUT
_O_END = _O_GATES + 2 * D_MODEL

N_GATE = N_DIR * ML_HEADS
QK_W = ML_HEADS * ML_QK
QK_PAD = ML_HEADS * HEAD_PAD

_P_CQ = 0
_P_CKV = _P_CQ + MLA_Q_RANK
_P_MLQK = _P_CKV + MLA_KV_RANK
_P_MLV = _P_MLQK + 2 * QK_W
_P_MLO = _P_MLV + ML_OUT
_P_GATES = _P_MLO + ML_OUT
_P_END = _P_GATES + 2 * D_MODEL


def _rms(x, w):
    ms = jnp.mean(x * x, axis=-1, keepdims=True)
    return x * lax.rsqrt(ms + EPS) * w


def _sigmoid(x):
    return 1.0 / (1.0 + jnp.exp(-x))


def _dot(a, b):
    return jnp.dot(a, b, preferred_element_type=F32)


def _const_spec(shape):
    nd = len(shape)
    return pl.BlockSpec(shape, lambda *_: (0,) * nd, pipeline_mode=pl.Buffered(1))


def _params(sem):
    return pltpu.CompilerParams(dimension_semantics=sem, vmem_limit_bytes=VMEM_LIMIT)


def _ffn_body(x, pre, post, wg, wu, wd):
    u = _rms(x, pre).astype(BF16)
    g = _dot(u, wg)
    up = _dot(u, wu)
    hid = (g * _sigmoid(g) * up).astype(BF16)
    y = _dot(hid, wd)
    return x + 0.5 * _rms(y, post)


STAGE_CHUNKS = 8
STAGE_SLOTS = 4


def _stage_weights(w_hbm, w_sc, stage, sem):
    jobs = [(w, sc, st, c) for w, sc, st in zip(w_hbm, w_sc, stage) for c in range(STAGE_CHUNKS)]
    ahead = STAGE_SLOTS - 1

    def copy(i):
        w, _, st, c = jobs[i]
        rows, slot = st.shape[1], i % STAGE_SLOTS
        return pltpu.make_async_copy(w.at[pl.ds(c * rows, rows), :], st.at[slot], sem.at[slot])

    for i in range(min(ahead, len(jobs))):
        copy(i).start()
    for i, (_, sc, st, c) in enumerate(jobs):
        if i + ahead < len(jobs):
            copy(i + ahead).start()
        copy(i).wait()
        rows = st.shape[1]
        sc[c * rows:(c + 1) * rows, :] = st[i % STAGE_SLOTS].astype(BF16)


def _ffn_weights(w_hbm, scratch):
    wg_sc, wu_sc, wd_sc, stage_wide, stage_tall, sem = scratch

    @pl.when(pl.program_id(0) == 0)
    def _():
        _stage_weights(w_hbm, (wg_sc, wu_sc, wd_sc), (stage_wide, stage_wide, stage_tall), sem)

    return wg_sc[...], wu_sc[...], wd_sc[...]


def _ffn_kernel(x_ref, pre_ref, post_ref, wg_hbm, wu_hbm, wd_hbm, o_ref, *scratch):
    o_ref[...] = _ffn_body(x_ref[...], pre_ref[...], post_ref[...],
                           *_ffn_weights((wg_hbm, wu_hbm, wd_hbm), scratch))


def _ffn_ple_kernel(x_ref, pre_ref, post_ref, wg_hbm, wu_hbm, wd_hbm,
                    p_ref, ppre_ref, ppost_ref, wproj_ref, wgate_ref, o_ref, *scratch):
    h = _ffn_body(x_ref[...], pre_ref[...], post_ref[...],
                  *_ffn_weights((wg_hbm, wu_hbm, wd_hbm), scratch))
    e = _dot(p_ref[...].astype(BF16), wproj_ref[...])
    g = _sigmoid(_dot(_rms(h, ppre_ref[...]).astype(BF16), wgate_ref[...]))
    o_ref[...] = h + _rms(g * e, ppost_ref[...])


def _ffn(x, pre, post, wg, wu, wd, ple=None, tm=512):
    s, d = x.shape
    f = wg.shape[1]
    row = pl.BlockSpec((tm, d), lambda i: (i, 0))
    hbm = pl.BlockSpec(memory_space=pl.ANY)
    in_specs = [row, _const_spec((1, d)), _const_spec((1, d)), hbm, hbm, hbm]
    args = [x, pre, post, wg, wu, wd]
    kern = _ffn_kernel
    if ple is not None:
        p, ppre, ppost, wproj, wgate = ple
        in_specs += [pl.BlockSpec((tm, p.shape[1]), lambda i: (i, 0)),
                     _const_spec((1, d)), _const_spec((1, d)),
                     _const_spec(wproj.shape), _const_spec(wgate.shape)]
        args += [p, ppre, ppost, wproj, wgate]
        kern = _ffn_ple_kernel
    return pl.pallas_call(
        kern, grid=(s // tm,), in_specs=in_specs, out_specs=row,
        out_shape=jax.ShapeDtypeStruct((s, d), F32),
        scratch_shapes=[pltpu.VMEM((d, f), BF16), pltpu.VMEM((d, f), BF16), pltpu.VMEM((f, d), BF16),
                        pltpu.VMEM((STAGE_SLOTS, d // STAGE_CHUNKS, f), F32),
                        pltpu.VMEM((STAGE_SLOTS, f // STAGE_CHUNKS, d), F32),
                        pltpu.SemaphoreType.DMA((STAGE_SLOTS,))],
        compiler_params=_params(("arbitrary",)),
        name="ffn" if ple is None else "ffn_ple",
    )(*args)


def _dot_nt(a, b):
    return lax.dot_general(a, b, (((1,), (1,)), ((), ())), preferred_element_type=F32)


def _rope_tables_t(pos_ref, invf_ref):
    ang = pos_ref[...].astype(F32) * invf_ref[...]
    return jnp.cos(ang), jnp.sin(ang)


def _rope_t(x, c, s):
    x1, x2 = x[:ROPE_HALF], x[ROPE_HALF:]
    return x1 * c - x2 * s, x2 * c + x1 * s


def _proj_kernel(h_ref, pos_ref, invf_ref, pre_ref, win_ref, wsm_ref, gbias_ref, qn_ref, kvn_ref,
                 wuqt_ref, wuk_ref, wuvt_ref, vone_ref,
                 qt_ref, k_ref, vt_ref, mlqk_ref, mlv_ref, gt_ref, cum_ref, mlo_ref, gates_ref):
    u = _rms(h_ref[...], pre_ref[...]).astype(BF16)
    zdot = lambda a, b: _dot(u, win_ref[:, a:b])
    c, s = _rope_tables_t(pos_ref, invf_ref)
    rope = slice(ROPE_LO, ROPE_LO + MLA_ROPE)
    small_t = _dot_nt(wsm_ref[...], u)
    z_c = zdot(_P_CQ, _P_MLQK)
    gates_ref[...] = _sigmoid(zdot(_P_GATES, _P_END)).astype(BF16)

    pre = small_t[0:2 * N_GATE, :] + gbias_ref[...]
    is_f = lax.broadcasted_iota(jnp.int32, (2 * N_GATE, 1), 0) >= N_GATE
    gate = jnp.where(is_f, jax.nn.log_sigmoid(pre), pre)
    gt_ref[...] = gate
    width = gate.shape[1]
    pos_in_chunk = lax.broadcasted_iota(jnp.int32, (1, width), 1) % ML_CHUNK
    prefix, suffix, shift = gate, gate, 1
    while shift < ML_CHUNK:
        prefix = prefix + jnp.where(pos_in_chunk >= shift, pltpu.roll(prefix, shift, 1), 0.0)
        suffix = suffix + jnp.where(pos_in_chunk < ML_CHUNK - shift,
                                    pltpu.roll(suffix, width - shift, 1), 0.0)
        shift *= 2
    cum_ref[0:2 * N_GATE, :] = prefix
    cum_ref[2 * N_GATE:, :] = suffix

    tm = small_t.shape[1]
    k_rope = jnp.concatenate(
        [jnp.zeros((ROPE_LO, tm), F32), *_rope_t(small_t[rope, :], c, s),
         jnp.zeros((HEAD_PAD - ROPE_LO - MLA_ROPE, tm), F32)], axis=0).T

    cq = _rms(z_c[:, :MLA_Q_RANK], qn_ref[...]).astype(BF16)
    ckv = _rms(z_c[:, MLA_Q_RANK:], kvn_ref[...]).astype(BF16)
    qpt = _dot_nt(wuqt_ref[...], cq)
    kp = _dot(ckv, wuk_ref[...])
    vt_ref[...] = (_dot_nt(wuvt_ref[...], ckv) + vone_ref[...]).astype(BF16)
    mlqk_ref[...] = zdot(_P_MLQK, _P_MLV)
    mlv_ref[...] = zdot(_P_MLV, _P_MLO).astype(BF16)
    mlo_ref[...] = _sigmoid(zdot(_P_MLO, _P_GATES)).astype(BF16)
    for hd in range(MLA_HEADS):
        blk = slice(hd * HEAD_PAD, (hd + 1) * HEAD_PAD)
        x = qpt[blk, :]
        r1, r2 = _rope_t(x[rope, :], c, s)
        qt_ref[blk, :] = jnp.concatenate(
            [x[:ROPE_LO] * Q_SCALE, r1 * Q_SCALE, r2 * Q_SCALE, x[ROPE_LO + MLA_ROPE:]],
            axis=0).astype(BF16)
        k_ref[hd] = (kp[:, blk] + k_rope).astype(BF16)


def _proj(h, pos, invf, pre, win, wsm, gbias, qn, kvn, wuqt, wuk, wuvt, vone, tm=1024):
    s, d = h.shape
    ng = 2 * N_GATE
    row = lambda w: pl.BlockSpec((tm, w), lambda i: (i, 0))
    col = lambda r: pl.BlockSpec((r, tm), lambda i: (0, i))
    cshape = jax.ShapeDtypeStruct((MLA_HEADS * HEAD_PAD, s), BF16)
    return pl.pallas_call(
        _proj_kernel, grid=(s // tm,),
        in_specs=[row(d), col(1), _const_spec(invf.shape), _const_spec((1, d)),
                  _const_spec(win.shape), _const_spec(wsm.shape), _const_spec(gbias.shape),
                  _const_spec(qn.shape), _const_spec(kvn.shape),
                  _const_spec(wuqt.shape), _const_spec(wuk.shape), _const_spec(wuvt.shape),
                  _const_spec(vone.shape)],
        out_specs=[col(MLA_HEADS * HEAD_PAD),
                   pl.BlockSpec((MLA_HEADS, tm, HEAD_PAD), lambda i: (0, i, 0)),
                   col(MLA_HEADS * HEAD_PAD),
                   row(2 * QK_W), row(ML_OUT), col(ng), col(N_DIR * ng),
                   row(ML_OUT), row(2 * D_MODEL)],
        out_shape=[cshape, jax.ShapeDtypeStruct((MLA_HEADS, s, HEAD_PAD), BF16), cshape,
                   jax.ShapeDtypeStruct((s, 2 * QK_W), F32),
                   jax.ShapeDtypeStruct((s, ML_OUT), BF16),
                   jax.ShapeDtypeStruct((ng, s), F32),
                   jax.ShapeDtypeStruct((N_DIR * ng, s), F32),
                   jax.ShapeDtypeStruct((s, ML_OUT), BF16),
                   jax.ShapeDtypeStruct((s, 2 * D_MODEL), BF16)],
        compiler_params=_params(("parallel",)),
        name="proj",
    )(h, pos, invf, pre, win, wsm, gbias, qn, kvn, wuqt, wuk, wuvt, vone)


BF16_ROWS = 16
V_ROWS = -(-(MLA_V + 1) // BF16_ROWS) * BF16_ROWS


def _attn_kernel(qt_ref, k_ref, vt_ref, o_ref, *scratch, tk, tq, steps):
    n = k_ref.shape[1] // tk
    nq = qt_ref.shape[1] // tq

    def tile(a):
        acc_sc, st0, st1, p0, p1 = scratch[5 * a:5 * a + 5]
        st, pb = (st0, st1), (p0, p1)
        qt = qt_ref[:, a * tq:(a + 1) * tq]

        def accumulate(t_prev, slot, alpha):
            off = pl.multiple_of(t_prev * tk, tk)
            vt = vt_ref[0:V_ROWS, pl.ds(off, tk)]
            acc_sc[...] = alpha * acc_sc[...] + _dot(vt, pb[slot][...])

        def prologue():
            acc_sc[...] = jnp.zeros(acc_sc.shape, F32)
            p1[...] = jnp.zeros(p1.shape, BF16)
            st0[...] = _dot(k_ref[0, 0:tk, :], qt)

        def step(t, slot, m, alpha_prev):
            off = pl.multiple_of(jnp.minimum(t + 1, n - 1) * tk, tk)
            st[1 - slot][...] = _dot(k_ref[0, pl.ds(off, tk), :], qt)
            accumulate(jnp.maximum(t - 1, 0), 1 - slot, alpha_prev)
            x = st[slot][...]
            m_new = jnp.maximum(m, jnp.max(x, axis=0, keepdims=True))
            pb[slot][...] = jnp.exp2(x - m_new).astype(BF16)
            return m_new, jnp.exp2(m - m_new)

        def body(i, carry):
            m, alpha = carry
            for j in range(steps):
                m, alpha = step(i * steps + j, j % 2, m, alpha)
            return m, alpha

        def loop():
            init = (jnp.full((1, tq), -jnp.inf, F32), jnp.ones((1, tq), F32))
            return lax.fori_loop(0, n // steps, body, init)[1]

        def epilogue(alpha):
            accumulate(n - 1, (n - 1) % 2, alpha)
            acc = acc_sc[...]
            o_ref[:, a * tq:(a + 1) * tq] = (acc[0:MLA_V] / acc[MLA_V:MLA_V + 1]).astype(o_ref.dtype)

        return prologue, loop, epilogue

    tiles = [tile(a) for a in range(nq)]
    tiles[0][0]()
    for a in range(nq):
        alpha = tiles[a][1]()
        if a + 1 < nq:
            tiles[a + 1][0]()
        tiles[a][2](alpha)


def _attn(qt, k, vt, tq=512, nq=4, tk=256, steps=32):
    nh, s, w = k.shape
    assert steps % 2 == 0 and (s // tk) % steps == 0
    sub_scratch = [pltpu.VMEM((V_ROWS, tq), F32),
                   pltpu.VMEM((tk, tq), F32), pltpu.VMEM((tk, tq), F32),
                   pltpu.VMEM((tk, tq), BF16), pltpu.VMEM((tk, tq), BF16)]
    return pl.pallas_call(
        functools.partial(_attn_kernel, tk=tk, tq=tq, steps=steps), grid=(nh, s // (nq * tq)),
        in_specs=[pl.BlockSpec((w, nq * tq), lambda h, i: (h, i)),
                  pl.BlockSpec((1, s, w), lambda h, i: (h, 0, 0)),
                  pl.BlockSpec((w, s), lambda h, i: (h, 0))],
        out_specs=pl.BlockSpec((MLA_V, nq * tq), lambda h, i: (h, i)),
        out_shape=jax.ShapeDtypeStruct((nh * MLA_V, s), BF16),
        scratch_shapes=sub_scratch * nq,
        compiler_params=_params(("parallel", "parallel")),
        name="attn",
    )(qt, k, vt)


CONV_HALO = 8


def _conv_kernel(x_ref, prev_ref, next_ref, w_ref, b_ref, q_ref, kt_ref, buf, *, tm):
    i = pl.program_id(0)
    n = pl.num_programs(0)
    buf[CONV_HALO:CONV_HALO + tm, :] = x_ref[...]
    buf[0:CONV_HALO, :] = jnp.where(i > 0, prev_ref[...], 0.0)
    buf[CONV_HALO + tm:, :] = jnp.where(i < n - 1, next_ref[...], 0.0)
    acc = jnp.broadcast_to(b_ref[...], (tm, x_ref.shape[1]))
    for j in range(CONV_W):
        start = CONV_HALO - CONV_W // 2 + j
        acc = acc + w_ref[j:j + 1, :] * buf[start:start + tm, :]
    y = acc * _sigmoid(acc)
    pad = jnp.zeros((tm, HEAD_PAD - ML_QK), F32)
    q_ref[...] = jnp.concatenate(
        [piece for hd in range(ML_HEADS) for piece in (y[:, hd * ML_QK:(hd + 1) * ML_QK], pad)],
        axis=1).astype(BF16)
    kt_ref[...] = (y[:, QK_W:] * ML_QK ** -0.5).T


def _conv(x, w, b, tm=2048):
    s, c = x.shape
    r = tm // CONV_HALO
    nblk = s // CONV_HALO
    return pl.pallas_call(
        functools.partial(_conv_kernel, tm=tm), grid=(s // tm,),
        in_specs=[pl.BlockSpec((tm, c), lambda i: (i, 0)),
                  pl.BlockSpec((CONV_HALO, c), lambda i: (jnp.maximum(i * r - 1, 0), 0)),
                  pl.BlockSpec((CONV_HALO, c), lambda i: (jnp.minimum((i + 1) * r, nblk - 1), 0)),
                  _const_spec(w.shape), _const_spec(b.shape)],
        out_specs=[pl.BlockSpec((tm, QK_PAD), lambda i: (i, 0)),
                   pl.BlockSpec((QK_W, tm), lambda i: (0, i))],
        out_shape=[jax.ShapeDtypeStruct((s, QK_PAD), BF16),
                   jax.ShapeDtypeStruct((QK_W, s), F32)],
        scratch_shapes=[pltpu.VMEM((tm + 2 * CONV_HALO, c), F32)],
        compiler_params=_params(("parallel",)),
        name="conv",
    )(x, x, x, w, b)


def _mlstm_kernel(qf_ref, ktf_ref, vf_ref, gf_ref, cumf_ref, qb_ref, ktb_ref, vb_ref, gb_ref,
                  cumb_ref, hf_ref, hb_ref, c_sc, m_sc, *, chunks):
    L = ML_CHUNK

    @pl.when(pl.program_id(0) == 0)
    def _():
        c_sc[...] = jnp.zeros(c_sc.shape, F32)
        m_sc[...] = jnp.zeros(m_sc.shape, F32)

    row = lax.broadcasted_iota(jnp.int32, (L, L), 0)
    col = lax.broadcasted_iota(jnp.int32, (L, L), 1)
    mask = [col <= row, col >= row]
    ones_blk = jnp.ones((L, LANES), BF16)
    k_pad = jnp.zeros((HEAD_PAD - ML_QK, L), BF16)
    c_pad = jnp.zeros((HEAD_PAD - ML_QK, ML_V + LANES), BF16)
    refs = ((qf_ref, ktf_ref, vf_ref, gf_ref, hf_ref), (qb_ref, ktb_ref, vb_ref, gb_ref, hb_ref))
    cum_refs = (cumf_ref, cumb_ref)
    heads = [(d, hd) for d in range(N_DIR) for hd in range(ML_HEADS)]
    blk = lambda hd: slice(hd * HEAD_PAD, (hd + 1) * HEAD_PAD)
    kblk = lambda hd: slice(hd * ML_QK, (hd + 1) * ML_QK)

    wins = [[slice(lc * L, (lc + 1) * L) for lc in (step, chunks - 1 - step)]
            for step in range(chunks)]
    gates = [[refs[d][3][:, win[d]] for d in range(N_DIR)] for win in wins]
    cums = [[cum_refs[d][:, win[d]] for d in range(N_DIR)] for win in wins]

    q, kt, v_aug, qk, qc, m_prev = [{} for _ in range(6)]
    for step, win in enumerate(wins):
        gate, cum = gates[step], cums[step]
        for d, hd in heads:
            j = d * ML_HEADS + hd
            c = (step, j)
            q[c] = refs[d][0][win[d], blk(hd)]
            kt[c] = refs[d][1][kblk(hd), win[d]]
            v_aug[c] = jnp.concatenate([refs[d][2][win[d], blk(hd)], ones_blk], axis=1)
            qk[c] = _dot(q[c], jnp.concatenate([kt[c].astype(BF16), k_pad], axis=0))
            qc[c] = _dot(q[c], jnp.concatenate([c_sc[j].astype(BF16), c_pad], axis=0))
        for d, hd in heads:
            j = d * ML_HEADS + hd
            c = (step, j)
            br = cum[d][N_GATE + j:N_GATE + j + 1, :]
            ir = gate[d][j:j + 1, :]
            m_prev[c] = m_sc[j][0:1, 0:1]
            b_end = br[:, L - 1:L] if d == 0 else br[:, 0:1]
            g_row = b_end - br + ir
            m_new = jnp.maximum(b_end + m_prev[c], jnp.max(g_row, axis=1, keepdims=True))
            decay = jnp.exp(b_end + m_prev[c] - m_new)
            kw_t = (kt[c] * jnp.exp(g_row - m_new)).astype(BF16)
            c_sc[j] = decay * c_sc[j] + _dot(kw_t, v_aug[c])
            m_sc[j] = jnp.broadcast_to(m_new, m_sc.shape[1:])

    chains = [(step, d, hd) for step in range(chunks) for d, hd in heads]
    a, mm, mm_b, bc, sv = {}, {}, {}, {}, {}
    for step, d, hd in chains:
        j = d * ML_HEADS + hd
        c = (step, j)
        br = cums[step][d][N_GATE + j:N_GATE + j + 1, :]
        lf = gates[step][d][N_GATE + j:N_GATE + j + 1, :]
        a[c] = jnp.where(mask[d], gates[step][d][j:j + 1, :] - br, -jnp.inf)
        mm[c] = jnp.maximum(m_prev[c], jnp.max(a[c], axis=1, keepdims=True))
        bc[c] = jnp.sum(jnp.where(mask[d], lf, 0.0), axis=1, keepdims=True)
    for step, d, hd in chains:
        c = (step, d * ML_HEADS + hd)
        mm_b[c] = jnp.broadcast_to(mm[c], (L, L))
        smat = qk[c] * jnp.exp(a[c] - mm_b[c])
        sv[c] = _dot(smat.astype(BF16), v_aug[c])
    for step, d, hd in chains:
        c = (step, d * ML_HEADS + hd)
        inter = jnp.exp(m_prev[c] - mm_b[c])
        nv = sv[c] + jnp.concatenate([inter, inter], axis=1) * qc[c]
        den = nv[:, ML_V:ML_V + 1]
        inv = 1.0 / jnp.maximum(jnp.abs(den), jnp.exp(-(bc[c] + mm[c])))
        refs[d][4][wins[step][d], blk(hd)] = (nv[:, :ML_V] * inv).astype(BF16)


def _mlstm(q, kt, v, gate_t, cum_t, chunks=8):
    s = q.shape[0]
    tb = chunks * ML_CHUNK
    nb = s // tb
    ng = 2 * N_GATE
    rows = lambda w, f: pl.BlockSpec((tb, w), lambda c: (f(c), 0))
    cols = lambda h, f, r=0: pl.BlockSpec((h, tb), lambda c: (r, f(c)))
    fwd = lambda c: c
    bwd = lambda c: nb - 1 - c
    hshape = jax.ShapeDtypeStruct((s, ML_OUT), BF16)
    return pl.pallas_call(
        functools.partial(_mlstm_kernel, chunks=chunks), grid=(nb,),
        in_specs=[rows(QK_PAD, fwd), cols(QK_W, fwd), rows(ML_OUT, fwd),
                  cols(ng, fwd), cols(ng, fwd, 0),
                  rows(QK_PAD, bwd), cols(QK_W, bwd), rows(ML_OUT, bwd),
                  cols(ng, bwd), cols(ng, bwd, 1)],
        out_specs=[rows(ML_OUT, fwd), rows(ML_OUT, bwd)],
        out_shape=[hshape, hshape],
        scratch_shapes=[pltpu.VMEM((N_GATE, ML_QK, ML_V + LANES), F32),
                        pltpu.VMEM((N_GATE, 8, LANES), F32)],
        compiler_params=_params(("arbitrary",)),
        name="mlstm",
    )(q, kt, v, gate_t, cum_t, q, kt, v, gate_t, cum_t)


def _merge_kernel(h_ref, att_ref, hf_ref, hb_ref, mlo_ref, gates_ref, hn_ref,
                  wba_ref, wbm_ref, wout_ref, post_ref, o_ref):
    a = lax.dot_general(att_ref[...], wba_ref[...], (((0,), (0,)), ((), ())),
                        preferred_element_type=F32)
    hm = hf_ref[...].astype(F32) + hb_ref[...].astype(F32)
    heads = []
    for hd in range(ML_HEADS):
        blk = slice(hd * ML_V, (hd + 1) * ML_V)
        heads.append(_rms(hm[:, blk], hn_ref[:, blk]))
    hm = jnp.concatenate(heads, axis=1) * mlo_ref[...]
    bm = _dot(hm.astype(BF16), wbm_ref[...])
    g = gates_ref[...]
    mixed = _dot((g[:, :D_MODEL] * a + g[:, D_MODEL:] * bm).astype(BF16), wout_ref[...])
    o_ref[...] = h_ref[...] + _rms(mixed, post_ref[...])


def _merge(h, att, hf, hb, mlo, gates, hn, wba, wbm, wout, post, tm=1024):
    s, d = h.shape
    row = lambda w: pl.BlockSpec((tm, w), lambda i: (i, 0))
    return pl.pallas_call(
        _merge_kernel, grid=(s // tm,),
        in_specs=[row(d), pl.BlockSpec((att.shape[0], tm), lambda i: (0, i)),
                  row(ML_OUT), row(ML_OUT), row(ML_OUT),
                  row(2 * d), _const_spec(hn.shape), _const_spec(wba.shape),
                  _const_spec(wbm.shape), _const_spec(wout.shape), _const_spec(post.shape)],
        out_specs=row(d),
        out_shape=jax.ShapeDtypeStruct((s, d), F32),
        compiler_params=_params(("parallel",)),
        name="merge",
    )(h, att, hf, hb, mlo, gates, hn, wba, wbm, wout, post)


def _pad_heads(w, nh, width):
    r = w.shape[0]
    w = w.reshape(r, nh, width)
    return jnp.pad(w, ((0, 0), (0, 0), (0, HEAD_PAD - width))).reshape(r, nh * HEAD_PAD)


def _permute_w_in(w_in):
    d = w_in.shape[0]
    small = jnp.zeros((LANES, d), w_in.dtype)
    small = small.at[0:N_GATE].set(w_in[:, _O_MLI:_O_MLF].T)
    small = small.at[N_GATE:2 * N_GATE].set(w_in[:, _O_MLF:_O_MLO].T)
    small = small.at[ROPE_LO:ROPE_LO + MLA_ROPE].set(w_in[:, _O_KR:_O_MLQ].T)
    main = jnp.concatenate([w_in[:, _O_CQ:_O_KR].astype(BF16), w_in[:, _O_MLQ:_O_MLI].astype(BF16),
                            w_in[:, _O_MLO:_O_END].astype(BF16)], axis=1)
    return main, small.astype(BF16)


def kernel(x, p, positions, ffn1_pre_norm, ffn1_post_norm, ffn1_w_gate, ffn1_w_up, ffn1_w_down, mix_pre_norm, mix_post_norm, w_in, mla_q_norm, mla_kv_norm, mla_w_uq, mla_w_uk, mla_w_uv, ml_conv_w, ml_conv_b, ml_i_bias, ml_f_bias, ml_head_norm, w_branch_mla, w_branch_ml, w_out, ffn2_pre_norm, ffn2_post_norm, ffn2_w_gate, ffn2_w_up, ffn2_w_down, ple_pre_norm, ple_post_norm, ple_w_proj, ple_w_gate):
    depth, batch = p.shape[0], x.shape[0]
    assert depth == 1 and batch == 1, "kernel is specialised to DEPTH == 1, BATCH == 1"
    bf = lambda w: w.astype(BF16)
    h = x[0]
    pos = positions

    inv_freq = ROPE_THETA ** (-jnp.arange(0, MLA_ROPE, 2, dtype=F32) / MLA_ROPE)
    invf = inv_freq[:, None]

    i = 0
    h = _ffn(h, ffn1_pre_norm[i][None], ffn1_post_norm[i][None],
             ffn1_w_gate[i], ffn1_w_up[i], ffn1_w_down[i])

    vone = jnp.zeros((MLA_HEADS, HEAD_PAD), F32).at[:, MLA_V].set(1.0).reshape(-1, 1)
    win_main, win_small = _permute_w_in(w_in[i])
    gate_bias = jnp.concatenate([ml_i_bias[i].reshape(-1), ml_f_bias[i].reshape(-1)])[:, None]
    qt, k, vt, mlqk, mlv, gate_t, cum_t, mlo, gates = _proj(
        h, pos, invf, mix_pre_norm[i][None], win_main, win_small,
        gate_bias, mla_q_norm[i][None], mla_kv_norm[i][None],
        bf(_pad_heads(mla_w_uq[i], MLA_HEADS, MLA_NOPE + MLA_ROPE)).T,
        bf(_pad_heads(mla_w_uk[i], MLA_HEADS, MLA_NOPE)),
        bf(_pad_heads(mla_w_uv[i], MLA_HEADS, MLA_V)).T, vone)

    att = _attn(qt, k, vt)

    conv_w = jnp.pad(ml_conv_w[i], ((0, 8 - CONV_W), (0, 0)))
    conv_b = ml_conv_b[i][None]
    q_conv, kt_conv = _conv(mlqk, conv_w, conv_b)

    hf, hb = _mlstm(q_conv, kt_conv, mlv, gate_t, cum_t)

    h = _merge(h, att, hf, hb, mlo, gates, ml_head_norm[i][None], bf(w_branch_mla[i]),
               bf(w_branch_ml[i]), bf(w_out[i]), mix_post_norm[i][None])

    h = _ffn(h, ffn2_pre_norm[i][None], ffn2_post_norm[i][None],
             ffn2_w_gate[i], ffn2_w_up[i], ffn2_w_down[i],
             ple=(p[i, 0], ple_pre_norm[i][None], ple_post_norm[i][None],
                  bf(ple_w_proj[i]), bf(ple_w_gate[i])))
    return h[None]
```
